```python
import jax
import jax.numpy as jnp
from jax import lax
import numpy as np

D_MODEL = 2048
BATCH = 4
SEQ = 2048
DEPTH = 2

HEAD_DIM = 128
MLA_HEADS = 4
MLA_Q_RANK = 512
MLA_KV_RANK = 256
MLA_NOPE = 128
MLA_ROPE = 64
MLA_V = 128
MOBA_HEADS = 4
MOBA_BLOCK = 256
MOBA_TOPK = 3
MOBA_Q_CHUNK = 32
DIL_HEADS = 8
DIL_PATTERNS = ((128, 1), (512, 4), (2048, 16))
D_FF = 5632
ROPE_THETA = 10000.0
NORM_EPS = 1e-6
ATTN_BLOCK = 128
NEG_INF = -1e30

MLA_IN = MLA_Q_RANK + MLA_KV_RANK + MLA_ROPE
MOBA_IN = 3 * MOBA_HEADS * HEAD_DIM
DIL_IN = 3 * DIL_HEADS * HEAD_DIM
IN_WIDTH = MLA_IN + MOBA_IN + DIL_IN
MIX_WIDTH = MLA_HEADS * MLA_V + MOBA_HEADS * HEAD_DIM + DIL_HEADS * HEAD_DIM

kernel_name = 'hybrid_mla_moba_dilated_macaron'


def rms_norm(x, g):
    xf = x.astype(jnp.float32)
    y = xf * lax.rsqrt(jnp.mean(xf * xf, axis=-1, keepdims=True) + NORM_EPS)
    return (y * g.astype(jnp.float32)).astype(x.dtype)


def rope_tables(seq, dim):
    inv = ROPE_THETA ** (-jnp.arange(0, dim, 2, dtype=jnp.float32) / dim)
    ang = jnp.arange(seq, dtype=jnp.float32)[:, None] * inv[None, :]
    return jnp.cos(ang), jnp.sin(ang)


def apply_rope(x, cos, sin):
    x1, x2 = jnp.split(x, 2, axis=-1)
    c = cos.astype(x.dtype)
    s = sin.astype(x.dtype)
    return jnp.concatenate([x1 * c - x2 * s, x2 * c + x1 * s], axis=-1)


def swiglu(x, w_gate, w_up, w_down):
    return (jax.nn.silu(x @ w_gate) * (x @ w_up)) @ w_down


def split_heads(t, n_heads):
    b, s, _ = t.shape
    return t.reshape(b, s, n_heads, -1).transpose(0, 2, 1, 3)


def merge_heads(t):
    b, h, s, d = t.shape
    return t.transpose(0, 2, 1, 3).reshape(b, s, h * d)


def mla_attention(p, g_q, g_kv, w_uq, w_uk, w_uv, cos_r, sin_r):
    b, s, _ = p.shape
    c_q, c_kv, k_r = jnp.split(p, [MLA_Q_RANK, MLA_Q_RANK + MLA_KV_RANK], axis=-1)
    c_q = rms_norm(c_q, g_q)
    c_kv = rms_norm(c_kv, g_kv)
    q = split_heads(c_q @ w_uq, MLA_HEADS)
    q_nope = q[..., :MLA_NOPE]
    q_rope = apply_rope(q[..., MLA_NOPE:], cos_r, sin_r)
    k_nope = split_heads(c_kv @ w_uk, MLA_HEADS)
    v = split_heads(c_kv @ w_uv, MLA_HEADS)
    k_rope = apply_rope(k_r, cos_r, sin_r)
    scale = (MLA_NOPE + MLA_ROPE) ** -0.5
    nb = s // ATTN_BLOCK

    def to_blocks(t):
        return t.reshape(b, MLA_HEADS, nb, ATTN_BLOCK, t.shape[-1]).transpose(2, 0, 1, 3, 4)

    kpos = jnp.arange(s)

    def one_block(args):
        i, qn, qr = args
        sc = (jnp.einsum('bhqd,bhkd->bhqk', qn, k_nope)
              + jnp.einsum('bhqd,bkd->bhqk', qr, k_rope)).astype(jnp.float32) * scale
        qpos = i * ATTN_BLOCK + jnp.arange(ATTN_BLOCK)
        sc = jnp.where(kpos[None, :] <= qpos[:, None], sc, NEG_INF)
        pr = jax.nn.softmax(sc, axis=-1).astype(v.dtype)
        return jnp.einsum('bhqk,bhkd->bhqd', pr, v)

    out = lax.map(one_block, (jnp.arange(nb), to_blocks(q_nope), to_blocks(q_rope)))
    return out.transpose(1, 0, 3, 2, 4).reshape(b, s, MLA_HEADS * MLA_V)


def moba_attention(q, k, v):
    b, h, s, d = q.shape
    nblk = -(-s // MOBA_BLOCK)
    pad = nblk * MOBA_BLOCK - s
    kp = jnp.pad(k, ((0, 0), (0, 0), (0, pad), (0, 0)))
    vp = jnp.pad(v, ((0, 0), (0, 0), (0, pad), (0, 0)))
    kb = kp.reshape(b, h, nblk, MOBA_BLOCK, d)
    vb = vp.reshape(b, h, nblk, MOBA_BLOCK, d)
    k_mean = jnp.mean(kb.astype(jnp.float32), axis=3)
    topk = min(MOBA_TOPK, nblk - 1)
    scale = d ** -0.5
    nq = s // MOBA_Q_CHUNK
    q_chunks = q.reshape(b, h, nq, MOBA_Q_CHUNK, d).transpose(2, 0, 1, 3, 4)
    blk_ids = jnp.arange(nblk)
    take_blocks = jax.vmap(jax.vmap(lambda blocks, idx: blocks[idx]))

    def one_chunk(args):
        i, qi = args
        q0 = i * MOBA_Q_CHUNK
        own = q0 // MOBA_BLOCK
        qpos = q0 + jnp.arange(MOBA_Q_CHUNK)
        k_own = lax.dynamic_slice_in_dim(kp, own * MOBA_BLOCK, MOBA_BLOCK, axis=2)
        v_own = lax.dynamic_slice_in_dim(vp, own * MOBA_BLOCK, MOBA_BLOCK, axis=2)
        kpos = own * MOBA_BLOCK + jnp.arange(MOBA_BLOCK)
        s_own = jnp.einsum('bhqd,bhkd->bhqk', qi, k_own).astype(jnp.float32) * scale
        s_own = jnp.where(kpos[None, :] <= qpos[:, None], s_own, NEG_INF)
        if topk == 0:
            pr = jax.nn.softmax(s_own, axis=-1).astype(v.dtype)
            return jnp.einsum('bhqk,bhkd->bhqd', pr, v_own)
        gate = jnp.einsum('bhqd,bhnd->bhqn', qi.astype(jnp.float32), k_mean)
        gate = jnp.where(blk_ids < own, gate, NEG_INF)
        _, idx = lax.top_k(gate, topk)
        valid = idx < own
        k_sel = take_blocks(kb, idx)
        v_sel = take_blocks(vb, idx)
        s_sel = jnp.einsum('bhqd,bhqnkd->bhqnk', qi, k_sel).astype(jnp.float32) * scale
        n_sel = topk * MOBA_BLOCK
        s_sel = jnp.where(valid[..., None], s_sel, NEG_INF).reshape(b, h, MOBA_Q_CHUNK, n_sel)
        pr = jax.nn.softmax(jnp.concatenate([s_sel, s_own], axis=-1), axis=-1).astype(v.dtype)
        return (jnp.einsum('bhqk,bhqkd->bhqd', pr[..., :n_sel],
                           v_sel.reshape(b, h, MOBA_Q_CHUNK, n_sel, d))
                + jnp.einsum('bhqk,bhkd->bhqd', pr[..., n_sel:], v_own))

    out = lax.map(one_chunk, (jnp.arange(nq), q_chunks))
    return out.transpose(1, 2, 0, 3, 4).reshape(b, h, s, d)


def dilated_attention(q, k, v):
    b, h, s, d = q.shape
    scale = d ** -0.5
    ms, dens, accs = [], [], []
    for window, dil in DIL_PATTERNS:
        n = window // dil
        sub_len = s // dil
        nb = -(-sub_len // n)
        lp = nb * n

        def to_sub(t):
            t = t.reshape(b, h, sub_len, dil, d).transpose(0, 1, 3, 2, 4)
            t = jnp.pad(t, ((0, 0), (0, 0), (0, 0), (0, lp - sub_len), (0, 0)))
            return t.reshape(b, h, dil, nb, n, d)

        def band(t):
            prev = jnp.pad(t, ((0, 0), (0, 0), (0, 0), (1, 0), (0, 0), (0, 0)))[:, :, :, :-1]
            return jnp.concatenate([prev, t], axis=4)

        qs = to_sub(q)
        kband = band(to_sub(k))
        vband = band(to_sub(v))
        sc = jnp.einsum('bhrnqd,bhrnkd->bhrnqk', qs, kband).astype(jnp.float32) * scale
        qi = jnp.arange(n)[:, None]
        kj = jnp.arange(2 * n)[None, :]
        dist = qi + n - kj
        key_idx = jnp.arange(nb)[:, None, None] * n - n + kj[None]
        mask = (dist >= 0) & (dist <= n) & (key_idx >= 0)
        sc = jnp.where(mask, sc, NEG_INF)
        m = jnp.max(sc, axis=-1, keepdims=True)
        e = jnp.exp(sc - m)
        den = jnp.sum(e, axis=-1, keepdims=True)
        acc = jnp.einsum('bhrnqk,bhrnkd->bhrnqd', e, vband.astype(jnp.float32))

        def from_sub(t):
            last = t.shape[-1]
            t = t.reshape(b, h, dil, lp, last)[:, :, :, :sub_len]
            return t.transpose(0, 1, 3, 2, 4).reshape(b, h, s, last)

        ms.append(from_sub(m))
        dens.append(from_sub(den))
        accs.append(from_sub(acc))
    m_all = jnp.max(jnp.stack(ms, axis=0), axis=0)
    num = sum(jnp.exp(mp - m_all) * ap for mp, ap in zip(ms, accs))
    tot = sum(jnp.exp(mp - m_all) * dp for mp, dp in zip(ms, dens))
    return (num / tot).astype(q.dtype)


def hybrid_layer(x, ln_ffn1, w_ffn1_gate, w_ffn1_up, w_ffn1_down, ln_mix, w_in,
                 g_mla_q, g_mla_kv, w_mla_uq, w_mla_uk, w_mla_uv, w_out,
                 ln_ffn2, w_ffn2_gate, w_ffn2_up, w_ffn2_down,
                 cos_h, sin_h, cos_r, sin_r):
    x = x + 0.5 * swiglu(rms_norm(x, ln_ffn1), w_ffn1_gate, w_ffn1_up, w_ffn1_down)
    hmix = rms_norm(x, ln_mix)
    proj = hmix @ w_in
    p_mla, p_moba, p_dil = jnp.split(proj, [MLA_IN, MLA_IN + MOBA_IN], axis=-1)
    o_mla = mla_attention(p_mla, g_mla_q, g_mla_kv, w_mla_uq, w_mla_uk, w_mla_uv, cos_r, sin_r)
    q, k, v = (split_heads(t, MOBA_HEADS) for t in jnp.split(p_moba, 3, axis=-1))
    o_moba = merge_heads(moba_attention(apply_rope(q, cos_h, sin_h), apply_rope(k, cos_h, sin_h), v))
    q, k, v = (split_heads(t, DIL_HEADS) for t in jnp.split(p_dil, 3, axis=-1))
    o_dil = merge_heads(dilated_attention(apply_rope(q, cos_h, sin_h), apply_rope(k, cos_h, sin_h), v))
    x = x + jnp.concatenate([o_mla, o_moba, o_dil], axis=-1) @ w_out
    x = x + 0.5 * swiglu(rms_norm(x, ln_ffn2), w_ffn2_gate, w_ffn2_up, w_ffn2_down)
    return x


def setup_inputs(seed: int = 0) -> dict:
    key = jax.random.key(seed)
    ks = jax.random.split(key, 18)

    def w(k, shape, fan_in):
        return jax.random.normal(k, shape, jnp.float32) * (fan_in ** -0.5)

    def g(k, shape):
        return 1.0 + 0.02 * jax.random.normal(k, shape, jnp.float32)

    return {
        'x': jax.random.normal(ks[0], (BATCH, SEQ, D_MODEL), jnp.float32),
        'ln_ffn1': g(ks[1], (DEPTH, D_MODEL)),
        'w_ffn1_gate': w(ks[2], (DEPTH, D_MODEL, D_FF), D_MODEL),
        'w_ffn1_up': w(ks[3], (DEPTH, D_MODEL, D_FF), D_MODEL),
        'w_ffn1_down': w(ks[4], (DEPTH, D_FF, D_MODEL), D_FF),
        'ln_mix': g(ks[5], (DEPTH, D_MODEL)),
        'w_in': w(ks[6], (DEPTH, D_MODEL, IN_WIDTH), D_MODEL),
        'g_mla_q': g(ks[7], (DEPTH, MLA_Q_RANK)),
        'g_mla_kv': g(ks[8], (DEPTH, MLA_KV_RANK)),
        'w_mla_uq': w(ks[9], (DEPTH, MLA_Q_RANK, MLA_HEADS * (MLA_NOPE + MLA_ROPE)), MLA_Q_RANK),
        'w_mla_uk': w(ks[10], (DEPTH, MLA_KV_RANK, MLA_HEADS * MLA_NOPE), MLA_KV_RANK),
        'w_mla_uv': w(ks[11], (DEPTH, MLA_KV_RANK, MLA_HEADS * MLA_V), MLA_KV_RANK),
        'w_out': w(ks[12], (DEPTH, MIX_WIDTH, D_MODEL), MIX_WIDTH),
        'ln_ffn2': g(ks[13], (DEPTH, D_MODEL)),
        'w_ffn2_gate': w(ks[14], (DEPTH, D_MODEL, D_FF), D_MODEL),
        'w_ffn2_up': w(ks[15], (DEPTH, D_MODEL, D_FF), D_MODEL),
        'w_ffn2_down': w(ks[16], (DEPTH, D_FF, D_MODEL), D_FF),
        'ln_final': g(ks[17], (D_MODEL,)),
    }


def reference(x, ln_ffn1, w_ffn1_gate, w_ffn1_up, w_ffn1_down, ln_mix, w_in,
              g_mla_q, g_mla_kv, w_mla_uq, w_mla_uk, w_mla_uv, w_out,
              ln_ffn2, w_ffn2_gate, w_ffn2_up, w_ffn2_down, ln_final):
    s = x.shape[1]
    cos_h, sin_h = rope_tables(s, HEAD_DIM)
    cos_r, sin_r = rope_tables(s, MLA_ROPE)
    for l in range(DEPTH):
        x = hybrid_layer(x, ln_ffn1[l], w_ffn1_gate[l], w_ffn1_up[l], w_ffn1_down[l],
                         ln_mix[l], w_in[l], g_mla_q[l], g_mla_kv[l],
                         w_mla_uq[l], w_mla_uk[l], w_mla_uv[l], w_out[l],
                         ln_ffn2[l], w_ffn2_gate[l], w_ffn2_up[l], w_ffn2_down[l],
                         cos_h, sin_h, cos_r, sin_r)
    return rms_norm(x, ln_final)
```

```python
import functools

import jax
import jax.numpy as jnp
from jax import lax
from jax.experimental import pallas as pl
from jax.experimental.pallas import tpu as pltpu

D_MODEL = 2048
BATCH = 4
SEQ = 2048
DEPTH = 2
TOKENS = BATCH * SEQ

HEAD_DIM = 128
MLA_HEADS = 4
MLA_Q_RANK = 512
MLA_KV_RANK = 256
MLA_NOPE = 128
MLA_ROPE = 64
MLA_V = 128
MLA_QK_PAD = 256
MOBA_HEADS = 4
MOBA_BLOCK = 256
MOBA_TOPK = 3
MOBA_NBLK = SEQ // MOBA_BLOCK
DIL_HEADS = 8
DIL_PATTERNS = ((128, 1), (512, 4), (2048, 16))
DIL_N = 128
D_FF = 5632
ROPE_THETA = 10000.0
NORM_EPS = 1e-6
NEG_INF = -1e30

MLA_IN = MLA_Q_RANK + MLA_KV_RANK + MLA_ROPE
MLA_IN_PAD = MLA_Q_RANK + MLA_KV_RANK + 128
MOBA_IN = 3 * MOBA_HEADS * HEAD_DIM
DIL_IN = 3 * DIL_HEADS * HEAD_DIM
MIX_WIDTH = MLA_HEADS * MLA_V + MOBA_HEADS * HEAD_DIM + DIL_HEADS * HEAD_DIM

LANES = 128
MIB = 1024 * 1024

FFN_TM = 512
FFN_TF = 512
PROJ_TM = 1024
PROJ_TN = 512
MLA_PREP_TM = 512
ATT_TQ = 256
OUT_TM = 256


def _cparams(semantics, vmem_mib):
    return pltpu.CompilerParams(dimension_semantics=semantics,
                                vmem_limit_bytes=vmem_mib * MIB)


def _rms(x, g):
    ms = jnp.mean(x * x, axis=-1, keepdims=True)
    return x * lax.rsqrt(ms + NORM_EPS) * g


def _lane_col(a, j, lane):
    return jnp.sum(jnp.where(lane == j, a, 0.0), axis=1, keepdims=True)


def _dot_nt(a, b):
    return lax.dot_general(a, b, (((1,), (1,)), ((), ())),
                           preferred_element_type=jnp.float32)


def _ffn_kernel(x_ref, ln_ref, wg_ref, wu_ref, wd_ref, *rest, final_norm):
    if final_norm:
        lnf_ref, o_ref, h_ref, acc_ref = rest
    else:
        o_ref, h_ref, acc_ref = rest
    j = pl.program_id(1)

    @pl.when(j == 0)
    def _():
        h_ref[...] = _rms(x_ref[...], ln_ref[...]).astype(jnp.bfloat16)
        acc_ref[...] = jnp.zeros_like(acc_ref)

    h = h_ref[...]
    g = jnp.dot(h, wg_ref[...], preferred_element_type=jnp.float32)
    u = jnp.dot(h, wu_ref[...], preferred_element_type=jnp.float32)
    a = (g * (1.0 / (1.0 + jnp.exp(-g))) * u).astype(jnp.bfloat16)
    acc_ref[...] += jnp.dot(a, wd_ref[...], preferred_element_type=jnp.float32)

    @pl.when(j == pl.num_programs(1) - 1)
    def _():
        y = x_ref[...] + 0.5 * acc_ref[...]
        if final_norm:
            y = _rms(y, lnf_ref[...])
        o_ref[...] = y


def _ffn(x, ln, wg, wu, wd, ln_final=None):
    final_norm = ln_final is not None
    tm, tf = FFN_TM, FFN_TF
    in_specs = [
        pl.BlockSpec((tm, D_MODEL), lambda i, j: (i, 0)),
        pl.BlockSpec((1, D_MODEL), lambda i, j: (0, 0)),
        pl.BlockSpec((D_MODEL, tf), lambda i, j: (0, j)),
        pl.BlockSpec((D_MODEL, tf), lambda i, j: (0, j)),
        pl.BlockSpec((tf, D_MODEL), lambda i, j: (j, 0)),
    ]
    args = [x, ln, wg, wu, wd]
    if final_norm:
        in_specs.append(pl.BlockSpec((1, D_MODEL), lambda i, j: (0, 0)))
        args.append(ln_final)
    return pl.pallas_call(
        functools.partial(_ffn_kernel, final_norm=final_norm),
        grid=(TOKENS // tm, D_FF // tf),
        in_specs=in_specs,
        out_specs=pl.BlockSpec((tm, D_MODEL), lambda i, j: (i, 0)),
        out_shape=jax.ShapeDtypeStruct((TOKENS, D_MODEL), jnp.float32),
        scratch_shapes=[pltpu.VMEM((tm, D_MODEL), jnp.bfloat16),
                        pltpu.VMEM((tm, D_MODEL), jnp.float32)],
        compiler_params=_cparams(("parallel", "arbitrary"), 48),
        name="ffn_final" if final_norm else "ffn",
    )(*args)


def _proj_kernel(x_ref, ln_ref, w_ref, cos_ref, sin_ref, o_ref, h_ref, *, n_rope_blocks):
    j = pl.program_id(1)

    @pl.when(j == 0)
    def _():
        h_ref[...] = _rms(x_ref[...], ln_ref[...]).astype(jnp.bfloat16)

    y = jnp.dot(h_ref[...], w_ref[...], preferred_element_type=jnp.float32)
    groups = y.shape[1] // LANES

    if n_rope_blocks > 0:
        @pl.when(j < n_rope_blocks)
        def _():
            c = cos_ref[0]
            s = sin_ref[0]
            for gidx in range(groups):
                yg = y[:, gidx * LANES:(gidx + 1) * LANES]
                o_ref[:, gidx * LANES:(gidx + 1) * LANES] = (
                    yg * c + pltpu.roll(yg, LANES // 2, 1) * s).astype(o_ref.dtype)

    @pl.when(j >= n_rope_blocks)
    def _():
        o_ref[...] = y.astype(o_ref.dtype)


def _proj(x, ln, w, cos_tab, sin_tab, *, tn, rope_sections, out_dtype, name):
    n = w.shape[1]
    tm = PROJ_TM
    nj = n // tn
    blocks_per_section = max(nj // 3, 1)
    n_rope_blocks = rope_sections * blocks_per_section
    pos_blocks = SEQ // tm

    def tab_map(i, j):
        return (jnp.minimum(j // blocks_per_section, 1), i % pos_blocks, 0)

    return pl.pallas_call(
        functools.partial(_proj_kernel, n_rope_blocks=n_rope_blocks),
        grid=(TOKENS // tm, nj),
        in_specs=[
            pl.BlockSpec((tm, D_MODEL), lambda i, j: (i, 0)),
            pl.BlockSpec((1, D_MODEL), lambda i, j: (0, 0)),
            pl.BlockSpec((D_MODEL, tn), lambda i, j: (0, j)),
            pl.BlockSpec((1, tm, LANES), tab_map),
            pl.BlockSpec((1, tm, LANES), tab_map),
        ],
        out_specs=pl.BlockSpec((tm, tn), lambda i, j: (i, j)),
        out_shape=jax.ShapeDtypeStruct((TOKENS, n), out_dtype),
        scratch_shapes=[pltpu.VMEM((tm, D_MODEL), jnp.bfloat16)],
        compiler_params=_cparams(("parallel", "arbitrary"), 48),
        name=name,
    )(x, ln, w, cos_tab, sin_tab)


def _mla_prep_kernel(p_ref, gq_ref, gkv_ref, wuq_ref, wuk_ref, wuv_ref, cos_ref, sin_ref,
                     q_ref, k_ref, v_ref):
    scale = (MLA_NOPE + MLA_ROPE) ** -0.5
    c = cos_ref[...]
    s = sin_ref[...]

    def rope(t):
        return t * c + pltpu.roll(t, LANES // 2, 1) * s

    cq = _rms(p_ref[:, :MLA_Q_RANK], gq_ref[...]).astype(jnp.bfloat16)
    q = jnp.dot(cq, wuq_ref[...], preferred_element_type=jnp.float32)
    ckv = _rms(p_ref[:, MLA_Q_RANK:MLA_Q_RANK + MLA_KV_RANK], gkv_ref[...]).astype(jnp.bfloat16)
    kn = jnp.dot(ckv, wuk_ref[...], preferred_element_type=jnp.float32)
    v_ref[...] = jnp.dot(ckv, wuv_ref[...],
                         preferred_element_type=jnp.float32).astype(jnp.bfloat16)
    kr = rope(p_ref[:, MLA_Q_RANK + MLA_KV_RANK:]).astype(jnp.bfloat16)
    for h in range(MLA_HEADS):
        b0 = h * MLA_QK_PAD
        q_ref[:, b0:b0 + LANES] = (q[:, b0:b0 + LANES] * scale).astype(jnp.bfloat16)
        q_ref[:, b0 + LANES:b0 + 2 * LANES] = (
            rope(q[:, b0 + LANES:b0 + 2 * LANES]) * scale).astype(jnp.bfloat16)
        k_ref[:, b0:b0 + LANES] = kn[:, h * LANES:(h + 1) * LANES].astype(jnp.bfloat16)
        k_ref[:, b0 + LANES:b0 + 2 * LANES] = kr


def _mla_prep(p, gq, gkv, wuq, wuk, wuv, cos_r, sin_r):
    tm = MLA_PREP_TM
    pos_blocks = SEQ // tm
    qk_w = MLA_HEADS * MLA_QK_PAD
    v_w = MLA_HEADS * MLA_V

    def full(shape):
        return pl.BlockSpec(shape, lambda i: (0, 0))

    return pl.pallas_call(
        _mla_prep_kernel,
        grid=(TOKENS // tm,),
        in_specs=[
            pl.BlockSpec((tm, MLA_IN_PAD), lambda i: (i, 0)),
            full((1, MLA_Q_RANK)), full((1, MLA_KV_RANK)),
            full((MLA_Q_RANK, qk_w)), full((MLA_KV_RANK, v_w)), full((MLA_KV_RANK, v_w)),
            pl.BlockSpec((tm, LANES), lambda i: (i % pos_blocks, 0)),
            pl.BlockSpec((tm, LANES), lambda i: (i % pos_blocks, 0)),
        ],
        out_specs=[
            pl.BlockSpec((tm, qk_w), lambda i: (i, 0)),
            pl.BlockSpec((tm, qk_w), lambda i: (i, 0)),
            pl.BlockSpec((tm, v_w), lambda i: (i, 0)),
        ],
        out_shape=[
            jax.ShapeDtypeStruct((TOKENS, qk_w), jnp.bfloat16),
            jax.ShapeDtypeStruct((TOKENS, qk_w), jnp.bfloat16),
            jax.ShapeDtypeStruct((TOKENS, v_w), jnp.bfloat16),
        ],
        compiler_params=_cparams(("parallel",), 32),
        name="mla_prep",
    )(p, gq, gkv, wuq, wuk, wuv, cos_r, sin_r)


def _mla_attn_kernel(q_ref, k_ref, v_ref, o_ref):
    i = pl.program_id(2)
    s = _dot_nt(q_ref[...], k_ref[...])
    row = lax.broadcasted_iota(jnp.int32, s.shape, 0) + i * ATT_TQ
    col = lax.broadcasted_iota(jnp.int32, s.shape, 1)
    s = jnp.where(col <= row, s, NEG_INF)
    m = jnp.max(s, axis=1, keepdims=True)
    e = jnp.exp(s - m)
    l = jnp.sum(e, axis=1, keepdims=True)
    acc = jnp.dot(e.astype(jnp.bfloat16), v_ref[...], preferred_element_type=jnp.float32)
    o_ref[...] = (acc / l).astype(o_ref.dtype)


def _mla_attn(q, k, v):
    qb = SEQ // ATT_TQ
    return pl.pallas_call(
        _mla_attn_kernel,
        grid=(BATCH, MLA_HEADS, qb),
        in_specs=[
            pl.BlockSpec((ATT_TQ, MLA_QK_PAD), lambda b, h, i: (b * qb + i, h)),
            pl.BlockSpec((SEQ, MLA_QK_PAD), lambda b, h, i: (b, h)),
            pl.BlockSpec((SEQ, MLA_V), lambda b, h, i: (b, h)),
        ],
        out_specs=pl.BlockSpec((ATT_TQ, MLA_V), lambda b, h, i: (b * qb + i, h)),
        out_shape=jax.ShapeDtypeStruct((TOKENS, MLA_HEADS * MLA_V), jnp.bfloat16),
        compiler_params=_cparams(("parallel", "parallel", "arbitrary"), 32),
        name="mla_attn",
    )(q, k, v)


def _moba_attn_kernel(q_ref, k_ref, v_ref, o_ref, km_ref):
    i = pl.program_id(2)

    @pl.when(i == 0)
    def _():
        rid = lax.broadcasted_iota(jnp.int32, (LANES, HEAD_DIM), 0)
        km = jnp.zeros((LANES, HEAD_DIM), jnp.float32)
        for j in range(MOBA_NBLK):
            kj = k_ref[j * MOBA_BLOCK:(j + 1) * MOBA_BLOCK, :].astype(jnp.float32)
            mean_j = jnp.sum(kj, axis=0, keepdims=True) * (1.0 / MOBA_BLOCK)
            km = jnp.where(rid == j, mean_j, km)
        km_ref[...] = km

    q = q_ref[...]
    km = km_ref[...]
    km_hi = km.astype(jnp.bfloat16)
    km_lo = (km - km_hi.astype(jnp.float32)).astype(jnp.bfloat16)
    gate = _dot_nt(q, km_hi) + _dot_nt(q, km_lo)

    lane = lax.broadcasted_iota(jnp.int32, gate.shape, 1)
    beats = jnp.zeros(gate.shape, jnp.float32)
    for jp in range(MOBA_NBLK):
        cj = _lane_col(gate, jp, lane)
        wins = (cj > gate) | ((cj == gate) & (lane > jp))
        beats = beats + jnp.where(wins, jnp.where(i > jp, 1.0, 0.0), 0.0)
    sel = jnp.where((lane < i) & (beats < MOBA_TOPK), 1.0, 0.0)

    s = _dot_nt(q, k_ref[...])
    r = lax.broadcasted_iota(jnp.int32, (ATT_TQ, MOBA_BLOCK), 0)
    c = lax.broadcasted_iota(jnp.int32, (ATT_TQ, MOBA_BLOCK), 1)
    parts = []
    for j in range(MOBA_NBLK):
        tri = c <= r - jnp.where(i == j, 0, 2 * MOBA_BLOCK)
        keep = (_lane_col(sel, j, lane) > 0.5) | tri
        parts.append(jnp.where(keep, s[:, j * MOBA_BLOCK:(j + 1) * MOBA_BLOCK], NEG_INF))
    s = jnp.concatenate(parts, axis=1)
    m = jnp.max(s, axis=1, keepdims=True)
    e = jnp.exp(s - m)
    l = jnp.sum(e, axis=1, keepdims=True)
    acc = jnp.dot(e.astype(jnp.bfloat16), v_ref[...], preferred_element_type=jnp.float32)
    o_ref[...] = (acc / l).astype(o_ref.dtype)


def _moba_attn(qkv):
    qb = SEQ // ATT_TQ
    return pl.pallas_call(
        _moba_attn_kernel,
        grid=(BATCH, MOBA_HEADS, qb),
        in_specs=[
            pl.BlockSpec((ATT_TQ, HEAD_DIM), lambda b, h, i: (b * qb + i, h)),
            pl.BlockSpec((SEQ, HEAD_DIM), lambda b, h, i: (b, MOBA_HEADS + h)),
            pl.BlockSpec((SEQ, HEAD_DIM), lambda b, h, i: (b, 2 * MOBA_HEADS + h)),
        ],
        out_specs=pl.BlockSpec((ATT_TQ, HEAD_DIM), lambda b, h, i: (b * qb + i, h)),
        out_shape=jax.ShapeDtypeStruct((TOKENS, MOBA_HEADS * HEAD_DIM), jnp.bfloat16),
        scratch_shapes=[pltpu.VMEM((LANES, HEAD_DIM), jnp.float32)],
        compiler_params=_cparams(("parallel", "parallel", "arbitrary"), 32),
        name="moba_attn",
    )(qkv, qkv, qkv)


def _dil_attn_kernel(q_ref, k_ref, v_ref, acc_ref, st_ref, *, sub_len):
    n = DIL_N
    nb = sub_len // n
    r = lax.broadcasted_iota(jnp.int32, (n, n), 0)
    c = lax.broadcasted_iota(jnp.int32, (n, n), 1)
    cur_mask = c <= r
    lane = lax.broadcasted_iota(jnp.int32, (n, LANES), 1)

    def body(qb, carry):
        r0 = pl.multiple_of(qb * n, n)
        p0 = pl.multiple_of(jnp.maximum(qb - 1, 0) * n, n)
        prev_mask = c >= r + jnp.where(qb > 0, 0, 2 * n)
        stats = jnp.zeros((n, LANES), jnp.float32)
        for h in range(DIL_HEADS):
            hs = slice(h * HEAD_DIM, (h + 1) * HEAD_DIM)
            q = q_ref[0, pl.ds(r0, n), hs]
            sc = _dot_nt(q, k_ref[0, pl.ds(r0, n), hs])
            sp = _dot_nt(q, k_ref[0, pl.ds(p0, n), hs])
            sc = jnp.where(cur_mask, sc, NEG_INF)
            sp = jnp.where(prev_mask, sp, NEG_INF)
            m = jnp.max(jnp.maximum(sc, sp), axis=1, keepdims=True)
            ec = jnp.exp(sc - m)
            ep = jnp.exp(sp - m)
            den = jnp.sum(ec + ep, axis=1, keepdims=True)
            acc = (jnp.dot(ec.astype(jnp.bfloat16), v_ref[0, pl.ds(r0, n), hs],
                           preferred_element_type=jnp.float32)
                   + jnp.dot(ep.astype(jnp.bfloat16), v_ref[0, pl.ds(p0, n), hs],
                             preferred_element_type=jnp.float32))
            acc_ref[0, pl.ds(r0, n), hs] = acc
            stats = jnp.where(lane == h, m, stats)
            stats = jnp.where(lane == DIL_HEADS + h, den, stats)
        st_ref[0, pl.ds(r0, n), :] = stats
        return carry

    lax.fori_loop(0, nb, body, 0)


def _dil_attn(qkv, dil):
    sub_len = SEQ // dil
    hw = DIL_HEADS * HEAD_DIM
    view = qkv.reshape(BATCH, sub_len, dil * 3 * hw)

    def in_spec(section):
        return pl.BlockSpec((1, sub_len, hw), lambda b, rho: (b, 0, rho * 3 + section))

    acc, st = pl.pallas_call(
        functools.partial(_dil_attn_kernel, sub_len=sub_len),
        grid=(BATCH, dil),
        in_specs=[in_spec(0), in_spec(1), in_spec(2)],
        out_specs=[
            pl.BlockSpec((1, sub_len, hw), lambda b, rho: (b, 0, rho)),
            pl.BlockSpec((1, sub_len, LANES), lambda b, rho: (b, 0, rho)),
        ],
        out_shape=[
            jax.ShapeDtypeStruct((BATCH, sub_len, dil * hw), jnp.float32),
            jax.ShapeDtypeStruct((BATCH, sub_len, dil * LANES), jnp.float32),
        ],
        compiler_params=_cparams(("parallel", "parallel"), 48),
        name=f"dil_attn_r{dil}",
    )(view, view, view)
    return acc.reshape(TOKENS, hw), st.reshape(TOKENS, LANES)


def _out_proj_kernel(x_ref, mla_ref, moba_ref, a1_ref, a2_ref, a3_ref,
                     s1_ref, s2_ref, s3_ref, w_ref, o_ref):
    accs = (a1_ref, a2_ref, a3_ref)
    stats = (s1_ref[...], s2_ref[...], s3_ref[...])
    lane = lax.broadcasted_iota(jnp.int32, stats[0].shape, 1)
    dil_heads = []
    for h in range(DIL_HEADS):
        hs = slice(h * HEAD_DIM, (h + 1) * HEAD_DIM)
        ms = [_lane_col(st, h, lane) for st in stats]
        dens = [_lane_col(st, DIL_HEADS + h, lane) for st in stats]
        m_all = jnp.maximum(jnp.maximum(ms[0], ms[1]), ms[2])
        ws = [jnp.exp(mp - m_all) for mp in ms]
        num = sum(wp * a[:, hs] for wp, a in zip(ws, accs))
        tot = sum(wp * dp for wp, dp in zip(ws, dens))
        dil_heads.append((num / tot).astype(jnp.bfloat16))
    mix = jnp.concatenate([mla_ref[...], moba_ref[...]] + dil_heads, axis=1)
    o_ref[...] = x_ref[...] + jnp.dot(mix, w_ref[...], preferred_element_type=jnp.float32)


def _out_proj(x, o_mla, o_moba, dil_outs, w_out):
    tm = OUT_TM
    hw = DIL_HEADS * HEAD_DIM
    (a1, s1), (a2, s2), (a3, s3) = dil_outs

    def rows(width):
        return pl.BlockSpec((tm, width), lambda i: (i, 0))

    return pl.pallas_call(
        _out_proj_kernel,
        grid=(TOKENS // tm,),
        in_specs=[rows(D_MODEL), rows(MLA_HEADS * MLA_V), rows(MOBA_HEADS * HEAD_DIM),
                  rows(hw), rows(hw), rows(hw), rows(LANES), rows(LANES), rows(LANES),
                  pl.BlockSpec((MIX_WIDTH, D_MODEL), lambda i: (0, 0))],
        out_specs=rows(D_MODEL),
        out_shape=jax.ShapeDtypeStruct((TOKENS, D_MODEL), jnp.float32),
        compiler_params=_cparams(("parallel",), 48),
        name="out_proj",
    )(x, o_mla, o_moba, a1, a2, a3, s1, s2, s3, w_out)


def _rope_tables():
    pos = jnp.arange(SEQ, dtype=jnp.float32)[:, None]
    inv_h = ROPE_THETA ** (-jnp.arange(0, HEAD_DIM, 2, dtype=jnp.float32) / HEAD_DIM)
    ang = pos * inv_h[None, :]
    cos_h = jnp.concatenate([jnp.cos(ang), jnp.cos(ang)], axis=1)
    sin_h = jnp.concatenate([-jnp.sin(ang), jnp.sin(ang)], axis=1)
    scale = HEAD_DIM ** -0.5
    cos_qk = jnp.stack([cos_h * scale, cos_h])
    sin_qk = jnp.stack([sin_h * scale, sin_h])
    inv_r = ROPE_THETA ** (-jnp.arange(0, MLA_ROPE, 2, dtype=jnp.float32) / MLA_ROPE)
    ang_r = pos * inv_r[None, :]
    z = jnp.zeros_like(ang_r)
    cos_r = jnp.concatenate([jnp.cos(ang_r), z, jnp.cos(ang_r), z], axis=1)
    sin_r = jnp.concatenate([-jnp.sin(ang_r), z, jnp.sin(ang_r), z], axis=1)
    return cos_qk, sin_qk, cos_r, sin_r


def _pad_rope_cols(w):
    half = MLA_ROPE // 2
    z = jnp.zeros(w.shape[:-1] + (half,), w.dtype)
    return jnp.concatenate([w[..., :half], z, w[..., half:], z], axis=-1)


def _prep_layer_weights(w_in, w_uq, w_uk, w_uv, w_out):
    bf = jnp.bfloat16
    w_mla = jnp.concatenate(
        [w_in[:, :MLA_Q_RANK + MLA_KV_RANK], _pad_rope_cols(w_in[:, MLA_Q_RANK + MLA_KV_RANK:MLA_IN])],
        axis=1).astype(bf)
    w_moba = w_in[:, MLA_IN:MLA_IN + MOBA_IN].astype(bf)
    w_dil = w_in[:, MLA_IN + MOBA_IN:].astype(bf)
    uq = w_uq.reshape(MLA_Q_RANK, MLA_HEADS, MLA_NOPE + MLA_ROPE)
    uq = jnp.concatenate([uq[..., :MLA_NOPE], _pad_rope_cols(uq[..., MLA_NOPE:])], axis=-1)
    uq = uq.reshape(MLA_Q_RANK, MLA_HEADS * MLA_QK_PAD).astype(bf)
    return w_mla, w_moba, w_dil, uq, w_uk.astype(bf), w_uv.astype(bf), w_out.astype(bf)


def kernel(x, ln_ffn1, w_ffn1_gate, w_ffn1_up, w_ffn1_down, ln_mix, w_in, g_mla_q, g_mla_kv,
           w_mla_uq, w_mla_uk, w_mla_uv, w_out, ln_ffn2, w_ffn2_gate, w_ffn2_up, w_ffn2_down,
           ln_final):
    bf = jnp.bfloat16
    cos_qk, sin_qk, cos_r, sin_r = _rope_tables()
    xt = x.reshape(TOKENS, D_MODEL)
    for l in range(DEPTH):
        w_mla, w_moba, w_dil, uq, uk, uv, wo = _prep_layer_weights(
            w_in[l], w_mla_uq[l], w_mla_uk[l], w_mla_uv[l], w_out[l])
        xt = _ffn(xt, ln_ffn1[l][None], w_ffn1_gate[l].astype(bf), w_ffn1_up[l].astype(bf),
                  w_ffn1_down[l].astype(bf))
        ln = ln_mix[l][None]
        p_mla = _proj(xt, ln, w_mla, cos_qk, sin_qk, tn=MLA_IN_PAD, rope_sections=0,
                      out_dtype=jnp.float32, name="proj_mla")
        qkv_moba = _proj(xt, ln, w_moba, cos_qk, sin_qk, tn=PROJ_TN, rope_sections=2,
                         out_dtype=bf, name="proj_moba")
        qkv_dil = _proj(xt, ln, w_dil, cos_qk, sin_qk, tn=PROJ_TN, rope_sections=2,
                        out_dtype=bf, name="proj_dil")
        q_mla, k_mla, v_mla = _mla_prep(p_mla, g_mla_q[l][None], g_mla_kv[l][None],
                                        uq, uk, uv, cos_r, sin_r)
        o_mla = _mla_attn(q_mla, k_mla, v_mla)
        o_moba = _moba_attn(qkv_moba)
        dil_outs = [_dil_attn(qkv_dil, dil) for _, dil in DIL_PATTERNS]
        xt = _out_proj(xt, o_mla, o_moba, dil_outs, wo)
        xt = _ffn(xt, ln_ffn2[l][None], w_ffn2_gate[l].astype(bf), w_ffn2_up[l].astype(bf),
                  w_ffn2_down[l].astype(bf),
                  ln_final=ln_final[None] if l == DEPTH - 1 else None)
    return xt.reshape(BATCH, SEQ, D_MODEL)
```

```python
import functools

import numpy as np

import jax
import jax.numpy as jnp
from jax import lax
from jax.experimental import pallas as pl
from jax.experimental.pallas import tpu as pltpu

D_MODEL = 2048
BATCH = 4
SEQ = 2048
DEPTH = 2
TOKENS = BATCH * SEQ

HEAD_DIM = 128
MLA_HEADS = 4
MLA_Q_RANK = 512
MLA_KV_RANK = 256
MLA_NOPE = 128
MLA_ROPE = 64
MLA_V = 128
MLA_QK_PAD = 256
MOBA_HEADS = 4
MOBA_BLOCK = 256
MOBA_TOPK = 3
MOBA_NBLK = SEQ // MOBA_BLOCK
DIL_HEADS = 8
DIL_PATTERNS = ((128, 1), (512, 4), (2048, 16))
D_FF = 5632
ROPE_THETA = 10000.0
NORM_EPS = 1e-6
NEG_INF = -1e30

MLA_IN = MLA_Q_RANK + MLA_KV_RANK + MLA_ROPE
MLA_IN_PAD = MLA_Q_RANK + MLA_KV_RANK + 128
MOBA_IN = 3 * MOBA_HEADS * HEAD_DIM
DIL_IN = 3 * DIL_HEADS * HEAD_DIM
MIX_WIDTH = MLA_HEADS * MLA_V + MOBA_HEADS * HEAD_DIM + DIL_HEADS * HEAD_DIM

LANES = 128
MIB = 1024 * 1024

FFN_TM = 1024
FFN_TF = 256
PROJ_TM = 1024
PROJ_TN = 512
MLA_PREP_TM = 512
ATT_T = 256
ATT_NT = SEQ // ATT_T
OUT_TM = 512

assert ATT_T == MOBA_BLOCK


def _cparams(semantics, vmem_mib):
    return pltpu.CompilerParams(dimension_semantics=semantics,
                                vmem_limit_bytes=vmem_mib * MIB)


def _rms(x, g):
    ms = jnp.mean(x * x, axis=-1, keepdims=True)
    return x * lax.rsqrt(ms + NORM_EPS) * g


def _lane_col(a, j, lane):
    return jnp.sum(jnp.where(lane == j, a, 0.0), axis=1, keepdims=True)


def _dot_nt(a, b):
    return lax.dot_general(a, b, (((1,), (1,)), ((), ())),
                           preferred_element_type=jnp.float32)


def _cat(parts):
    return parts[0] if len(parts) == 1 else jnp.concatenate(parts, axis=1)


def _for_each_query_tile(body):
    i = pl.program_id(2)
    for c in range(ATT_NT):
        pl.when(i == c)(functools.partial(body, c))


def _causal_tri():
    r = lax.broadcasted_iota(jnp.int32, (ATT_T, ATT_T), 0)
    c = lax.broadcasted_iota(jnp.int32, (ATT_T, ATT_T), 1)
    return c <= r


def _softmax_pv(s, v, o_ref, weights=None):
    m = jnp.max(s, axis=1, keepdims=True)
    e = jnp.exp(s - m)
    if weights is not None:
        e = _cat([e[:, j * ATT_T:(j + 1) * ATT_T] * w if w is not None
                  else e[:, j * ATT_T:(j + 1) * ATT_T] for j, w in enumerate(weights)])
    tot = jnp.sum(e, axis=1, keepdims=True)
    acc = jnp.dot(e.astype(jnp.bfloat16), v, preferred_element_type=jnp.float32)
    o_ref[...] = (acc / tot).astype(o_ref.dtype)


def _ffn_kernel(x_ref, ln_ref, wg_ref, wu_ref, wd_ref, *rest, final_norm):
    if final_norm:
        lnf_ref, o_ref, h_ref = rest
    else:
        o_ref, h_ref = rest
    j = pl.program_id(1)

    @pl.when(j == 0)
    def _():
        x = x_ref[...]
        h_ref[...] = _rms(x, ln_ref[...]).astype(jnp.bfloat16)
        o_ref[...] = x

    tf = wg_ref.shape[1]
    wgu = jnp.concatenate([wg_ref[...].astype(jnp.bfloat16),
                           wu_ref[...].astype(jnp.bfloat16)], axis=1)
    gu = jnp.dot(h_ref[...], wgu, preferred_element_type=jnp.float32)
    g = gu[:, :tf]
    u = gu[:, tf:]
    a = (g * (1.0 / (1.0 + jnp.exp(-g))) * u * 0.5).astype(jnp.bfloat16)
    o_ref[...] += jnp.dot(a, wd_ref[...].astype(jnp.bfloat16),
                          preferred_element_type=jnp.float32)

    if final_norm:
        @pl.when(j == pl.num_programs(1) - 1)
        def _():
            o_ref[...] = _rms(o_ref[...], lnf_ref[...])


def _ffn(x, ln, wg, wu, wd, layer, ln_final=None):
    final_norm = ln_final is not None
    tm, tf = FFN_TM, FFN_TF
    in_specs = [
        pl.BlockSpec((tm, D_MODEL), lambda i, j: (i, 0), pipeline_mode=pl.Buffered(1)),
        pl.BlockSpec((1, D_MODEL), lambda i, j: (0, 0)),
        pl.BlockSpec((None, D_MODEL, tf), lambda i, j: (layer, 0, j)),
        pl.BlockSpec((None, D_MODEL, tf), lambda i, j: (layer, 0, j)),
        pl.BlockSpec((None, tf, D_MODEL), lambda i, j: (layer, j, 0)),
    ]
    args = [x, ln, wg, wu, wd]
    if final_norm:
        in_specs.append(pl.BlockSpec((1, D_MODEL), lambda i, j: (0, 0)))
        args.append(ln_final)
    return pl.pallas_call(
        functools.partial(_ffn_kernel, final_norm=final_norm),
        grid=(TOKENS // tm, D_FF // tf),
        in_specs=in_specs,
        out_specs=pl.BlockSpec((tm, D_MODEL), lambda i, j: (i, 0)),
        out_shape=jax.ShapeDtypeStruct((TOKENS, D_MODEL), jnp.float32),
        scratch_shapes=[pltpu.VMEM((tm, D_MODEL), jnp.bfloat16)],
        compiler_params=_cparams(("parallel", "arbitrary"), 60),
        name="ffn_final" if final_norm else "ffn",
    )(*args)


def _proj_kernel(x_ref, ln_ref, w_ref, cos_ref, sin_ref, o_ref, h_ref, *, n_rope_blocks):
    j = pl.program_id(1)

    @pl.when(j == 0)
    def _():
        h_ref[...] = _rms(x_ref[...], ln_ref[...]).astype(jnp.bfloat16)

    y = jnp.dot(h_ref[...], w_ref[...], preferred_element_type=jnp.float32)
    groups = y.shape[1] // LANES

    if n_rope_blocks > 0:
        @pl.when(j < n_rope_blocks)
        def _():
            c = cos_ref[0]
            s = sin_ref[0]
            for gidx in range(groups):
                yg = y[:, gidx * LANES:(gidx + 1) * LANES]
                o_ref[:, gidx * LANES:(gidx + 1) * LANES] = (
                    yg * c + pltpu.roll(yg, LANES // 2, 1) * s).astype(o_ref.dtype)

    @pl.when(j >= n_rope_blocks)
    def _():
        o_ref[...] = y.astype(o_ref.dtype)


def _proj(x, ln, w, cos_tab, sin_tab, *, tn, rope_sections, out_dtype, name):
    n = w.shape[1]
    tm = PROJ_TM
    nj = n // tn
    blocks_per_section = max(nj // 3, 1)
    n_rope_blocks = rope_sections * blocks_per_section
    pos_blocks = SEQ // tm

    def tab_map(i, j):
        return (jnp.minimum(j // blocks_per_section, 1), i % pos_blocks, 0)

    return pl.pallas_call(
        functools.partial(_proj_kernel, n_rope_blocks=n_rope_blocks),
        grid=(TOKENS // tm, nj),
        in_specs=[
            pl.BlockSpec((tm, D_MODEL), lambda i, j: (i, 0)),
            pl.BlockSpec((1, D_MODEL), lambda i, j: (0, 0)),
            pl.BlockSpec((D_MODEL, tn), lambda i, j: (0, j)),
            pl.BlockSpec((1, tm, LANES), tab_map),
            pl.BlockSpec((1, tm, LANES), tab_map),
        ],
        out_specs=pl.BlockSpec((tm, tn), lambda i, j: (i, j)),
        out_shape=jax.ShapeDtypeStruct((TOKENS, n), out_dtype),
        scratch_shapes=[pltpu.VMEM((tm, D_MODEL), jnp.bfloat16)],
        compiler_params=_cparams(("parallel", "arbitrary"), 48),
        name=name,
    )(x, ln, w, cos_tab, sin_tab)


def _mla_prep_kernel(p_ref, gq_ref, gkv_ref, wuq_ref, wuk_ref, wuv_ref, cos_ref, sin_ref,
                     q_ref, k_ref, v_ref):
    scale = (MLA_NOPE + MLA_ROPE) ** -0.5
    c = cos_ref[...]
    s = sin_ref[...]

    def rope(t):
        return t * c + pltpu.roll(t, LANES // 2, 1) * s

    cq = _rms(p_ref[:, :MLA_Q_RANK], gq_ref[...]).astype(jnp.bfloat16)
    q = jnp.dot(cq, wuq_ref[...], preferred_element_type=jnp.float32)
    ckv = _rms(p_ref[:, MLA_Q_RANK:MLA_Q_RANK + MLA_KV_RANK], gkv_ref[...]).astype(jnp.bfloat16)
    kn = jnp.dot(ckv, wuk_ref[...], preferred_element_type=jnp.float32)
    v_ref[...] = jnp.dot(ckv, wuv_ref[...],
                         preferred_element_type=jnp.float32).astype(jnp.bfloat16)
    kr = rope(p_ref[:, MLA_Q_RANK + MLA_KV_RANK:]).astype(jnp.bfloat16)
    for h in range(MLA_HEADS):
        b0 = h * MLA_QK_PAD
        q_ref[:, b0:b0 + LANES] = (q[:, b0:b0 + LANES] * scale).astype(jnp.bfloat16)
        q_ref[:, b0 + LANES:b0 + 2 * LANES] = (
            rope(q[:, b0 + LANES:b0 + 2 * LANES]) * scale).astype(jnp.bfloat16)
        k_ref[:, b0:b0 + LANES] = kn[:, h * LANES:(h + 1) * LANES].astype(jnp.bfloat16)
        k_ref[:, b0 + LANES:b0 + 2 * LANES] = kr


def _mla_prep(p, gq, gkv, wuq, wuk, wuv, cos_r, sin_r):
    tm = MLA_PREP_TM
    pos_blocks = SEQ // tm
    qk_w = MLA_HEADS * MLA_QK_PAD
    v_w = MLA_HEADS * MLA_V

    def full(shape):
        return pl.BlockSpec(shape, lambda i: (0, 0))

    return pl.pallas_call(
        _mla_prep_kernel,
        grid=(TOKENS // tm,),
        in_specs=[
            pl.BlockSpec((tm, MLA_IN_PAD), lambda i: (i, 0)),
            full((1, MLA_Q_RANK)), full((1, MLA_KV_RANK)),
            full((MLA_Q_RANK, qk_w)), full((MLA_KV_RANK, v_w)), full((MLA_KV_RANK, v_w)),
            pl.BlockSpec((tm, LANES), lambda i: (i % pos_blocks, 0)),
            pl.BlockSpec((tm, LANES), lambda i: (i % pos_blocks, 0)),
        ],
        out_specs=[
            pl.BlockSpec((tm, qk_w), lambda i: (i, 0)),
            pl.BlockSpec((tm, qk_w), lambda i: (i, 0)),
            pl.BlockSpec((tm, v_w), lambda i: (i, 0)),
        ],
        out_shape=[
            jax.ShapeDtypeStruct((TOKENS, qk_w), jnp.bfloat16),
            jax.ShapeDtypeStruct((TOKENS, qk_w), jnp.bfloat16),
            jax.ShapeDtypeStruct((TOKENS, v_w), jnp.bfloat16),
        ],
        compiler_params=_cparams(("parallel",), 32),
        name="mla_prep",
    )(p, gq, gkv, wuq, wuk, wuv, cos_r, sin_r)


def _mla_attn_kernel(q_ref, k_ref, v_ref, o_ref):
    def body(c):
        n = (c + 1) * ATT_T
        s = _dot_nt(q_ref[...], k_ref[0:n, :])
        parts = [s[:, j * ATT_T:(j + 1) * ATT_T] for j in range(c)]
        parts.append(jnp.where(_causal_tri(), s[:, c * ATT_T:n], NEG_INF))
        _softmax_pv(_cat(parts), v_ref[0:n, :], o_ref)

    _for_each_query_tile(body)


def _mla_attn(q, k, v):
    nt = ATT_NT
    return pl.pallas_call(
        _mla_attn_kernel,
        grid=(BATCH, MLA_HEADS, nt),
        in_specs=[
            pl.BlockSpec((ATT_T, MLA_QK_PAD), lambda b, h, i: (b * nt + i, h)),
            pl.BlockSpec((SEQ, MLA_QK_PAD), lambda b, h, i: (b, h)),
            pl.BlockSpec((SEQ, MLA_V), lambda b, h, i: (b, h)),
        ],
        out_specs=pl.BlockSpec((ATT_T, MLA_V), lambda b, h, i: (b * nt + i, h)),
        out_shape=jax.ShapeDtypeStruct((TOKENS, MLA_HEADS * MLA_V), jnp.bfloat16),
        compiler_params=_cparams(("parallel", "parallel", "arbitrary"), 32),
        name="mla_attn",
    )(q, k, v)


def _moba_attn_kernel(q_ref, k_ref, v_ref, o_ref, km_ref):
    @pl.when(pl.program_id(2) == 0)
    def _():
        rid = lax.broadcasted_iota(jnp.int32, (LANES, HEAD_DIM), 0)
        km = jnp.zeros((LANES, HEAD_DIM), jnp.float32)
        for j in range(MOBA_NBLK):
            kj = k_ref[j * MOBA_BLOCK:(j + 1) * MOBA_BLOCK, :].astype(jnp.float32)
            mean_j = jnp.sum(kj, axis=0, keepdims=True) * (1.0 / MOBA_BLOCK)
            km = jnp.where(rid == j, mean_j, km)
        km_ref[...] = km

    def body(c):
        n = (c + 1) * ATT_T
        q = q_ref[...]
        s = _dot_nt(q, k_ref[0:n, :])
        parts = [s[:, j * ATT_T:(j + 1) * ATT_T] for j in range(c)]
        if c > MOBA_TOPK:
            km = km_ref[...]
            km_hi = km.astype(jnp.bfloat16)
            km_lo = (km - km_hi.astype(jnp.float32)).astype(jnp.bfloat16)
            gate = _dot_nt(q, km_hi) + _dot_nt(q, km_lo)
            lane = lax.broadcasted_iota(jnp.int32, gate.shape, 1)
            ahead = jnp.zeros(gate.shape, jnp.float32)
            for jp in range(c):
                cj = _lane_col(gate, jp, lane)
                wins = (cj > gate) | ((cj == gate) & (lane > jp))
                ahead = ahead + jnp.where(wins, 1.0, 0.0)
            sel = jnp.where(ahead < MOBA_TOPK, 1.0, 0.0)
            parts = [jnp.where(_lane_col(sel, j, lane) > 0.5, parts[j], NEG_INF)
                     for j in range(c)]
        parts.append(jnp.where(_causal_tri(), s[:, c * ATT_T:n], NEG_INF))
        _softmax_pv(_cat(parts), v_ref[0:n, :], o_ref)

    _for_each_query_tile(body)


def _moba_attn(qkv):
    nt = ATT_NT
    return pl.pallas_call(
        _moba_attn_kernel,
        grid=(BATCH, MOBA_HEADS, nt),
        in_specs=[
            pl.BlockSpec((ATT_T, HEAD_DIM), lambda b, h, i: (b * nt + i, h)),
            pl.BlockSpec((SEQ, HEAD_DIM), lambda b, h, i: (b, MOBA_HEADS + h)),
            pl.BlockSpec((SEQ, HEAD_DIM), lambda b, h, i: (b, 2 * MOBA_HEADS + h)),
        ],
        out_specs=pl.BlockSpec((ATT_T, HEAD_DIM), lambda b, h, i: (b * nt + i, h)),
        out_shape=jax.ShapeDtypeStruct((TOKENS, MOBA_HEADS * HEAD_DIM), jnp.bfloat16),
        scratch_shapes=[pltpu.VMEM((LANES, HEAD_DIM), jnp.float32)],
        compiler_params=_cparams(("arbitrary", "arbitrary", "arbitrary"), 32),
        name="moba_attn",
    )(qkv, qkv, qkv)


def _dil_tables():
    r = np.arange(ATT_T)[:, None]
    c = np.arange(ATT_T)[None, :]
    cnts = []
    for d in range(ATT_NT):
        delta = r - c + ATT_T * d
        cnts.append(sum(((delta >= 0) & (delta <= w) & (delta % dil == 0)).astype(np.float32)
                        for w, dil in DIL_PATTERNS))
    cnt = np.stack(cnts)
    bias = np.where(cnt > 0, 0.0, NEG_INF).astype(np.float32)
    n_weighted = max(d + 1 for d in range(ATT_NT) if cnt[d].max() > 1)
    return bias, cnt[:n_weighted]


def _dil_attn_kernel(q_ref, k_ref, v_ref, bias_ref, cnt_ref, o_ref):
    n_weighted = cnt_ref.shape[0]

    def body(c):
        n = (c + 1) * ATT_T
        s = _dot_nt(q_ref[...], k_ref[0:n, :])
        s = _cat([s[:, j * ATT_T:(j + 1) * ATT_T] + bias_ref[c - j] for j in range(c + 1)])
        weights = [cnt_ref[c - j] if c - j < n_weighted else None for j in range(c + 1)]
        _softmax_pv(s, v_ref[0:n, :], o_ref, weights)

    _for_each_query_tile(body)


def _dil_attn(qkv):
    nt = ATT_NT
    bias, cnt = _dil_tables()
    return pl.pallas_call(
        _dil_attn_kernel,
        grid=(BATCH, DIL_HEADS, nt),
        in_specs=[
            pl.BlockSpec((ATT_T, HEAD_DIM), lambda b, h, i: (b * nt + i, h)),
            pl.BlockSpec((SEQ, HEAD_DIM), lambda b, h, i: (b, DIL_HEADS + h)),
            pl.BlockSpec((SEQ, HEAD_DIM), lambda b, h, i: (b, 2 * DIL_HEADS + h)),
            pl.BlockSpec(bias.shape, lambda b, h, i: (0, 0, 0)),
            pl.BlockSpec(cnt.shape, lambda b, h, i: (0, 0, 0)),
        ],
        out_specs=pl.BlockSpec((ATT_T, HEAD_DIM), lambda b, h, i: (b * nt + i, h)),
        out_shape=jax.ShapeDtypeStruct((TOKENS, DIL_HEADS * HEAD_DIM), jnp.bfloat16),
        compiler_params=_cparams(("parallel", "parallel", "arbitrary"), 32),
        name="dil_attn",
    )(qkv, qkv, qkv, jnp.asarray(bias), jnp.asarray(cnt))


def _out_proj_kernel(x_ref, mla_ref, moba_ref, dil_ref, w_ref, o_ref):
    mix = jnp.concatenate([mla_ref[...], moba_ref[...], dil_ref[...]], axis=1)
    o_ref[...] = x_ref[...] + jnp.dot(mix, w_ref[...], preferred_element_type=jnp.float32)


def _out_proj(x, o_mla, o_moba, o_dil, w_out):
    tm = OUT_TM

    def rows(width):
        return pl.BlockSpec((tm, width), lambda i: (i, 0))

    return pl.pallas_call(
        _out_proj_kernel,
        grid=(TOKENS // tm,),
        in_specs=[rows(D_MODEL), rows(MLA_HEADS * MLA_V), rows(MOBA_HEADS * HEAD_DIM),
                  rows(DIL_HEADS * HEAD_DIM),
                  pl.BlockSpec((MIX_WIDTH, D_MODEL), lambda i: (0, 0))],
        out_specs=rows(D_MODEL),
        out_shape=jax.ShapeDtypeStruct((TOKENS, D_MODEL), jnp.float32),
        compiler_params=_cparams(("parallel",), 48),
        name="out_proj",
    )(x, o_mla, o_moba, o_dil, w_out)


def _rope_tables():
    pos = jnp.arange(SEQ, dtype=jnp.float32)[:, None]
    inv_h = ROPE_THETA ** (-jnp.arange(0, HEAD_DIM, 2, dtype=jnp.float32) / HEAD_DIM)
    ang = pos * inv_h[None, :]
    cos_h = jnp.concatenate([jnp.cos(ang), jnp.cos(ang)], axis=1)
    sin_h = jnp.concatenate([-jnp.sin(ang), jnp.sin(ang)], axis=1)
    scale = HEAD_DIM ** -0.5
    cos_qk = jnp.stack([cos_h * scale, cos_h])
    sin_qk = jnp.stack([sin_h * scale, sin_h])
    inv_r = ROPE_THETA ** (-jnp.arange(0, MLA_ROPE, 2, dtype=jnp.float32) / MLA_ROPE)
    ang_r = pos * inv_r[None, :]
    z = jnp.zeros_like(ang_r)
    cos_r = jnp.concatenate([jnp.cos(ang_r), z, jnp.cos(ang_r), z], axis=1)
    sin_r = jnp.concatenate([-jnp.sin(ang_r), z, jnp.sin(ang_r), z], axis=1)
    return cos_qk, sin_qk, cos_r, sin_r


def _pad_rope_cols(w):
    half = MLA_ROPE // 2
    z = jnp.zeros(w.shape[:-1] + (half,), w.dtype)
    return jnp.concatenate([w[..., :half], z, w[..., half:], z], axis=-1)


def _prep_layer_weights(w_in, w_uq, w_uk, w_uv, w_out):
    bf = jnp.bfloat16
    w_mla = jnp.concatenate(
        [w_in[:, :MLA_Q_RANK + MLA_KV_RANK], _pad_rope_cols(w_in[:, MLA_Q_RANK + MLA_KV_RANK:MLA_IN])],
        axis=1).astype(bf)
    w_moba = w_in[:, MLA_IN:MLA_IN + MOBA_IN].astype(bf)
    w_dil = w_in[:, MLA_IN + MOBA_IN:].astype(bf)
    uq = w_uq.reshape(MLA_Q_RANK, MLA_HEADS, MLA_NOPE + MLA_ROPE)
    uq = jnp.concatenate([uq[..., :MLA_NOPE], _pad_rope_cols(uq[..., MLA_NOPE:])], axis=-1)
    uq = uq.reshape(MLA_Q_RANK, MLA_HEADS * MLA_QK_PAD).astype(bf)
    return w_mla, w_moba, w_dil, uq, w_uk.astype(bf), w_uv.astype(bf), w_out.astype(bf)


def kernel(x, ln_ffn1, w_ffn1_gate, w_ffn1_up, w_ffn1_down, ln_mix, w_in, g_mla_q, g_mla_kv,
           w_mla_uq, w_mla_uk, w_mla_uv, w_out, ln_ffn2, w_ffn2_gate, w_ffn2_up, w_ffn2_down,
           ln_final):
    bf = jnp.bfloat16
    cos_qk, sin_qk, cos_r, sin_r = _rope_tables()
    xt = x.reshape(TOKENS, D_MODEL)
    for l in range(DEPTH):
        w_mla, w_moba, w_dil, uq, uk, uv, wo = _prep_layer_weights(
            w_in[l], w_mla_uq[l], w_mla_uk[l], w_mla_uv[l], w_out[l])
        xt = _ffn(xt, ln_ffn1[l][None], w_ffn1_gate, w_ffn1_up, w_ffn1_down, l)
        ln = ln_mix[l][None]
        p_mla = _proj(xt, ln, w_mla, cos_qk, sin_qk, tn=MLA_IN_PAD, rope_sections=0,
                      out_dtype=jnp.float32, name="proj_mla")
        qkv_moba = _proj(xt, ln, w_moba, cos_qk, sin_qk, tn=PROJ_TN, rope_sections=2,
                         out_dtype=bf, name="proj_moba")
        qkv_dil = _proj(xt, ln, w_dil, cos_qk, sin_qk, tn=PROJ_TN, rope_sections=2,
                        out_dtype=bf, name="proj_dil")
        q_mla, k_mla, v_mla = _mla_prep(p_mla, g_mla_q[l][None], g_mla_kv[l][None],
                                        uq, uk, uv, cos_r, sin_r)
        o_mla = _mla_attn(q_mla, k_mla, v_mla)
        o_moba = _moba_attn(qkv_moba)
        o_dil = _dil_attn(qkv_dil)
        xt = _out_proj(xt, o_mla, o_moba, o_dil, wo)
        xt = _ffn(xt, ln_ffn2[l][None], w_ffn2_gate, w_ffn2_up, w_ffn2_down, l,
                  ln_final=ln_final[None] if l == DEPTH - 1 else None)
    return xt.reshape(BATCH, SEQ, D_MODEL)
```

```python
import functools

import numpy as np

import jax
import jax.numpy as jnp
from jax import lax
from jax.experimental import pallas as pl
from jax.experimental.pallas import tpu as pltpu

D_MODEL = 2048
BATCH = 4
SEQ = 2048
DEPTH = 2
TOKENS = BATCH * SEQ

HEAD_DIM = 128
MLA_HEADS = 4
MLA_Q_RANK = 512
MLA_KV_RANK = 256
MLA_NOPE = 128
MLA_ROPE = 64
MLA_V = 128
MLA_QK_PAD = 256
MOBA_HEADS = 4
MOBA_BLOCK = 256
MOBA_TOPK = 3
MOBA_NBLK = SEQ // MOBA_BLOCK
DIL_HEADS = 8
DIL_PATTERNS = ((128, 1), (512, 4), (2048, 16))
D_FF = 5632
ROPE_THETA = 10000.0
NORM_EPS = 1e-6
NEG_INF = -1e30

MLA_IN = MLA_Q_RANK + MLA_KV_RANK + MLA_ROPE
MLA_IN_PAD = MLA_Q_RANK + MLA_KV_RANK + 128
MOBA_IN = 3 * MOBA_HEADS * HEAD_DIM
DIL_IN = 3 * DIL_HEADS * HEAD_DIM
MIX_WIDTH = MLA_HEADS * MLA_V + MOBA_HEADS * HEAD_DIM + DIL_HEADS * HEAD_DIM

LANES = 128
MIB = 1024 * 1024

FFN_TM = 1024
FFN_TF = 512
PROJ_TM = 512
PROJ_TN = 512
MLA_PREP_TM = 512
ATT_T = 256
ATT_NT = SEQ // ATT_T
OUT_TM = 512

assert ATT_T == MOBA_BLOCK


def _cparams(semantics, vmem_mib):
    return pltpu.CompilerParams(dimension_semantics=semantics,
                                vmem_limit_bytes=vmem_mib * MIB)


def _rms(x, g):
    ms = jnp.mean(x * x, axis=-1, keepdims=True)
    return x * lax.rsqrt(ms + NORM_EPS) * g


def _lane_col(a, j, lane):
    return jnp.sum(jnp.where(lane == j, a, 0.0), axis=1, keepdims=True)


def _dot_nt(a, b):
    return lax.dot_general(a, b, (((1,), (1,)), ((), ())),
                           preferred_element_type=jnp.float32)


def _cat(parts):
    return parts[0] if len(parts) == 1 else jnp.concatenate(parts, axis=1)


def _tile(c):
    return slice(c * ATT_T, (c + 1) * ATT_T)


def _causal_tri():
    r = lax.broadcasted_iota(jnp.int32, (ATT_T, ATT_T), 0)
    c = lax.broadcasted_iota(jnp.int32, (ATT_T, ATT_T), 1)
    return c <= r


def _softmax_pv(s, v, weights=None):
    m = jnp.max(s, axis=1, keepdims=True)
    e = jnp.exp(s - m)
    if weights is not None:
        e = _cat([e[:, _tile(j)] * w if w is not None else e[:, _tile(j)]
                  for j, w in enumerate(weights)])
    tot = jnp.sum(e, axis=1, keepdims=True)
    acc = jnp.dot(e.astype(jnp.bfloat16), v, preferred_element_type=jnp.float32)
    return (acc / tot).astype(jnp.bfloat16)


def _head_spec(width, col_offset):
    return pl.BlockSpec((SEQ, width), lambda b, h: (b, col_offset + h))


def _ffn_kernel(x_ref, ln_ref, wg_ref, wu_ref, wd_ref, *rest, final_norm):
    if final_norm:
        lnf_ref, o_ref, h_ref = rest
    else:
        o_ref, h_ref = rest
    j = pl.program_id(1)

    @pl.when(j == 0)
    def _():
        x = x_ref[...]
        h_ref[...] = _rms(x, ln_ref[...]).astype(jnp.bfloat16)
        o_ref[...] = x

    tf = wg_ref.shape[1]
    wgu = jnp.concatenate([wg_ref[...].astype(jnp.bfloat16),
                           wu_ref[...].astype(jnp.bfloat16)], axis=1)
    gu = jnp.dot(h_ref[...], wgu, preferred_element_type=jnp.float32)
    g = gu[:, :tf]
    u = gu[:, tf:]
    a = (g * (1.0 / (1.0 + jnp.exp(-g))) * u * 0.5).astype(jnp.bfloat16)
    o_ref[...] += jnp.dot(a, wd_ref[...].astype(jnp.bfloat16),
                          preferred_element_type=jnp.float32)

    if final_norm:
        @pl.when(j == pl.num_programs(1) - 1)
        def _():
            o_ref[...] = _rms(o_ref[...], lnf_ref[...])


def _ffn(x, ln, wg, wu, wd, layer, ln_final=None):
    final_norm = ln_final is not None
    tm, tf = FFN_TM, FFN_TF
    in_specs = [
        pl.BlockSpec((tm, D_MODEL), lambda i, j: (i, 0), pipeline_mode=pl.Buffered(1)),
        pl.BlockSpec((1, D_MODEL), lambda i, j: (0, 0)),
        pl.BlockSpec((None, D_MODEL, tf), lambda i, j: (layer, 0, j)),
        pl.BlockSpec((None, D_MODEL, tf), lambda i, j: (layer, 0, j)),
        pl.BlockSpec((None, tf, D_MODEL), lambda i, j: (layer, j, 0)),
    ]
    args = [x, ln, wg, wu, wd]
    if final_norm:
        in_specs.append(pl.BlockSpec((1, D_MODEL), lambda i, j: (0, 0)))
        args.append(ln_final)
    return pl.pallas_call(
        functools.partial(_ffn_kernel, final_norm=final_norm),
        grid=(TOKENS // tm, D_FF // tf),
        in_specs=in_specs,
        out_specs=pl.BlockSpec((tm, D_MODEL), lambda i, j: (i, 0)),
        out_shape=jax.ShapeDtypeStruct((TOKENS, D_MODEL), jnp.float32),
        scratch_shapes=[pltpu.VMEM((tm, D_MODEL), jnp.bfloat16)],
        compiler_params=_cparams(("parallel", "arbitrary"), 60),
        name="ffn_final" if final_norm else "ffn",
    )(*args)


_MOBA_W = MOBA_HEADS * HEAD_DIM
_DIL_W = DIL_HEADS * HEAD_DIM
PROJ_SECTIONS = (
    ("mla", ((MLA_IN_PAD, None),)),
    ("moba", ((_MOBA_W, 0), (_MOBA_W, 1), (_MOBA_W, None))),
    ("dil", ((_DIL_W, 0), (_DIL_W, 1), (_DIL_W, None))),
)
PROJ_WIDTH = MLA_IN_PAD + MOBA_IN + DIL_IN


def _proj_kernel(x_ref, ln_ref, w_ref, cos_ref, sin_ref, mla_ref, moba_ref, dil_ref):
    h = _rms(x_ref[...], ln_ref[...]).astype(jnp.bfloat16)
    out_refs = {"mla": mla_ref, "moba": moba_ref, "dil": dil_ref}
    col = 0
    for name, parts in PROJ_SECTIONS:
        o_ref = out_refs[name]
        out_col = 0
        for width, table in parts:
            for c0 in range(0, width, PROJ_TN):
                cw = min(PROJ_TN, width - c0)
                y = jnp.dot(h, w_ref[:, col + c0:col + c0 + cw],
                            preferred_element_type=jnp.float32)
                if table is None:
                    o_ref[:, out_col + c0:out_col + c0 + cw] = y.astype(o_ref.dtype)
                else:
                    c = cos_ref[table]
                    s = sin_ref[table]
                    for g0 in range(0, cw, LANES):
                        yg = y[:, g0:g0 + LANES]
                        o_ref[:, out_col + c0 + g0:out_col + c0 + g0 + LANES] = (
                            yg * c + pltpu.roll(yg, LANES // 2, 1) * s).astype(o_ref.dtype)
            col += width
            out_col += width


def _proj(x, ln, w, cos_tab, sin_tab):
    tm = PROJ_TM
    pos_blocks = SEQ // tm

    def rows(width):
        return pl.BlockSpec((tm, width), lambda i: (i, 0))

    def tab_spec():
        return pl.BlockSpec((2, tm, LANES), lambda i: (0, i % pos_blocks, 0))

    return pl.pallas_call(
        _proj_kernel,
        grid=(TOKENS // tm,),
        in_specs=[
            rows(D_MODEL),
            pl.BlockSpec((1, D_MODEL), lambda i: (0, 0)),
            pl.BlockSpec((D_MODEL, PROJ_WIDTH), lambda i: (0, 0), pipeline_mode=pl.Buffered(1)),
            tab_spec(), tab_spec(),
        ],
        out_specs=[rows(MLA_IN_PAD), rows(MOBA_IN), rows(DIL_IN)],
        out_shape=[
            jax.ShapeDtypeStruct((TOKENS, MLA_IN_PAD), jnp.float32),
            jax.ShapeDtypeStruct((TOKENS, MOBA_IN), jnp.bfloat16),
            jax.ShapeDtypeStruct((TOKENS, DIL_IN), jnp.bfloat16),
        ],
        compiler_params=_cparams(("parallel",), 56),
        name="mix_proj",
    )(x, ln, w, cos_tab, sin_tab)


def _mla_prep_kernel(p_ref, gq_ref, gkv_ref, wuq_ref, wuk_ref, wuv_ref, cos_ref, sin_ref,
                     q_ref, k_ref, v_ref):
    scale = (MLA_NOPE + MLA_ROPE) ** -0.5
    c = cos_ref[...]
    s = sin_ref[...]

    def rope(t):
        return t * c + pltpu.roll(t, LANES // 2, 1) * s

    cq = _rms(p_ref[:, :MLA_Q_RANK], gq_ref[...]).astype(jnp.bfloat16)
    q = jnp.dot(cq, wuq_ref[...], preferred_element_type=jnp.float32)
    ckv = _rms(p_ref[:, MLA_Q_RANK:MLA_Q_RANK + MLA_KV_RANK], gkv_ref[...]).astype(jnp.bfloat16)
    kn = jnp.dot(ckv, wuk_ref[...], preferred_element_type=jnp.float32)
    v_ref[...] = jnp.dot(ckv, wuv_ref[...],
                         preferred_element_type=jnp.float32).astype(jnp.bfloat16)
    kr = rope(p_ref[:, MLA_Q_RANK + MLA_KV_RANK:]).astype(jnp.bfloat16)
    for h in range(MLA_HEADS):
        b0 = h * MLA_QK_PAD
        q_ref[:, b0:b0 + LANES] = (q[:, b0:b0 + LANES] * scale).astype(jnp.bfloat16)
        q_ref[:, b0 + LANES:b0 + 2 * LANES] = (
            rope(q[:, b0 + LANES:b0 + 2 * LANES]) * scale).astype(jnp.bfloat16)
        k_ref[:, b0:b0 + LANES] = kn[:, h * LANES:(h + 1) * LANES].astype(jnp.bfloat16)
        k_ref[:, b0 + LANES:b0 + 2 * LANES] = kr


def _mla_prep(p, gq, gkv, wuq, wuk, wuv, cos_r, sin_r):
    tm = MLA_PREP_TM
    pos_blocks = SEQ // tm
    qk_w = MLA_HEADS * MLA_QK_PAD
    v_w = MLA_HEADS * MLA_V

    def full(shape):
        return pl.BlockSpec(shape, lambda i: (0, 0))

    return pl.pallas_call(
        _mla_prep_kernel,
        grid=(TOKENS // tm,),
        in_specs=[
            pl.BlockSpec((tm, MLA_IN_PAD), lambda i: (i, 0)),
            full((1, MLA_Q_RANK)), full((1, MLA_KV_RANK)),
            full((MLA_Q_RANK, qk_w)), full((MLA_KV_RANK, v_w)), full((MLA_KV_RANK, v_w)),
            pl.BlockSpec((tm, LANES), lambda i: (i % pos_blocks, 0)),
            pl.BlockSpec((tm, LANES), lambda i: (i % pos_blocks, 0)),
        ],
        out_specs=[
            pl.BlockSpec((tm, qk_w), lambda i: (i, 0)),
            pl.BlockSpec((tm, qk_w), lambda i: (i, 0)),
            pl.BlockSpec((tm, v_w), lambda i: (i, 0)),
        ],
        out_shape=[
            jax.ShapeDtypeStruct((TOKENS, qk_w), jnp.bfloat16),
            jax.ShapeDtypeStruct((TOKENS, qk_w), jnp.bfloat16),
            jax.ShapeDtypeStruct((TOKENS, v_w), jnp.bfloat16),
        ],
        compiler_params=_cparams(("parallel",), 32),
        name="mla_prep",
    )(p, gq, gkv, wuq, wuk, wuv, cos_r, sin_r)


def _mla_attn_kernel(q_ref, k_ref, v_ref, o_ref):
    tri = _causal_tri()
    for c in range(ATT_NT):
        n = (c + 1) * ATT_T
        s = _dot_nt(q_ref[_tile(c), :], k_ref[0:n, :])
        parts = [s[:, _tile(j)] for j in range(c)]
        parts.append(jnp.where(tri, s[:, _tile(c)], NEG_INF))
        o_ref[_tile(c), :] = _softmax_pv(_cat(parts), v_ref[0:n, :])


def _mla_attn(q, k, v):
    return pl.pallas_call(
        _mla_attn_kernel,
        grid=(BATCH, MLA_HEADS),
        in_specs=[_head_spec(MLA_QK_PAD, 0), _head_spec(MLA_QK_PAD, 0), _head_spec(MLA_V, 0)],
        out_specs=_head_spec(MLA_V, 0),
        out_shape=jax.ShapeDtypeStruct((TOKENS, MLA_HEADS * MLA_V), jnp.bfloat16),
        compiler_params=_cparams(("parallel", "parallel"), 32),
        name="mla_attn",
    )(q, k, v)


def _moba_attn_kernel(q_ref, k_ref, v_ref, o_ref):
    rid = lax.broadcasted_iota(jnp.int32, (LANES, HEAD_DIM), 0)
    km = jnp.zeros((LANES, HEAD_DIM), jnp.float32)
    for j in range(MOBA_NBLK):
        kj = k_ref[_tile(j), :].astype(jnp.float32)
        mean_j = jnp.sum(kj, axis=0, keepdims=True) * (1.0 / MOBA_BLOCK)
        km = jnp.where(rid == j, mean_j, km)
    km_hi = km.astype(jnp.bfloat16)
    km_lo = (km - km_hi.astype(jnp.float32)).astype(jnp.bfloat16)
    tri = _causal_tri()

    for c in range(ATT_NT):
        n = (c + 1) * ATT_T
        q = q_ref[_tile(c), :]
        s = _dot_nt(q, k_ref[0:n, :])
        parts = [s[:, _tile(j)] for j in range(c)]
        if c > MOBA_TOPK:
            gate = _dot_nt(q, km_hi) + _dot_nt(q, km_lo)
            lane = lax.broadcasted_iota(jnp.int32, gate.shape, 1)
            ahead = jnp.zeros(gate.shape, jnp.float32)
            for jp in range(c):
                cj = _lane_col(gate, jp, lane)
                wins = (cj > gate) | ((cj == gate) & (lane > jp))
                ahead = ahead + jnp.where(wins, 1.0, 0.0)
            sel = jnp.where(ahead < MOBA_TOPK, 1.0, 0.0)
            parts = [jnp.where(_lane_col(sel, j, lane) > 0.5, parts[j], NEG_INF)
                     for j in range(c)]
        parts.append(jnp.where(tri, s[:, _tile(c)], NEG_INF))
        o_ref[_tile(c), :] = _softmax_pv(_cat(parts), v_ref[0:n, :])


def _moba_attn(qkv):
    return pl.pallas_call(
        _moba_attn_kernel,
        grid=(BATCH, MOBA_HEADS),
        in_specs=[_head_spec(HEAD_DIM, 0), _head_spec(HEAD_DIM, MOBA_HEADS),
                  _head_spec(HEAD_DIM, 2 * MOBA_HEADS)],
        out_specs=_head_spec(HEAD_DIM, 0),
        out_shape=jax.ShapeDtypeStruct((TOKENS, MOBA_HEADS * HEAD_DIM), jnp.bfloat16),
        compiler_params=_cparams(("parallel", "parallel"), 32),
        name="moba_attn",
    )(qkv, qkv, qkv)


def _dil_tables():
    r = np.arange(ATT_T)[:, None]
    c = np.arange(ATT_T)[None, :]
    cnts = []
    for d in range(ATT_NT):
        delta = r - c + ATT_T * d
        cnts.append(sum(((delta >= 0) & (delta <= w) & (delta % dil == 0)).astype(np.float32)
                        for w, dil in DIL_PATTERNS))
    cnt = np.stack(cnts)
    bias = np.where(cnt > 0, 0.0, NEG_INF).astype(np.float32)
    n_weighted = max(d + 1 for d in range(ATT_NT) if cnt[d].max() > 1)
    return bias, cnt[:n_weighted]


def _dil_attn_kernel(q_ref, k_ref, v_ref, bias_ref, cnt_ref, o_ref):
    n_weighted = cnt_ref.shape[0]
    for c in range(ATT_NT):
        n = (c + 1) * ATT_T
        s = _dot_nt(q_ref[_tile(c), :], k_ref[0:n, :])
        s = _cat([s[:, _tile(j)] + bias_ref[c - j] for j in range(c + 1)])
        weights = [cnt_ref[c - j] if c - j < n_weighted else None for j in range(c + 1)]
        o_ref[_tile(c), :] = _softmax_pv(s, v_ref[0:n, :], weights)


def _dil_attn(qkv):
    bias, cnt = _dil_tables()
    return pl.pallas_call(
        _dil_attn_kernel,
        grid=(BATCH, DIL_HEADS),
        in_specs=[_head_spec(HEAD_DIM, 0), _head_spec(HEAD_DIM, DIL_HEADS),
                  _head_spec(HEAD_DIM, 2 * DIL_HEADS),
                  pl.BlockSpec(bias.shape, lambda b, h: (0, 0, 0)),
                  pl.BlockSpec(cnt.shape, lambda b, h: (0, 0, 0))],
        out_specs=_head_spec(HEAD_DIM, 0),
        out_shape=jax.ShapeDtypeStruct((TOKENS, DIL_HEADS * HEAD_DIM), jnp.bfloat16),
        compiler_params=_cparams(("parallel", "parallel"), 32),
        name="dil_attn",
    )(qkv, qkv, qkv, jnp.asarray(bias), jnp.asarray(cnt))


def _out_proj_kernel(x_ref, mla_ref, moba_ref, dil_ref, w_ref, o_ref):
    mix = jnp.concatenate([mla_ref[...], moba_ref[...], dil_ref[...]], axis=1)
    o_ref[...] = x_ref[...] + jnp.dot(mix, w_ref[...], preferred_element_type=jnp.float32)


def _out_proj(x, o_mla, o_moba, o_dil, w_out):
    tm = OUT_TM

    def rows(width):
        return pl.BlockSpec((tm, width), lambda i: (i, 0))

    return pl.pallas_call(
        _out_proj_kernel,
        grid=(TOKENS // tm,),
        in_specs=[rows(D_MODEL), rows(MLA_HEADS * MLA_V), rows(MOBA_HEADS * HEAD_DIM),
                  rows(DIL_HEADS * HEAD_DIM),
                  pl.BlockSpec((MIX_WIDTH, D_MODEL), lambda i: (0, 0))],
        out_specs=rows(D_MODEL),
        out_shape=jax.ShapeDtypeStruct((TOKENS, D_MODEL), jnp.float32),
        compiler_params=_cparams(("parallel",), 48),
        name="out_proj",
    )(x, o_mla, o_moba, o_dil, w_out)


def _rope_tables():
    pos = jnp.arange(SEQ, dtype=jnp.float32)[:, None]
    inv_h = ROPE_THETA ** (-jnp.arange(0, HEAD_DIM, 2, dtype=jnp.float32) / HEAD_DIM)
    ang = pos * inv_h[None, :]
    cos_h = jnp.concatenate([jnp.cos(ang), jnp.cos(ang)], axis=1)
    sin_h = jnp.concatenate([-jnp.sin(ang), jnp.sin(ang)], axis=1)
    scale = HEAD_DIM ** -0.5
    cos_qk = jnp.stack([cos_h * scale, cos_h])
    sin_qk = jnp.stack([sin_h * scale, sin_h])
    inv_r = ROPE_THETA ** (-jnp.arange(0, MLA_ROPE, 2, dtype=jnp.float32) / MLA_ROPE)
    ang_r = pos * inv_r[None, :]
    z = jnp.zeros_like(ang_r)
    cos_r = jnp.concatenate([jnp.cos(ang_r), z, jnp.cos(ang_r), z], axis=1)
    sin_r = jnp.concatenate([-jnp.sin(ang_r), z, jnp.sin(ang_r), z], axis=1)
    return cos_qk, sin_qk, cos_r, sin_r


def _pad_rope_cols(w):
    half = MLA_ROPE // 2
    z = jnp.zeros(w.shape[:-1] + (half,), w.dtype)
    return jnp.concatenate([w[..., :half], z, w[..., half:], z], axis=-1)


def _prep_layer_weights(w_in, w_uq, w_uk, w_uv, w_out):
    bf = jnp.bfloat16
    n_lat = MLA_Q_RANK + MLA_KV_RANK
    w_proj = jnp.concatenate(
        [w_in[:, :n_lat], _pad_rope_cols(w_in[:, n_lat:MLA_IN]), w_in[:, MLA_IN:]],
        axis=1).astype(bf)
    uq = w_uq.reshape(MLA_Q_RANK, MLA_HEADS, MLA_NOPE + MLA_ROPE)
    uq = jnp.concatenate([uq[..., :MLA_NOPE], _pad_rope_cols(uq[..., MLA_NOPE:])], axis=-1)
    uq = uq.reshape(MLA_Q_RANK, MLA_HEADS * MLA_QK_PAD).astype(bf)
    return w_proj, uq, w_uk.astype(bf), w_uv.astype(bf), w_out.astype(bf)


def kernel(x, ln_ffn1, w_ffn1_gate, w_ffn1_up, w_ffn1_down, ln_mix, w_in, g_mla_q, g_mla_kv,
           w_mla_uq, w_mla_uk, w_mla_uv, w_out, ln_ffn2, w_ffn2_gate, w_ffn2_up, w_ffn2_down,
           ln_final):
    bf = jnp.bfloat16
    cos_qk, sin_qk, cos_r, sin_r = _rope_tables()
    xt = x.reshape(TOKENS, D_MODEL)
    for l in range(DEPTH):
        w_proj, uq, uk, uv, wo = _prep_layer_weights(
            w_in[l], w_mla_uq[l], w_mla_uk[l], w_mla_uv[l], w_out[l])
        xt = _ffn(xt, ln_ffn1[l][None], w_ffn1_gate, w_ffn1_up, w_ffn1_down, l)
        p_mla, qkv_moba, qkv_dil = _proj(xt, ln_mix[l][None], w_proj, cos_qk, sin_qk)
        q_mla, k_mla, v_mla = _mla_prep(p_mla, g_mla_q[l][None], g_mla_kv[l][None],
                                        uq, uk, uv, cos_r, sin_r)
        o_mla = _mla_attn(q_mla, k_mla, v_mla)
        o_moba = _moba_attn(qkv_moba)
        o_dil = _dil_attn(qkv_dil)
        xt = _out_proj(xt, o_mla, o_moba, o_dil, wo)
        xt = _ffn(xt, ln_ffn2[l][None], w_ffn2_gate, w_ffn2_up, w_ffn2_down, l,
                  ln_final=ln_final[None] if l == DEPTH - 1 else None)
    return xt.reshape(BATCH, SEQ, D_MODEL)
```

```python
import functools

import numpy as np

import jax
import jax.numpy as jnp
from jax import lax
from jax.experimental import pallas as pl
from jax.experimental.pallas import tpu as pltpu

D_MODEL = 2048
BATCH = 4
SEQ = 2048
DEPTH = 2
TOKENS = BATCH * SEQ

HEAD_DIM = 128
MLA_HEADS = 4
MLA_Q_RANK = 512
MLA_KV_RANK = 256
MLA_NOPE = 128
MLA_ROPE = 64
MLA_V = 128
MLA_QK_PAD = 256
MOBA_HEADS = 4
MOBA_BLOCK = 256
MOBA_TOPK = 3
MOBA_NBLK = SEQ // MOBA_BLOCK
DIL_HEADS = 8
DIL_PATTERNS = ((128, 1), (512, 4), (2048, 16))
D_FF = 5632
ROPE_THETA = 10000.0
NORM_EPS = 1e-6
NEG_INF = -1e30
LOG2E = 1.4426950408889634

MLA_IN = MLA_Q_RANK + MLA_KV_RANK + MLA_ROPE
MLA_IN_PAD = MLA_Q_RANK + MLA_KV_RANK + 128
MOBA_IN = 3 * MOBA_HEADS * HEAD_DIM
DIL_IN = 3 * DIL_HEADS * HEAD_DIM
MIX_WIDTH = MLA_HEADS * MLA_V + MOBA_HEADS * HEAD_DIM + DIL_HEADS * HEAD_DIM

LANES = 128
MIB = 1024 * 1024

FFN_TM = 1024
FFN_TF = 512
PROJ_TM = 512
PROJ_TN = 512
MLA_PREP_TM = 512
ATT_T = 256
ATT_NT = SEQ // ATT_T
OUT_TM = 512

assert ATT_T == MOBA_BLOCK


def _cparams(semantics, vmem_mib):
    return pltpu.CompilerParams(dimension_semantics=semantics,
                                vmem_limit_bytes=vmem_mib * MIB)


def _rms(x, g):
    ms = jnp.mean(x * x, axis=-1, keepdims=True)
    return x * lax.rsqrt(ms + NORM_EPS) * g


def _lane_col(a, j, lane):
    return jnp.sum(jnp.where(lane == j, a, 0.0), axis=1, keepdims=True)


def _dot_nt(a, b):
    return lax.dot_general(a, b, (((1,), (1,)), ((), ())),
                           preferred_element_type=jnp.float32)


def _cat(parts):
    return parts[0] if len(parts) == 1 else jnp.concatenate(parts, axis=1)


def _tile(c):
    return slice(c * ATT_T, (c + 1) * ATT_T)


def _causal_tri():
    r = lax.broadcasted_iota(jnp.int32, (ATT_T, ATT_T), 0)
    c = lax.broadcasted_iota(jnp.int32, (ATT_T, ATT_T), 1)
    return c <= r


def _softmax_pv(tiles, v_ref, weights=None):
    ms, dens, accs = [], [], []
    for j, t in enumerate(tiles):
        m = jnp.max(t, axis=1, keepdims=True)
        e = jnp.exp2(t - m)
        if weights is not None and weights[j] is not None:
            e = e * weights[j]
        ms.append(m)
        dens.append(jnp.sum(e, axis=1, keepdims=True))
        accs.append(jnp.dot(e.astype(jnp.bfloat16), v_ref[_tile(j), :],
                            preferred_element_type=jnp.float32))
    if len(tiles) == 1:
        return (accs[0] / dens[0]).astype(jnp.bfloat16)
    m_all = ms[0]
    for m in ms[1:]:
        m_all = jnp.maximum(m_all, m)
    num = None
    tot = None
    for m, den, acc in zip(ms, dens, accs):
        a = jnp.exp2(m - m_all)
        num = a * acc if num is None else num + a * acc
        tot = a * den if tot is None else tot + a * den
    return (num / tot).astype(jnp.bfloat16)


def _head_spec(width, col_offset):
    return pl.BlockSpec((SEQ, width), lambda b, h: (b, col_offset + h))


def _ffn_kernel(x_ref, ln_ref, wg_ref, wu_ref, wd_ref, *rest, final_norm):
    if final_norm:
        lnf_ref, o_ref, h_ref = rest
    else:
        o_ref, h_ref = rest
    j = pl.program_id(1)

    @pl.when(j == 0)
    def _():
        x = x_ref[...]
        h_ref[...] = _rms(x, ln_ref[...]).astype(jnp.bfloat16)
        o_ref[...] = x

    tf = wg_ref.shape[1]
    wgu = jnp.concatenate([wg_ref[...].astype(jnp.bfloat16),
                           wu_ref[...].astype(jnp.bfloat16)], axis=1)
    gu = jnp.dot(h_ref[...], wgu, preferred_element_type=jnp.float32)
    g = gu[:, :tf]
    u = gu[:, tf:]
    a = (g * (1.0 / (1.0 + jnp.exp(-g))) * u * 0.5).astype(jnp.bfloat16)
    o_ref[...] += jnp.dot(a, wd_ref[...].astype(jnp.bfloat16),
                          preferred_element_type=jnp.float32)

    if final_norm:
        @pl.when(j == pl.num_programs(1) - 1)
        def _():
            o_ref[...] = _rms(o_ref[...], lnf_ref[...])


def _ffn(x, ln, wg, wu, wd, layer, ln_final=None):
    final_norm = ln_final is not None
    tm, tf = FFN_TM, FFN_TF
    in_specs = [
        pl.BlockSpec((tm, D_MODEL), lambda i, j: (i, 0), pipeline_mode=pl.Buffered(1)),
        pl.BlockSpec((1, D_MODEL), lambda i, j: (0, 0)),
        pl.BlockSpec((None, D_MODEL, tf), lambda i, j: (layer, 0, j)),
        pl.BlockSpec((None, D_MODEL, tf), lambda i, j: (layer, 0, j)),
        pl.BlockSpec((None, tf, D_MODEL), lambda i, j: (layer, j, 0)),
    ]
    args = [x, ln, wg, wu, wd]
    if final_norm:
        in_specs.append(pl.BlockSpec((1, D_MODEL), lambda i, j: (0, 0)))
        args.append(ln_final)
    return pl.pallas_call(
        functools.partial(_ffn_kernel, final_norm=final_norm),
        grid=(TOKENS // tm, D_FF // tf),
        in_specs=in_specs,
        out_specs=pl.BlockSpec((tm, D_MODEL), lambda i, j: (i, 0)),
        out_shape=jax.ShapeDtypeStruct((TOKENS, D_MODEL), jnp.float32),
        scratch_shapes=[pltpu.VMEM((tm, D_MODEL), jnp.bfloat16)],
        compiler_params=_cparams(("parallel", "arbitrary"), 60),
        name="ffn_final" if final_norm else "ffn",
    )(*args)


PROJ_WIDTH = -(-(MLA_IN + MOBA_IN + DIL_IN) // LANES) * LANES
_N_LAT_GROUPS = (MLA_Q_RANK + MLA_KV_RANK) // LANES
_MOBA_GROUPS = MOBA_IN // LANES
_DIL_GROUPS = DIL_IN // LANES


def _proj_kernel(x_ref, ln_ref, w_ref, cos_ref, sin_ref, mla_ref, moba_ref, dil_ref):
    h = _rms(x_ref[...], ln_ref[...]).astype(jnp.bfloat16)
    lane = lax.broadcasted_iota(jnp.int32, (x_ref.shape[0], LANES), 1)
    upper = lane >= LANES // 2
    half = MLA_ROPE // 2
    prev = None
    for c0 in range(0, PROJ_WIDTH, PROJ_TN):
        cw = min(PROJ_TN, PROJ_WIDTH - c0)
        y = jnp.dot(h, w_ref[:, c0:c0 + cw], preferred_element_type=jnp.float32)
        for g0 in range(0, cw, LANES):
            g = (c0 + g0) // LANES
            yg = y[:, g0:g0 + LANES]
            if g < _N_LAT_GROUPS:
                mla_ref[:, g * LANES:(g + 1) * LANES] = yg
                continue
            if g == _N_LAT_GROUPS:
                kr = (jnp.where(lane < half, yg, 0.0)
                      + jnp.where(upper & (lane < LANES // 2 + half),
                                  pltpu.roll(yg, half, 1), 0.0))
                mla_ref[:, g * LANES:(g + 1) * LANES] = kr
                prev = yg
                continue
            u = jnp.where(upper, prev, yg)
            prev = yg
            og = g - _N_LAT_GROUPS - 1
            if og < _MOBA_GROUPS:
                o_ref, section = moba_ref, og // (_MOBA_GROUPS // 3)
            else:
                og -= _MOBA_GROUPS
                o_ref, section = dil_ref, og // (_DIL_GROUPS // 3)
            head = pltpu.roll(u, LANES // 2, 1)
            if section < 2:
                head = head * cos_ref[section] + u * sin_ref[section]
            o_ref[:, og * LANES:(og + 1) * LANES] = head.astype(o_ref.dtype)


def _proj(x, ln, w, layer, cos_tab, sin_tab):
    tm = PROJ_TM
    pos_blocks = SEQ // tm

    def rows(width):
        return pl.BlockSpec((tm, width), lambda i: (i, 0))

    def tab_spec():
        return pl.BlockSpec((2, tm, LANES), lambda i: (0, i % pos_blocks, 0))

    return pl.pallas_call(
        _proj_kernel,
        grid=(TOKENS // tm,),
        in_specs=[
            rows(D_MODEL),
            pl.BlockSpec((1, D_MODEL), lambda i: (0, 0)),
            pl.BlockSpec((None, D_MODEL, PROJ_WIDTH), lambda i: (layer, 0, 0),
                         pipeline_mode=pl.Buffered(1)),
            tab_spec(), tab_spec(),
        ],
        out_specs=[rows(MLA_IN_PAD), rows(MOBA_IN), rows(DIL_IN)],
        out_shape=[
            jax.ShapeDtypeStruct((TOKENS, MLA_IN_PAD), jnp.float32),
            jax.ShapeDtypeStruct((TOKENS, MOBA_IN), jnp.bfloat16),
            jax.ShapeDtypeStruct((TOKENS, DIL_IN), jnp.bfloat16),
        ],
        compiler_params=_cparams(("parallel",), 56),
        name="mix_proj",
    )(x, ln, w, cos_tab, sin_tab)


def _mla_prep_kernel(p_ref, gq_ref, gkv_ref, wuq_ref, wuk_ref, wuv_ref, cos_ref, sin_ref,
                     q_ref, k_ref, v_ref):
    scale = (MLA_NOPE + MLA_ROPE) ** -0.5 * LOG2E
    c = cos_ref[...]
    s = sin_ref[...]

    def rope(t):
        return t * c + pltpu.roll(t, LANES // 2, 1) * s

    cq = _rms(p_ref[:, :MLA_Q_RANK], gq_ref[...]).astype(jnp.bfloat16)
    q = jnp.dot(cq, wuq_ref[...], preferred_element_type=jnp.float32)
    ckv = _rms(p_ref[:, MLA_Q_RANK:MLA_Q_RANK + MLA_KV_RANK], gkv_ref[...]).astype(jnp.bfloat16)
    kn = jnp.dot(ckv, wuk_ref[...], preferred_element_type=jnp.float32)
    v_ref[...] = jnp.dot(ckv, wuv_ref[...],
                         preferred_element_type=jnp.float32).astype(jnp.bfloat16)
    kr = rope(p_ref[:, MLA_Q_RANK + MLA_KV_RANK:]).astype(jnp.bfloat16)
    for h in range(MLA_HEADS):
        b0 = h * MLA_QK_PAD
        q_ref[:, b0:b0 + LANES] = (q[:, b0:b0 + LANES] * scale).astype(jnp.bfloat16)
        q_ref[:, b0 + LANES:b0 + 2 * LANES] = (
            rope(q[:, b0 + LANES:b0 + 2 * LANES]) * scale).astype(jnp.bfloat16)
        k_ref[:, b0:b0 + LANES] = kn[:, h * LANES:(h + 1) * LANES].astype(jnp.bfloat16)
        k_ref[:, b0 + LANES:b0 + 2 * LANES] = kr


def _mla_prep(p, gq, gkv, wuq, wuk, wuv, cos_r, sin_r):
    tm = MLA_PREP_TM
    pos_blocks = SEQ // tm
    qk_w = MLA_HEADS * MLA_QK_PAD
    v_w = MLA_HEADS * MLA_V

    def full(shape):
        return pl.BlockSpec(shape, lambda i: (0, 0))

    return pl.pallas_call(
        _mla_prep_kernel,
        grid=(TOKENS // tm,),
        in_specs=[
            pl.BlockSpec((tm, MLA_IN_PAD), lambda i: (i, 0)),
            full((1, MLA_Q_RANK)), full((1, MLA_KV_RANK)),
            full((MLA_Q_RANK, qk_w)), full((MLA_KV_RANK, v_w)), full((MLA_KV_RANK, v_w)),
            pl.BlockSpec((tm, LANES), lambda i: (i % pos_blocks, 0)),
            pl.BlockSpec((tm, LANES), lambda i: (i % pos_blocks, 0)),
        ],
        out_specs=[
            pl.BlockSpec((tm, qk_w), lambda i: (i, 0)),
            pl.BlockSpec((tm, qk_w), lambda i: (i, 0)),
            pl.BlockSpec((tm, v_w), lambda i: (i, 0)),
        ],
        out_shape=[
            jax.ShapeDtypeStruct((TOKENS, qk_w), jnp.bfloat16),
            jax.ShapeDtypeStruct((TOKENS, qk_w), jnp.bfloat16),
            jax.ShapeDtypeStruct((TOKENS, v_w), jnp.bfloat16),
        ],
        compiler_params=_cparams(("parallel",), 32),
        name="mla_prep",
    )(p, gq, gkv, wuq, wuk, wuv, cos_r, sin_r)


def _mla_attn_kernel(q_ref, k_ref, v_ref, o_ref):
    tri = _causal_tri()
    for c in reversed(range(ATT_NT)):
        n = (c + 1) * ATT_T
        s = _dot_nt(q_ref[_tile(c), :], k_ref[0:n, :])
        parts = [s[:, _tile(j)] for j in range(c)]
        parts.append(jnp.where(tri, s[:, _tile(c)], NEG_INF))
        o_ref[_tile(c), :] = _softmax_pv(parts, v_ref)


def _mla_attn(q, k, v):
    return pl.pallas_call(
        _mla_attn_kernel,
        grid=(BATCH, MLA_HEADS),
        in_specs=[_head_spec(MLA_QK_PAD, 0), _head_spec(MLA_QK_PAD, 0), _head_spec(MLA_V, 0)],
        out_specs=_head_spec(MLA_V, 0),
        out_shape=jax.ShapeDtypeStruct((TOKENS, MLA_HEADS * MLA_V), jnp.bfloat16),
        compiler_params=_cparams(("parallel", "parallel"), 32),
        name="mla_attn",
    )(q, k, v)


def _moba_attn_kernel(q_ref, k_ref, v_ref, o_ref):
    rid = lax.broadcasted_iota(jnp.int32, (LANES, HEAD_DIM), 0)
    km = jnp.zeros((LANES, HEAD_DIM), jnp.float32)
    for j in range(MOBA_NBLK):
        kj = k_ref[_tile(j), :].astype(jnp.float32)
        mean_j = jnp.sum(kj, axis=0, keepdims=True) * (1.0 / MOBA_BLOCK)
        km = jnp.where(rid == j, mean_j, km)
    km_hi = km.astype(jnp.bfloat16)
    km_lo = (km - km_hi.astype(jnp.float32)).astype(jnp.bfloat16)
    tri = _causal_tri()

    for c in reversed(range(ATT_NT)):
        n = (c + 1) * ATT_T
        q = q_ref[_tile(c), :]
        s = _dot_nt(q, k_ref[0:n, :])
        parts = [s[:, _tile(j)] for j in range(c)]
        if c > MOBA_TOPK:
            gate = _dot_nt(q, km_hi) + _dot_nt(q, km_lo)
            lane = lax.broadcasted_iota(jnp.int32, gate.shape, 1)
            ahead = jnp.zeros(gate.shape, jnp.float32)
            for jp in range(c):
                cj = _lane_col(gate, jp, lane)
                wins = (cj > gate) | ((cj == gate) & (lane > jp))
                ahead = ahead + jnp.where(wins, 1.0, 0.0)
            sel = jnp.where(ahead < MOBA_TOPK, 1.0, 0.0)
            parts = [jnp.where(_lane_col(sel, j, lane) > 0.5, parts[j], NEG_INF)
                     for j in range(c)]
        parts.append(jnp.where(tri, s[:, _tile(c)], NEG_INF))
        o_ref[_tile(c), :] = _softmax_pv(parts, v_ref)


def _moba_attn(qkv):
    return pl.pallas_call(
        _moba_attn_kernel,
        grid=(BATCH, MOBA_HEADS),
        in_specs=[_head_spec(HEAD_DIM, 0), _head_spec(HEAD_DIM, MOBA_HEADS),
                  _head_spec(HEAD_DIM, 2 * MOBA_HEADS)],
        out_specs=_head_spec(HEAD_DIM, 0),
        out_shape=jax.ShapeDtypeStruct((TOKENS, MOBA_HEADS * HEAD_DIM), jnp.bfloat16),
        compiler_params=_cparams(("parallel", "parallel"), 32),
        name="moba_attn",
    )(qkv, qkv, qkv)


def _dil_tables():
    r = np.arange(ATT_T)[:, None]
    c = np.arange(ATT_T)[None, :]
    cnts = []
    for d in range(ATT_NT):
        delta = r - c + ATT_T * d
        cnts.append(sum(((delta >= 0) & (delta <= w) & (delta % dil == 0)).astype(np.float32)
                        for w, dil in DIL_PATTERNS))
    cnt = np.stack(cnts)
    bias = np.where(cnt > 0, 0.0, NEG_INF).astype(np.float32)
    n_weighted = max(d + 1 for d in range(ATT_NT) if cnt[d].max() > 1)
    return bias, cnt[:n_weighted]


def _dil_attn_kernel(q_ref, k_ref, v_ref, bias_ref, cnt_ref, o_ref):
    n_weighted = cnt_ref.shape[0]
    for c in reversed(range(ATT_NT)):
        n = (c + 1) * ATT_T
        s = _dot_nt(q_ref[_tile(c), :], k_ref[0:n, :])
        tiles = [s[:, _tile(j)] + bias_ref[c - j] for j in range(c + 1)]
        weights = [cnt_ref[c - j] if c - j < n_weighted else None for j in range(c + 1)]
        o_ref[_tile(c), :] = _softmax_pv(tiles, v_ref, weights)


def _dil_attn(qkv):
    bias, cnt = _dil_tables()
    return pl.pallas_call(
        _dil_attn_kernel,
        grid=(BATCH, DIL_HEADS),
        in_specs=[_head_spec(HEAD_DIM, 0), _head_spec(HEAD_DIM, DIL_HEADS),
                  _head_spec(HEAD_DIM, 2 * DIL_HEADS),
                  pl.BlockSpec(bias.shape, lambda b, h: (0, 0, 0)),
                  pl.BlockSpec(cnt.shape, lambda b, h: (0, 0, 0))],
        out_specs=_head_spec(HEAD_DIM, 0),
        out_shape=jax.ShapeDtypeStruct((TOKENS, DIL_HEADS * HEAD_DIM), jnp.bfloat16),
        compiler_params=_cparams(("parallel", "parallel"), 32),
        name="dil_attn",
    )(qkv, qkv, qkv, jnp.asarray(bias), jnp.asarray(cnt))


def _out_proj_kernel(x_ref, mla_ref, moba_ref, dil_ref, w_ref, o_ref, wb_ref):
    @pl.when(pl.program_id(0) == 0)
    def _():
        wb_ref[...] = w_ref[...].astype(jnp.bfloat16)

    mix = jnp.concatenate([mla_ref[...], moba_ref[...], dil_ref[...]], axis=1)
    o_ref[...] = x_ref[...] + jnp.dot(mix, wb_ref[...], preferred_element_type=jnp.float32)


def _out_proj(x, o_mla, o_moba, o_dil, w_out, layer):
    tm = OUT_TM

    def rows(width):
        return pl.BlockSpec((tm, width), lambda i: (i, 0))

    return pl.pallas_call(
        _out_proj_kernel,
        grid=(TOKENS // tm,),
        in_specs=[rows(D_MODEL), rows(MLA_HEADS * MLA_V), rows(MOBA_HEADS * HEAD_DIM),
                  rows(DIL_HEADS * HEAD_DIM),
                  pl.BlockSpec((None, MIX_WIDTH, D_MODEL), lambda i: (layer, 0, 0),
                               pipeline_mode=pl.Buffered(1))],
        out_specs=rows(D_MODEL),
        out_shape=jax.ShapeDtypeStruct((TOKENS, D_MODEL), jnp.float32),
        scratch_shapes=[pltpu.VMEM((MIX_WIDTH, D_MODEL), jnp.bfloat16)],
        compiler_params=_cparams(("arbitrary",), 56),
        name="out_proj",
    )(x, o_mla, o_moba, o_dil, w_out)


def _rope_tables():
    pos = jnp.arange(SEQ, dtype=jnp.float32)[:, None]
    inv_h = ROPE_THETA ** (-jnp.arange(0, HEAD_DIM, 2, dtype=jnp.float32) / HEAD_DIM)
    ang = pos * inv_h[None, :]
    cos_h = jnp.concatenate([jnp.cos(ang), jnp.cos(ang)], axis=1)
    sin_h = jnp.concatenate([-jnp.sin(ang), jnp.sin(ang)], axis=1)
    scale = HEAD_DIM ** -0.5 * LOG2E
    cos_qk = jnp.stack([cos_h * scale, cos_h])
    sin_qk = jnp.stack([sin_h * scale, sin_h])
    inv_r = ROPE_THETA ** (-jnp.arange(0, MLA_ROPE, 2, dtype=jnp.float32) / MLA_ROPE)
    ang_r = pos * inv_r[None, :]
    z = jnp.zeros_like(ang_r)
    cos_r = jnp.concatenate([jnp.cos(ang_r), z, jnp.cos(ang_r), z], axis=1)
    sin_r = jnp.concatenate([-jnp.sin(ang_r), z, jnp.sin(ang_r), z], axis=1)
    return cos_qk, sin_qk, cos_r, sin_r


def _pad_rope_cols(w):
    half = MLA_ROPE // 2
    z = jnp.zeros(w.shape[:-1] + (half,), w.dtype)
    return jnp.concatenate([w[..., :half], z, w[..., half:], z], axis=-1)


def _prep_mla_weights(w_uq, w_uk, w_uv):
    bf = jnp.bfloat16
    uq = w_uq.reshape(MLA_Q_RANK, MLA_HEADS, MLA_NOPE + MLA_ROPE)
    uq = jnp.concatenate([uq[..., :MLA_NOPE], _pad_rope_cols(uq[..., MLA_NOPE:])], axis=-1)
    uq = uq.reshape(MLA_Q_RANK, MLA_HEADS * MLA_QK_PAD).astype(bf)
    return uq, w_uk.astype(bf), w_uv.astype(bf)


def kernel(x, ln_ffn1, w_ffn1_gate, w_ffn1_up, w_ffn1_down, ln_mix, w_in, g_mla_q, g_mla_kv,
           w_mla_uq, w_mla_uk, w_mla_uv, w_out, ln_ffn2, w_ffn2_gate, w_ffn2_up, w_ffn2_down,
           ln_final):
    cos_qk, sin_qk, cos_r, sin_r = _rope_tables()
    w_proj = jnp.pad(w_in.astype(jnp.bfloat16),
                     ((0, 0), (0, 0), (0, PROJ_WIDTH - w_in.shape[-1])))
    xt = x.reshape(TOKENS, D_MODEL)
    for l in range(DEPTH):
        uq, uk, uv = _prep_mla_weights(w_mla_uq[l], w_mla_uk[l], w_mla_uv[l])
        xt = _ffn(xt, ln_ffn1[l][None], w_ffn1_gate, w_ffn1_up, w_ffn1_down, l)
        p_mla, qkv_moba, qkv_dil = _proj(xt, ln_mix[l][None], w_proj, l, cos_qk, sin_qk)
        q_mla, k_mla, v_mla = _mla_prep(p_mla, g_mla_q[l][None], g_mla_kv[l][None],
                                        uq, uk, uv, cos_r, sin_r)
        o_mla = _mla_attn(q_mla, k_mla, v_mla)
        o_moba = _moba_attn(qkv_moba)
        o_dil = _dil_attn(qkv_dil)
        xt = _out_proj(xt, o_mla, o_moba, o_dil, w_out, l)
        xt = _ffn(xt, ln_ffn2[l][None], w_ffn2_gate, w_ffn2_up, w_ffn2_down, l,
                  ln_final=ln_final[None] if l == DEPTH - 1 else None)
    return xt.reshape(BATCH, SEQ, D_MODEL)
```

```python
import functools

import numpy as np

import jax
import jax.numpy as jnp
from jax import lax
from jax.experimental import pallas as pl
from jax.experimental.pallas import tpu as pltpu

D_MODEL = 2048
BATCH = 4
SEQ = 2048
DEPTH = 2
TOKENS = BATCH * SEQ

HEAD_DIM = 128
MLA_HEADS = 4
MLA_Q_RANK = 512
MLA_KV_RANK = 256
MLA_NOPE = 128
MLA_ROPE = 64
MLA_V = 128
MLA_QK_PAD = 256
MOBA_HEADS = 4
MOBA_BLOCK = 256
MOBA_TOPK = 3
MOBA_NBLK = SEQ // MOBA_BLOCK
DIL_HEADS = 8
DIL_PATTERNS = ((128, 1), (512, 4), (2048, 16))
D_FF = 5632
ROPE_THETA = 10000.0
NORM_EPS = 1e-6
NEG_INF = -1e30
LOG2E = 1.4426950408889634

MLA_IN = MLA_Q_RANK + MLA_KV_RANK + MLA_ROPE
MLA_IN_PAD = MLA_Q_RANK + MLA_KV_RANK + 128
MOBA_IN = 3 * MOBA_HEADS * HEAD_DIM
DIL_IN = 3 * DIL_HEADS * HEAD_DIM
MIX_WIDTH = MLA_HEADS * MLA_V + MOBA_HEADS * HEAD_DIM + DIL_HEADS * HEAD_DIM

LANES = 128
MIB = 1024 * 1024

FFN_TM = 1024
FFN_TF = 512
FFN_TF_HEAD = 256
PROJ_TM = 512
PROJ_TN = 512
MLA_PREP_TM = 512
ATT_T = 256
ATT_NT = SEQ // ATT_T
OUT_TM = 512

assert ATT_T == MOBA_BLOCK


def _cparams(semantics, vmem_mib):
    return pltpu.CompilerParams(dimension_semantics=semantics,
                                vmem_limit_bytes=vmem_mib * MIB)


def _rms(x, g):
    ms = jnp.mean(x * x, axis=-1, keepdims=True)
    return x * lax.rsqrt(ms + NORM_EPS) * g


def _lane_col(a, j, lane):
    return jnp.sum(jnp.where(lane == j, a, 0.0), axis=1, keepdims=True)


def _dot_nt(a, b):
    return lax.dot_general(a, b, (((1,), (1,)), ((), ())),
                           preferred_element_type=jnp.float32)


def _cat(parts):
    return parts[0] if len(parts) == 1 else jnp.concatenate(parts, axis=1)


def _tile(c):
    return slice(c * ATT_T, (c + 1) * ATT_T)


def _causal_tri():
    r = lax.broadcasted_iota(jnp.int32, (ATT_T, ATT_T), 0)
    c = lax.broadcasted_iota(jnp.int32, (ATT_T, ATT_T), 1)
    return c <= r


def _softmax_pv(tiles, v_ref, weights=None):
    ms, dens, accs = [], [], []
    for j, t in enumerate(tiles):
        m = jnp.max(t, axis=1, keepdims=True)
        e = jnp.exp2(t - m)
        if weights is not None and weights[j] is not None:
            e = e * weights[j]
        ms.append(m)
        dens.append(jnp.sum(e, axis=1, keepdims=True))
        accs.append(jnp.dot(e.astype(jnp.bfloat16), v_ref[_tile(j), :],
                            preferred_element_type=jnp.float32))
    if len(tiles) == 1:
        return (accs[0] / dens[0]).astype(jnp.bfloat16)
    m_all = ms[0]
    for m in ms[1:]:
        m_all = jnp.maximum(m_all, m)
    num = None
    tot = None
    for m, den, acc in zip(ms, dens, accs):
        a = jnp.exp2(m - m_all)
        num = a * acc if num is None else num + a * acc
        tot = a * den if tot is None else tot + a * den
    return (num / tot).astype(jnp.bfloat16)


def _head_spec(width, col_offset):
    return pl.BlockSpec((SEQ, width), lambda b, h: (b, col_offset + h))


def _ffn_kernel(x_ref, ln_ref, wg_ref, wu_ref, wd_ref, *rest, final_norm, first):
    rest = list(rest)
    lnf_ref = rest.pop(0) if final_norm else None
    if first:
        o_ref, wg16_ref, wu16_ref, wd16_ref, h_ref = rest
    else:
        _, o_ref, h_ref = rest
    j = pl.program_id(1)

    @pl.when(j == 0)
    def _():
        x = x_ref[...]
        h_ref[...] = _rms(x, ln_ref[...]).astype(jnp.bfloat16)
        o_ref[...] = x

    wg = wg_ref[...].astype(jnp.bfloat16)
    wu = wu_ref[...].astype(jnp.bfloat16)
    wd = wd_ref[...].astype(jnp.bfloat16)
    if first:
        wg16_ref[...] = wg
        wu16_ref[...] = wu
        wd16_ref[...] = wd
    tf = wg.shape[1]
    gu = jnp.dot(h_ref[...], jnp.concatenate([wg, wu], axis=1),
                 preferred_element_type=jnp.float32)
    g = gu[:, :tf]
    u = gu[:, tf:]
    a = (g * (1.0 / (1.0 + jnp.exp(-g))) * u * 0.5).astype(jnp.bfloat16)
    o_ref[...] += jnp.dot(a, wd, preferred_element_type=jnp.float32)

    if final_norm:
        @pl.when(j == pl.num_programs(1) - 1)
        def _():
            o_ref[...] = _rms(o_ref[...], lnf_ref[...])


def _ffn(x, ln, wg, wu, wd, layer, ln_final=None):
    final_norm = ln_final is not None
    tm = FFN_TM
    bf = jnp.bfloat16
    out_shape = jax.ShapeDtypeStruct((TOKENS, D_MODEL), jnp.float32)
    vec_spec = pl.BlockSpec((1, D_MODEL), lambda i, j: (0, 0))
    tail_specs = [vec_spec] if final_norm else []
    tail_args = [ln_final] if final_norm else []
    suffix = "_final" if final_norm else ""

    def w_specs(tf):
        return [pl.BlockSpec((D_MODEL, tf), lambda i, j: (0, j)),
                pl.BlockSpec((D_MODEL, tf), lambda i, j: (0, j)),
                pl.BlockSpec((tf, D_MODEL), lambda i, j: (j, 0))]

    def call(first, tf, grid_rows, row0, in_w_specs, w_args, extra_specs, extra_args,
             out_specs, out_shapes, aliases, x_mode):
        x_spec = pl.BlockSpec((tm, D_MODEL), lambda i, j: (i + row0, 0), **x_mode)
        return pl.pallas_call(
            functools.partial(_ffn_kernel, final_norm=final_norm, first=first),
            grid=(grid_rows, D_FF // tf),
            in_specs=[x_spec, vec_spec] + in_w_specs + tail_specs + extra_specs,
            out_specs=out_specs,
            out_shape=out_shapes,
            scratch_shapes=[pltpu.VMEM((tm, D_MODEL), bf)],
            input_output_aliases=aliases,
            compiler_params=_cparams(("parallel", "arbitrary"), 60),
            name=("ffn_head" if first else "ffn_rest") + suffix,
        )(x, ln, *w_args, *tail_args, *extra_args)

    tf = FFN_TF_HEAD
    y, wg16, wu16, wd16 = call(
        True, tf, 1, 0,
        [pl.BlockSpec((None, D_MODEL, tf), lambda i, j: (layer, 0, j)),
         pl.BlockSpec((None, D_MODEL, tf), lambda i, j: (layer, 0, j)),
         pl.BlockSpec((None, tf, D_MODEL), lambda i, j: (layer, j, 0))],
        (wg, wu, wd), [], [],
        [pl.BlockSpec((tm, D_MODEL), lambda i, j: (0, 0))] + w_specs(tf),
        [out_shape, jax.ShapeDtypeStruct((D_MODEL, D_FF), bf),
         jax.ShapeDtypeStruct((D_MODEL, D_FF), bf), jax.ShapeDtypeStruct((D_FF, D_MODEL), bf)],
        {}, dict(pipeline_mode=pl.Buffered(1)))
    alias_index = 5 + len(tail_args)
    return call(
        False, FFN_TF, TOKENS // tm - 1, 1, w_specs(FFN_TF), (wg16, wu16, wd16),
        [pl.BlockSpec(memory_space=pl.ANY)], [y],
        pl.BlockSpec((tm, D_MODEL), lambda i, j: (i + 1, 0)), out_shape,
        {alias_index: 0}, {})


PROJ_WIDTH = -(-(MLA_IN + MOBA_IN + DIL_IN) // LANES) * LANES
_N_LAT_GROUPS = (MLA_Q_RANK + MLA_KV_RANK) // LANES
_MOBA_GROUPS = MOBA_IN // LANES
_DIL_GROUPS = DIL_IN // LANES


def _proj_kernel(x_ref, ln_ref, w_ref, cos_ref, sin_ref, mla_ref, moba_ref, dil_ref):
    h = _rms(x_ref[...], ln_ref[...]).astype(jnp.bfloat16)
    lane = lax.broadcasted_iota(jnp.int32, (x_ref.shape[0], LANES), 1)
    upper = lane >= LANES // 2
    half = MLA_ROPE // 2
    prev = None
    for c0 in range(0, PROJ_WIDTH, PROJ_TN):
        cw = min(PROJ_TN, PROJ_WIDTH - c0)
        y = jnp.dot(h, w_ref[:, c0:c0 + cw], preferred_element_type=jnp.float32)
        for g0 in range(0, cw, LANES):
            g = (c0 + g0) // LANES
            yg = y[:, g0:g0 + LANES]
            if g < _N_LAT_GROUPS:
                mla_ref[:, g * LANES:(g + 1) * LANES] = yg
                continue
            if g == _N_LAT_GROUPS:
                kr = (jnp.where(lane < half, yg, 0.0)
                      + jnp.where(upper & (lane < LANES // 2 + half),
                                  pltpu.roll(yg, half, 1), 0.0))
                mla_ref[:, g * LANES:(g + 1) * LANES] = kr
                prev = yg
                continue
            u = jnp.where(upper, prev, yg)
            prev = yg
            og = g - _N_LAT_GROUPS - 1
            if og < _MOBA_GROUPS:
                o_ref, section = moba_ref, og // (_MOBA_GROUPS // 3)
            else:
                og -= _MOBA_GROUPS
                o_ref, section = dil_ref, og // (_DIL_GROUPS // 3)
            head = pltpu.roll(u, LANES // 2, 1)
            if section < 2:
                head = head * cos_ref[section] + u * sin_ref[section]
            o_ref[:, og * LANES:(og + 1) * LANES] = head.astype(o_ref.dtype)


def _proj(x, ln, w, layer, cos_tab, sin_tab):
    tm = PROJ_TM
    pos_blocks = SEQ // tm

    def rows(width):
        return pl.BlockSpec((tm, width), lambda i: (i, 0))

    def tab_spec():
        return pl.BlockSpec((2, tm, LANES), lambda i: (0, i % pos_blocks, 0))

    return pl.pallas_call(
        _proj_kernel,
        grid=(TOKENS // tm,),
        in_specs=[
            rows(D_MODEL),
            pl.BlockSpec((1, D_MODEL), lambda i: (0, 0)),
            pl.BlockSpec((None, D_MODEL, PROJ_WIDTH), lambda i: (layer, 0, 0),
                         pipeline_mode=pl.Buffered(1)),
            tab_spec(), tab_spec(),
        ],
        out_specs=[rows(MLA_IN_PAD), rows(MOBA_IN), rows(DIL_IN)],
        out_shape=[
            jax.ShapeDtypeStruct((TOKENS, MLA_IN_PAD), jnp.float32),
            jax.ShapeDtypeStruct((TOKENS, MOBA_IN), jnp.bfloat16),
            jax.ShapeDtypeStruct((TOKENS, DIL_IN), jnp.bfloat16),
        ],
        compiler_params=_cparams(("parallel",), 56),
        name="mix_proj",
    )(x, ln, w, cos_tab, sin_tab)


def _mla_prep_kernel(p_ref, gq_ref, gkv_ref, wuq_ref, wuk_ref, wuv_ref, cos_ref, sin_ref,
                     q_ref, k_ref, v_ref):
    scale = (MLA_NOPE + MLA_ROPE) ** -0.5 * LOG2E
    c = cos_ref[...]
    s = sin_ref[...]

    def rope(t):
        return t * c + pltpu.roll(t, LANES // 2, 1) * s

    cq = _rms(p_ref[:, :MLA_Q_RANK], gq_ref[...]).astype(jnp.bfloat16)
    q = jnp.dot(cq, wuq_ref[...], preferred_element_type=jnp.float32)
    ckv = _rms(p_ref[:, MLA_Q_RANK:MLA_Q_RANK + MLA_KV_RANK], gkv_ref[...]).astype(jnp.bfloat16)
    kn = jnp.dot(ckv, wuk_ref[...], preferred_element_type=jnp.float32)
    v_ref[...] = jnp.dot(ckv, wuv_ref[...],
                         preferred_element_type=jnp.float32).astype(jnp.bfloat16)
    kr = rope(p_ref[:, MLA_Q_RANK + MLA_KV_RANK:]).astype(jnp.bfloat16)
    for h in range(MLA_HEADS):
        b0 = h * MLA_QK_PAD
        q_ref[:, b0:b0 + LANES] = (q[:, b0:b0 + LANES] * scale).astype(jnp.bfloat16)
        q_ref[:, b0 + LANES:b0 + 2 * LANES] = (
            rope(q[:, b0 + LANES:b0 + 2 * LANES]) * scale).astype(jnp.bfloat16)
        k_ref[:, b0:b0 + LANES] = kn[:, h * LANES:(h + 1) * LANES].astype(jnp.bfloat16)
        k_ref[:, b0 + LANES:b0 + 2 * LANES] = kr


def _mla_prep(p, gq, gkv, wuq, wuk, wuv, cos_r, sin_r):
    tm = MLA_PREP_TM
    pos_blocks = SEQ // tm
    qk_w = MLA_HEADS * MLA_QK_PAD
    v_w = MLA_HEADS * MLA_V

    def full(shape):
        return pl.BlockSpec(shape, lambda i: (0, 0))

    return pl.pallas_call(
        _mla_prep_kernel,
        grid=(TOKENS // tm,),
        in_specs=[
            pl.BlockSpec((tm, MLA_IN_PAD), lambda i: (i, 0)),
            full((1, MLA_Q_RANK)), full((1, MLA_KV_RANK)),
            full((MLA_Q_RANK, qk_w)), full((MLA_KV_RANK, v_w)), full((MLA_KV_RANK, v_w)),
            pl.BlockSpec((tm, LANES), lambda i: (i % pos_blocks, 0)),
            pl.BlockSpec((tm, LANES), lambda i: (i % pos_blocks, 0)),
        ],
        out_specs=[
            pl.BlockSpec((tm, qk_w), lambda i: (i, 0)),
            pl.BlockSpec((tm, qk_w), lambda i: (i, 0)),
            pl.BlockSpec((tm, v_w), lambda i: (i, 0)),
        ],
        out_shape=[
            jax.ShapeDtypeStruct((TOKENS, qk_w), jnp.bfloat16),
            jax.ShapeDtypeStruct((TOKENS, qk_w), jnp.bfloat16),
            jax.ShapeDtypeStruct((TOKENS, v_w), jnp.bfloat16),
        ],
        compiler_params=_cparams(("parallel",), 32),
        name="mla_prep",
    )(p, gq, gkv, wuq, wuk, wuv, cos_r, sin_r)


def _mla_attn_kernel(q_ref, k_ref, v_ref, o_ref):
    tri = _causal_tri()
    for c in reversed(range(ATT_NT)):
        n = (c + 1) * ATT_T
        s = _dot_nt(q_ref[_tile(c), :], k_ref[0:n, :])
        parts = [s[:, _tile(j)] for j in range(c)]
        parts.append(jnp.where(tri, s[:, _tile(c)], NEG_INF))
        o_ref[_tile(c), :] = _softmax_pv(parts, v_ref)


def _mla_attn(q, k, v):
    return pl.pallas_call(
        _mla_attn_kernel,
        grid=(BATCH, MLA_HEADS),
        in_specs=[_head_spec(MLA_QK_PAD, 0), _head_spec(MLA_QK_PAD, 0), _head_spec(MLA_V, 0)],
        out_specs=_head_spec(MLA_V, 0),
        out_shape=jax.ShapeDtypeStruct((TOKENS, MLA_HEADS * MLA_V), jnp.bfloat16),
        compiler_params=_cparams(("parallel", "parallel"), 32),
        name="mla_attn",
    )(q, k, v)


def _moba_attn_kernel(q_ref, k_ref, v_ref, o_ref):
    rid = lax.broadcasted_iota(jnp.int32, (LANES, HEAD_DIM), 0)
    km = jnp.zeros((LANES, HEAD_DIM), jnp.float32)
    for j in range(MOBA_NBLK):
        kj = k_ref[_tile(j), :].astype(jnp.float32)
        mean_j = jnp.sum(kj, axis=0, keepdims=True) * (1.0 / MOBA_BLOCK)
        km = jnp.where(rid == j, mean_j, km)
    km_hi = km.astype(jnp.bfloat16)
    km_lo = (km - km_hi.astype(jnp.float32)).astype(jnp.bfloat16)
    tri = _causal_tri()

    for c in reversed(range(ATT_NT)):
        n = (c + 1) * ATT_T
        q = q_ref[_tile(c), :]
        s = _dot_nt(q, k_ref[0:n, :])
        parts = [s[:, _tile(j)] for j in range(c)]
        if c > MOBA_TOPK:
            gate = _dot_nt(q, km_hi) + _dot_nt(q, km_lo)
            lane = lax.broadcasted_iota(jnp.int32, gate.shape, 1)
            ahead = jnp.zeros(gate.shape, jnp.float32)
            for jp in range(c):
                cj = _lane_col(gate, jp, lane)
                wins = (cj > gate) | ((cj == gate) & (lane > jp))
                ahead = ahead + jnp.where(wins, 1.0, 0.0)
            sel = jnp.where(ahead < MOBA_TOPK, 1.0, 0.0)
            parts = [jnp.where(_lane_col(sel, j, lane) > 0.5, parts[j], NEG_INF)
                     for j in range(c)]
        parts.append(jnp.where(tri, s[:, _tile(c)], NEG_INF))
        o_ref[_tile(c), :] = _softmax_pv(parts, v_ref)


def _moba_attn(qkv):
    return pl.pallas_call(
        _moba_attn_kernel,
        grid=(BATCH, MOBA_HEADS),
        in_specs=[_head_spec(HEAD_DIM, 0), _head_spec(HEAD_DIM, MOBA_HEADS),
                  _head_spec(HEAD_DIM, 2 * MOBA_HEADS)],
        out_specs=_head_spec(HEAD_DIM, 0),
        out_shape=jax.ShapeDtypeStruct((TOKENS, MOBA_HEADS * HEAD_DIM), jnp.bfloat16),
        compiler_params=_cparams(("parallel", "parallel"), 32),
        name="moba_attn",
    )(qkv, qkv, qkv)


def _dil_tables():
    r = np.arange(ATT_T)[:, None]
    c = np.arange(ATT_T)[None, :]
    cnts = []
    for d in range(ATT_NT):
        delta = r - c + ATT_T * d
        cnts.append(sum(((delta >= 0) & (delta <= w) & (delta % dil == 0)).astype(np.float32)
                        for w, dil in DIL_PATTERNS))
    cnt = np.stack(cnts)
    bias = np.where(cnt > 0, 0.0, NEG_INF).astype(np.float32)
    n_weighted = max(d + 1 for d in range(ATT_NT) if cnt[d].max() > 1)
    return bias, cnt[:n_weighted]


def _dil_attn_kernel(q_ref, k_ref, v_ref, bias_ref, cnt_ref, o_ref):
    n_weighted = cnt_ref.shape[0]
    for c in reversed(range(ATT_NT)):
        n = (c + 1) * ATT_T
        s = _dot_nt(q_ref[_tile(c), :], k_ref[0:n, :])
        tiles = [s[:, _tile(j)] + bias_ref[c - j] for j in range(c + 1)]
        weights = [cnt_ref[c - j] if c - j < n_weighted else None for j in range(c + 1)]
        o_ref[_tile(c), :] = _softmax_pv(tiles, v_ref, weights)


def _dil_attn(qkv):
    bias, cnt = _dil_tables()
    return pl.pallas_call(
        _dil_attn_kernel,
        grid=(BATCH, DIL_HEADS),
        in_specs=[_head_spec(HEAD_DIM, 0), _head_spec(HEAD_DIM, DIL_HEADS),
                  _head_spec(HEAD_DIM, 2 * DIL_HEADS),
                  pl.BlockSpec(bias.shape, lambda b, h: (0, 0, 0)),
                  pl.BlockSpec(cnt.shape, lambda b, h: (0, 0, 0))],
        out_specs=_head_spec(HEAD_DIM, 0),
        out_shape=jax.ShapeDtypeStruct((TOKENS, DIL_HEADS * HEAD_DIM), jnp.bfloat16),
        compiler_params=_cparams(("parallel", "parallel"), 32),
        name="dil_attn",
    )(qkv, qkv, qkv, jnp.asarray(bias), jnp.asarray(cnt))


def _out_proj_kernel(x_ref, mla_ref, moba_ref, dil_ref, w_ref, o_ref, wb_ref):
    @pl.when(pl.program_id(0) == 0)
    def _():
        wb_ref[...] = w_ref[...].astype(jnp.bfloat16)

    mix = jnp.concatenate([mla_ref[...], moba_ref[...], dil_ref[...]], axis=1)
    o_ref[...] = x_ref[...] + jnp.dot(mix, wb_ref[...], preferred_element_type=jnp.float32)


def _out_proj(x, o_mla, o_moba, o_dil, w_out, layer):
    tm = OUT_TM

    def rows(width):
        return pl.BlockSpec((tm, width), lambda i: (i, 0))

    return pl.pallas_call(
        _out_proj_kernel,
        grid=(TOKENS // tm,),
        in_specs=[rows(D_MODEL), rows(MLA_HEADS * MLA_V), rows(MOBA_HEADS * HEAD_DIM),
                  rows(DIL_HEADS * HEAD_DIM),
                  pl.BlockSpec((None, MIX_WIDTH, D_MODEL), lambda i: (layer, 0, 0),
                               pipeline_mode=pl.Buffered(1))],
        out_specs=rows(D_MODEL),
        out_shape=jax.ShapeDtypeStruct((TOKENS, D_MODEL), jnp.float32),
        scratch_shapes=[pltpu.VMEM((MIX_WIDTH, D_MODEL), jnp.bfloat16)],
        compiler_params=_cparams(("arbitrary",), 56),
        name="out_proj",
    )(x, o_mla, o_moba, o_dil, w_out)


def _rope_tables():
    pos = jnp.arange(SEQ, dtype=jnp.float32)[:, None]
    inv_h = ROPE_THETA ** (-jnp.arange(0, HEAD_DIM, 2, dtype=jnp.float32) / HEAD_DIM)
    ang = pos * inv_h[None, :]
    cos_h = jnp.concatenate([jnp.cos(ang), jnp.cos(ang)], axis=1)
    sin_h = jnp.concatenate([-jnp.sin(ang), jnp.sin(ang)], axis=1)
    scale = HEAD_DIM ** -0.5 * LOG2E
    cos_qk = jnp.stack([cos_h * scale, cos_h])
    sin_qk = jnp.stack([sin_h * scale, sin_h])
    inv_r = ROPE_THETA ** (-jnp.arange(0, MLA_ROPE, 2, dtype=jnp.float32) / MLA_ROPE)
    ang_r = pos * inv_r[None, :]
    z = jnp.zeros_like(ang_r)
    cos_r = jnp.concatenate([jnp.cos(ang_r), z, jnp.cos(ang_r), z], axis=1)
    sin_r = jnp.concatenate([-jnp.sin(ang_r), z, jnp.sin(ang_r), z], axis=1)
    return cos_qk, sin_qk, cos_r, sin_r


def _pad_rope_cols(w):
    half = MLA_ROPE // 2
    z = jnp.zeros(w.shape[:-1] + (half,), w.dtype)
    return jnp.concatenate([w[..., :half], z, w[..., half:], z], axis=-1)


def _prep_mla_weights(w_uq, w_uk, w_uv):
    bf = jnp.bfloat16
    uq = w_uq.reshape(MLA_Q_RANK, MLA_HEADS, MLA_NOPE + MLA_ROPE)
    uq = jnp.concatenate([uq[..., :MLA_NOPE], _pad_rope_cols(uq[..., MLA_NOPE:])], axis=-1)
    uq = uq.reshape(MLA_Q_RANK, MLA_HEADS * MLA_QK_PAD).astype(bf)
    return uq, w_uk.astype(bf), w_uv.astype(bf)


def kernel(x, ln_ffn1, w_ffn1_gate, w_ffn1_up, w_ffn1_down, ln_mix, w_in, g_mla_q, g_mla_kv,
           w_mla_uq, w_mla_uk, w_mla_uv, w_out, ln_ffn2, w_ffn2_gate, w_ffn2_up, w_ffn2_down,
           ln_final):
    cos_qk, sin_qk, cos_r, sin_r = _rope_tables()
    w_proj = jnp.pad(w_in.astype(jnp.bfloat16),
                     ((0, 0), (0, 0), (0, PROJ_WIDTH - w_in.shape[-1])))
    xt = x.reshape(TOKENS, D_MODEL)
    for l in range(DEPTH):
        uq, uk, uv = _prep_mla_weights(w_mla_uq[l], w_mla_uk[l], w_mla_uv[l])
        xt = _ffn(xt, ln_ffn1[l][None], w_ffn1_gate, w_ffn1_up, w_ffn1_down, l)
        p_mla, qkv_moba, qkv_dil = _proj(xt, ln_mix[l][None], w_proj, l, cos_qk, sin_qk)
        q_mla, k_mla, v_mla = _mla_prep(p_mla, g_mla_q[l][None], g_mla_kv[l][None],
                                        uq, uk, uv, cos_r, sin_r)
        o_mla = _mla_attn(q_mla, k_mla, v_mla)
        o_moba = _moba_attn(qkv_moba)
        o_dil = _dil_attn(qkv_dil)
        xt = _out_proj(xt, o_mla, o_moba, o_dil, w_out, l)
        xt = _ffn(xt, ln_ffn2[l][None], w_ffn2_gate, w_ffn2_up, w_ffn2_down, l,
                  ln_final=ln_final[None] if l == DEPTH - 1 else None)
    return xt.reshape(BATCH, SEQ, D_MODEL)
```

```python
import functools

import numpy as np

import jax
import jax.numpy as jnp
from jax import lax
from jax.experimental import pallas as pl
from jax.experimental.pallas import tpu as pltpu

D_MODEL = 2048
BATCH = 4
SEQ = 2048
DEPTH = 2
TOKENS = BATCH * SEQ

HEAD_DIM = 128
MLA_HEADS = 4
MLA_Q_RANK = 512
MLA_KV_RANK = 256
MLA_NOPE = 128
MLA_ROPE = 64
MLA_V = 128
MLA_QK_PAD = 256
MOBA_HEADS = 4
MOBA_BLOCK = 256
MOBA_TOPK = 3
MOBA_NBLK = SEQ // MOBA_BLOCK
DIL_HEADS = 8
DIL_PATTERNS = ((128, 1), (512, 4), (2048, 16))
D_FF = 5632
ROPE_THETA = 10000.0
NORM_EPS = 1e-6
NEG_INF = -1e30
LOG2E = 1.4426950408889634

MLA_IN = MLA_Q_RANK + MLA_KV_RANK + MLA_ROPE
MLA_IN_PAD = MLA_Q_RANK + MLA_KV_RANK + 128
MOBA_IN = 3 * MOBA_HEADS * HEAD_DIM
DIL_IN = 3 * DIL_HEADS * HEAD_DIM
MIX_WIDTH = MLA_HEADS * MLA_V + MOBA_HEADS * HEAD_DIM + DIL_HEADS * HEAD_DIM

LANES = 128
MIB = 1024 * 1024

FFN_TM = 1024
FFN_TF = 512
FFN_TF_HEAD = 256
PROJ_TM = 512
PROJ_TN = 512
MLA_PREP_TM = 512
ATT_T = 256
ATT_NT = SEQ // ATT_T
OUT_TM = 512

assert ATT_T == MOBA_BLOCK


def _cparams(semantics, vmem_mib):
    return pltpu.CompilerParams(dimension_semantics=semantics,
                                vmem_limit_bytes=vmem_mib * MIB)


def _rms(x, g):
    ms = jnp.mean(x * x, axis=-1, keepdims=True)
    return x * lax.rsqrt(ms + NORM_EPS) * g


def _lane_col(a, j, lane):
    return jnp.sum(jnp.where(lane == j, a, 0.0), axis=1, keepdims=True)


def _dot_nt(a, b):
    return lax.dot_general(a, b, (((1,), (1,)), ((), ())),
                           preferred_element_type=jnp.float32)


def _cat(parts):
    return parts[0] if len(parts) == 1 else jnp.concatenate(parts, axis=1)


def _tile(c):
    return slice(c * ATT_T, (c + 1) * ATT_T)


def _causal_tri():
    r = lax.broadcasted_iota(jnp.int32, (ATT_T, ATT_T), 0)
    c = lax.broadcasted_iota(jnp.int32, (ATT_T, ATT_T), 1)
    return c <= r


def _softmax_pv(tiles, v_ref, weights=None):
    mx = tiles[0]
    for t in tiles[1:]:
        mx = jnp.maximum(mx, t)
    m = jnp.max(mx, axis=1, keepdims=True)
    res = None
    for j, t in enumerate(tiles):
        e = jnp.exp2(t - m)
        if weights is not None and weights[j] is not None:
            e = e * weights[j]
        pv = jnp.dot(e.astype(jnp.bfloat16), v_ref[_tile(j), :],
                     preferred_element_type=jnp.float32)
        res = pv if res is None else res + pv
    half = res.shape[1] // 2
    return (res[:, :half] / res[:, half:]).astype(jnp.bfloat16)


def _head_spec(width, col_offset):
    return pl.BlockSpec((SEQ, width), lambda b, h: (b, col_offset + h))


def _ffn_kernel(x_ref, ln_ref, wg_ref, wu_ref, wd_ref, *rest, final_norm, first):
    rest = list(rest)
    lnf_ref = rest.pop(0) if final_norm else None
    if first:
        o_ref, wg16_ref, wu16_ref, wd16_ref, h_ref = rest
    else:
        _, o_ref, h_ref = rest
    j = pl.program_id(1)

    @pl.when(j == 0)
    def _():
        x = x_ref[...]
        h_ref[...] = _rms(x, ln_ref[...]).astype(jnp.bfloat16)
        o_ref[...] = x

    wg = wg_ref[...].astype(jnp.bfloat16)
    wu = wu_ref[...].astype(jnp.bfloat16)
    wd = wd_ref[...].astype(jnp.bfloat16)
    if first:
        wg16_ref[...] = wg
        wu16_ref[...] = wu
        wd16_ref[...] = wd
    tf = wg.shape[1]
    gu = jnp.dot(h_ref[...], jnp.concatenate([wg, wu], axis=1),
                 preferred_element_type=jnp.float32)
    g = gu[:, :tf]
    u = gu[:, tf:]
    a = (g * (1.0 / (1.0 + jnp.exp(-g))) * u * 0.5).astype(jnp.bfloat16)
    o_ref[...] += jnp.dot(a, wd, preferred_element_type=jnp.float32)

    if final_norm:
        @pl.when(j == pl.num_programs(1) - 1)
        def _():
            o_ref[...] = _rms(o_ref[...], lnf_ref[...])


def _ffn(x, ln, wg, wu, wd, layer, ln_final=None):
    final_norm = ln_final is not None
    tm = FFN_TM
    bf = jnp.bfloat16
    out_shape = jax.ShapeDtypeStruct((TOKENS, D_MODEL), jnp.float32)
    vec_spec = pl.BlockSpec((1, D_MODEL), lambda i, j: (0, 0))
    tail_specs = [vec_spec] if final_norm else []
    tail_args = [ln_final] if final_norm else []
    suffix = "_final" if final_norm else ""

    def w_specs(tf):
        return [pl.BlockSpec((D_MODEL, tf), lambda i, j: (0, j)),
                pl.BlockSpec((D_MODEL, tf), lambda i, j: (0, j)),
                pl.BlockSpec((tf, D_MODEL), lambda i, j: (j, 0))]

    def call(first, tf, grid_rows, row0, in_w_specs, w_args, extra_specs, extra_args,
             out_specs, out_shapes, aliases, x_mode):
        x_spec = pl.BlockSpec((tm, D_MODEL), lambda i, j: (i + row0, 0), **x_mode)
        return pl.pallas_call(
            functools.partial(_ffn_kernel, final_norm=final_norm, first=first),
            grid=(grid_rows, D_FF // tf),
            in_specs=[x_spec, vec_spec] + in_w_specs + tail_specs + extra_specs,
            out_specs=out_specs,
            out_shape=out_shapes,
            scratch_shapes=[pltpu.VMEM((tm, D_MODEL), bf)],
            input_output_aliases=aliases,
            compiler_params=_cparams(("parallel", "arbitrary"), 60),
            name=("ffn_head" if first else "ffn_rest") + suffix,
        )(x, ln, *w_args, *tail_args, *extra_args)

    tf = FFN_TF_HEAD
    y, wg16, wu16, wd16 = call(
        True, tf, 1, 0,
        [pl.BlockSpec((None, D_MODEL, tf), lambda i, j: (layer, 0, j)),
         pl.BlockSpec((None, D_MODEL, tf), lambda i, j: (layer, 0, j)),
         pl.BlockSpec((None, tf, D_MODEL), lambda i, j: (layer, j, 0))],
        (wg, wu, wd), [], [],
        [pl.BlockSpec((tm, D_MODEL), lambda i, j: (0, 0))] + w_specs(tf),
        [out_shape, jax.ShapeDtypeStruct((D_MODEL, D_FF), bf),
         jax.ShapeDtypeStruct((D_MODEL, D_FF), bf), jax.ShapeDtypeStruct((D_FF, D_MODEL), bf)],
        {}, dict(pipeline_mode=pl.Buffered(1)))
    alias_index = 5 + len(tail_args)
    return call(
        False, FFN_TF, TOKENS // tm - 1, 1, w_specs(FFN_TF), (wg16, wu16, wd16),
        [pl.BlockSpec(memory_space=pl.ANY)], [y],
        pl.BlockSpec((tm, D_MODEL), lambda i, j: (i + 1, 0)), out_shape,
        {alias_index: 0}, {})


PROJ_WIDTH = MLA_IN + MOBA_IN + DIL_IN
_N_LAT_GROUPS = (MLA_Q_RANK + MLA_KV_RANK) // LANES
_MOBA_GROUPS = MOBA_IN // LANES
_DIL_GROUPS = DIL_IN // LANES


def _proj_kernel(x_ref, ln_ref, w_ref, cos_ref, sin_ref, mla_ref, moba_ref, dil_ref):
    h = _rms(x_ref[...], ln_ref[...]).astype(jnp.bfloat16)
    lane = lax.broadcasted_iota(jnp.int32, (x_ref.shape[0], LANES), 1)
    upper = lane >= LANES // 2
    half = MLA_ROPE // 2
    prev = None
    for c0 in range(0, PROJ_WIDTH, PROJ_TN):
        cw = min(PROJ_TN, PROJ_WIDTH - c0)
        y = jnp.dot(h, w_ref[:, c0:c0 + cw], preferred_element_type=jnp.float32)
        for g0 in range(0, cw, LANES):
            g = (c0 + g0) // LANES
            gw = min(LANES, cw - g0)
            yg = y[:, g0:g0 + gw]
            if gw < LANES:
                yg = jnp.concatenate([yg, jnp.zeros((yg.shape[0], LANES - gw), yg.dtype)], axis=1)
            if g < _N_LAT_GROUPS:
                mla_ref[:, g * LANES:(g + 1) * LANES] = yg
                continue
            if g == _N_LAT_GROUPS:
                kr = (jnp.where(lane < half, yg, 0.0)
                      + jnp.where(upper & (lane < LANES // 2 + half),
                                  pltpu.roll(yg, half, 1), 0.0))
                mla_ref[:, g * LANES:(g + 1) * LANES] = kr
                prev = yg
                continue
            u = jnp.where(upper, prev, yg)
            prev = yg
            og = g - _N_LAT_GROUPS - 1
            if og < _MOBA_GROUPS:
                o_ref, section = moba_ref, og // (_MOBA_GROUPS // 3)
            else:
                og -= _MOBA_GROUPS
                o_ref, section = dil_ref, og // (_DIL_GROUPS // 3)
            head = pltpu.roll(u, LANES // 2, 1)
            if section < 2:
                head = head * cos_ref[section] + u * sin_ref[section]
            o_ref[:, og * LANES:(og + 1) * LANES] = head.astype(o_ref.dtype)


def _proj(x, ln, w, layer, cos_tab, sin_tab):
    tm = PROJ_TM
    pos_blocks = SEQ // tm

    def rows(width):
        return pl.BlockSpec((tm, width), lambda i: (i, 0))

    def tab_spec():
        return pl.BlockSpec((2, tm, LANES), lambda i: (0, i % pos_blocks, 0))

    return pl.pallas_call(
        _proj_kernel,
        grid=(TOKENS // tm,),
        in_specs=[
            rows(D_MODEL),
            pl.BlockSpec((1, D_MODEL), lambda i: (0, 0)),
            pl.BlockSpec((None, D_MODEL, PROJ_WIDTH), lambda i: (layer, 0, 0),
                         pipeline_mode=pl.Buffered(1)),
            tab_spec(), tab_spec(),
        ],
        out_specs=[rows(MLA_IN_PAD), rows(MOBA_IN), rows(DIL_IN)],
        out_shape=[
            jax.ShapeDtypeStruct((TOKENS, MLA_IN_PAD), jnp.float32),
            jax.ShapeDtypeStruct((TOKENS, MOBA_IN), jnp.bfloat16),
            jax.ShapeDtypeStruct((TOKENS, DIL_IN), jnp.bfloat16),
        ],
        compiler_params=_cparams(("parallel",), 56),
        name="mix_proj",
    )(x, ln, w, cos_tab, sin_tab)


def _mla_prep_kernel(p_ref, gq_ref, gkv_ref, wuq_ref, wuk_ref, wuv_ref, cos_ref, sin_ref,
                     q_ref, k_ref, v_ref):
    scale = (MLA_NOPE + MLA_ROPE) ** -0.5 * LOG2E
    c = cos_ref[...]
    s = sin_ref[...]

    def rope(t):
        return t * c + pltpu.roll(t, LANES // 2, 1) * s

    cq = _rms(p_ref[:, :MLA_Q_RANK], gq_ref[...]).astype(jnp.bfloat16)
    q = jnp.dot(cq, wuq_ref[...], preferred_element_type=jnp.float32)
    ckv = _rms(p_ref[:, MLA_Q_RANK:MLA_Q_RANK + MLA_KV_RANK], gkv_ref[...]).astype(jnp.bfloat16)
    kn = jnp.dot(ckv, wuk_ref[...], preferred_element_type=jnp.float32)
    v_ref[...] = jnp.dot(ckv, wuv_ref[...],
                         preferred_element_type=jnp.float32).astype(jnp.bfloat16)
    kr = rope(p_ref[:, MLA_Q_RANK + MLA_KV_RANK:]).astype(jnp.bfloat16)
    for h in range(MLA_HEADS):
        b0 = h * MLA_QK_PAD
        q_ref[:, b0:b0 + LANES] = (q[:, b0:b0 + LANES] * scale).astype(jnp.bfloat16)
        q_ref[:, b0 + LANES:b0 + 2 * LANES] = (
            rope(q[:, b0 + LANES:b0 + 2 * LANES]) * scale).astype(jnp.bfloat16)
        k_ref[:, b0:b0 + LANES] = kn[:, h * LANES:(h + 1) * LANES].astype(jnp.bfloat16)
        k_ref[:, b0 + LANES:b0 + 2 * LANES] = kr


def _mla_prep(p, gq, gkv, wuq, wuk, wuv, cos_r, sin_r):
    tm = MLA_PREP_TM
    pos_blocks = SEQ // tm
    qk_w = MLA_HEADS * MLA_QK_PAD
    v_w = MLA_HEADS * MLA_V

    def full(shape):
        return pl.BlockSpec(shape, lambda i: (0, 0))

    return pl.pallas_call(
        _mla_prep_kernel,
        grid=(TOKENS // tm,),
        in_specs=[
            pl.BlockSpec((tm, MLA_IN_PAD), lambda i: (i, 0)),
            full((1, MLA_Q_RANK)), full((1, MLA_KV_RANK)),
            full((MLA_Q_RANK, qk_w)), full((MLA_KV_RANK, v_w)), full((MLA_KV_RANK, v_w)),
            pl.BlockSpec((tm, LANES), lambda i: (i % pos_blocks, 0)),
            pl.BlockSpec((tm, LANES), lambda i: (i % pos_blocks, 0)),
        ],
        out_specs=[
            pl.BlockSpec((tm, qk_w), lambda i: (i, 0)),
            pl.BlockSpec((tm, qk_w), lambda i: (i, 0)),
            pl.BlockSpec((tm, v_w), lambda i: (i, 0)),
        ],
        out_shape=[
            jax.ShapeDtypeStruct((TOKENS, qk_w), jnp.bfloat16),
            jax.ShapeDtypeStruct((TOKENS, qk_w), jnp.bfloat16),
            jax.ShapeDtypeStruct((TOKENS, v_w), jnp.bfloat16),
        ],
        compiler_params=_cparams(("parallel",), 32),
        name="mla_prep",
    )(p, gq, gkv, wuq, wuk, wuv, cos_r, sin_r)


def _fill_v_ones(v_ref, va_ref):
    width = v_ref.shape[1]
    va_ref[:, :width] = v_ref[...]
    va_ref[:, width:] = jnp.ones((v_ref.shape[0], va_ref.shape[1] - width), va_ref.dtype)


_V_ONES_SCRATCH = [pltpu.VMEM((SEQ, 2 * HEAD_DIM), jnp.bfloat16)]


def _mla_attn_kernel(q_ref, k_ref, v_ref, o_ref, va_ref):
    _fill_v_ones(v_ref, va_ref)
    tri = _causal_tri()
    for c in reversed(range(ATT_NT)):
        n = (c + 1) * ATT_T
        s = _dot_nt(q_ref[_tile(c), :], k_ref[0:n, :])
        parts = [s[:, _tile(j)] for j in range(c)]
        parts.append(jnp.where(tri, s[:, _tile(c)], NEG_INF))
        o_ref[_tile(c), :] = _softmax_pv(parts, va_ref)


def _mla_attn(q, k, v):
    return pl.pallas_call(
        _mla_attn_kernel,
        grid=(BATCH, MLA_HEADS),
        in_specs=[_head_spec(MLA_QK_PAD, 0), _head_spec(MLA_QK_PAD, 0), _head_spec(MLA_V, 0)],
        out_specs=_head_spec(MLA_V, 0),
        out_shape=jax.ShapeDtypeStruct((TOKENS, MLA_HEADS * MLA_V), jnp.bfloat16),
        scratch_shapes=_V_ONES_SCRATCH,
        compiler_params=_cparams(("parallel", "parallel"), 32),
        name="mla_attn",
    )(q, k, v)


def _moba_attn_kernel(q_ref, k_ref, v_ref, o_ref, va_ref):
    _fill_v_ones(v_ref, va_ref)
    rid = lax.broadcasted_iota(jnp.int32, (LANES, HEAD_DIM), 0)
    km = jnp.zeros((LANES, HEAD_DIM), jnp.float32)
    for j in range(MOBA_NBLK):
        kj = k_ref[_tile(j), :].astype(jnp.float32)
        mean_j = jnp.sum(kj, axis=0, keepdims=True) * (1.0 / MOBA_BLOCK)
        km = jnp.where(rid == j, mean_j, km)
    km_hi = km.astype(jnp.bfloat16)
    km_lo = (km - km_hi.astype(jnp.float32)).astype(jnp.bfloat16)
    tri = _causal_tri()

    for c in reversed(range(ATT_NT)):
        n = (c + 1) * ATT_T
        q = q_ref[_tile(c), :]
        s = _dot_nt(q, k_ref[0:n, :])
        parts = [s[:, _tile(j)] for j in range(c)]
        if c > MOBA_TOPK:
            gate = _dot_nt(q, km_hi) + _dot_nt(q, km_lo)
            lane = lax.broadcasted_iota(jnp.int32, gate.shape, 1)
            ahead = jnp.zeros(gate.shape, jnp.float32)
            for jp in range(c):
                cj = _lane_col(gate, jp, lane)
                wins = (cj > gate) | ((cj == gate) & (lane > jp))
                ahead = ahead + jnp.where(wins, 1.0, 0.0)
            sel = jnp.where(ahead < MOBA_TOPK, 1.0, 0.0)
            parts = [jnp.where(_lane_col(sel, j, lane) > 0.5, parts[j], NEG_INF)
                     for j in range(c)]
        parts.append(jnp.where(tri, s[:, _tile(c)], NEG_INF))
        o_ref[_tile(c), :] = _softmax_pv(parts, va_ref)


def _moba_attn(qkv):
    return pl.pallas_call(
        _moba_attn_kernel,
        grid=(BATCH, MOBA_HEADS),
        in_specs=[_head_spec(HEAD_DIM, 0), _head_spec(HEAD_DIM, MOBA_HEADS),
                  _head_spec(HEAD_DIM, 2 * MOBA_HEADS)],
        out_specs=_head_spec(HEAD_DIM, 0),
        out_shape=jax.ShapeDtypeStruct((TOKENS, MOBA_HEADS * HEAD_DIM), jnp.bfloat16),
        scratch_shapes=_V_ONES_SCRATCH,
        compiler_params=_cparams(("parallel", "parallel"), 32),
        name="moba_attn",
    )(qkv, qkv, qkv)


def _dil_tables():
    r = np.arange(ATT_T)[:, None]
    c = np.arange(ATT_T)[None, :]
    cnts = []
    for d in range(ATT_NT):
        delta = r - c + ATT_T * d
        cnts.append(sum(((delta >= 0) & (delta <= w) & (delta % dil == 0)).astype(np.float32)
                        for w, dil in DIL_PATTERNS))
    cnt = np.stack(cnts)
    bias = np.where(cnt > 0, 0.0, NEG_INF).astype(np.float32)
    n_weighted = max(d + 1 for d in range(ATT_NT) if cnt[d].max() > 1)
    return bias, cnt[:n_weighted]


def _dil_attn_kernel(q_ref, k_ref, v_ref, bias_ref, cnt_ref, o_ref, va_ref):
    _fill_v_ones(v_ref, va_ref)
    n_weighted = cnt_ref.shape[0]
    for c in reversed(range(ATT_NT)):
        n = (c + 1) * ATT_T
        s = _dot_nt(q_ref[_tile(c), :], k_ref[0:n, :])
        tiles = [s[:, _tile(j)] + bias_ref[c - j] for j in range(c + 1)]
        weights = [cnt_ref[c - j] if c - j < n_weighted else None for j in range(c + 1)]
        o_ref[_tile(c), :] = _softmax_pv(tiles, va_ref, weights)


def _dil_attn(qkv):
    bias, cnt = _dil_tables()
    return pl.pallas_call(
        _dil_attn_kernel,
        grid=(BATCH, DIL_HEADS),
        in_specs=[_head_spec(HEAD_DIM, 0), _head_spec(HEAD_DIM, DIL_HEADS),
                  _head_spec(HEAD_DIM, 2 * DIL_HEADS),
                  pl.BlockSpec(bias.shape, lambda b, h: (0, 0, 0)),
                  pl.BlockSpec(cnt.shape, lambda b, h: (0, 0, 0))],
        out_specs=_head_spec(HEAD_DIM, 0),
        out_shape=jax.ShapeDtypeStruct((TOKENS, DIL_HEADS * HEAD_DIM), jnp.bfloat16),
        scratch_shapes=_V_ONES_SCRATCH,
        compiler_params=_cparams(("parallel", "parallel"), 32),
        name="dil_attn",
    )(qkv, qkv, qkv, jnp.asarray(bias), jnp.asarray(cnt))


def _out_proj_kernel(x_ref, mla_ref, moba_ref, dil_ref, w_ref, o_ref, wb_ref):
    @pl.when(pl.program_id(0) == 0)
    def _():
        wb_ref[...] = w_ref[...].astype(jnp.bfloat16)

    mix = jnp.concatenate([mla_ref[...], moba_ref[...], dil_ref[...]], axis=1)
    o_ref[...] = x_ref[...] + jnp.dot(mix, wb_ref[...], preferred_element_type=jnp.float32)


def _out_proj(x, o_mla, o_moba, o_dil, w_out, layer):
    tm = OUT_TM

    def rows(width):
        return pl.BlockSpec((tm, width), lambda i: (i, 0))

    return pl.pallas_call(
        _out_proj_kernel,
        grid=(TOKENS // tm,),
        in_specs=[rows(D_MODEL), rows(MLA_HEADS * MLA_V), rows(MOBA_HEADS * HEAD_DIM),
                  rows(DIL_HEADS * HEAD_DIM),
                  pl.BlockSpec((None, MIX_WIDTH, D_MODEL), lambda i: (layer, 0, 0),
                               pipeline_mode=pl.Buffered(1))],
        out_specs=rows(D_MODEL),
        out_shape=jax.ShapeDtypeStruct((TOKENS, D_MODEL), jnp.float32),
        scratch_shapes=[pltpu.VMEM((MIX_WIDTH, D_MODEL), jnp.bfloat16)],
        compiler_params=_cparams(("arbitrary",), 56),
        name="out_proj",
    )(x, o_mla, o_moba, o_dil, w_out)


def _rope_tables():
    pos = jnp.arange(SEQ, dtype=jnp.float32)[:, None]
    inv_h = ROPE_THETA ** (-jnp.arange(0, HEAD_DIM, 2, dtype=jnp.float32) / HEAD_DIM)
    ang = pos * inv_h[None, :]
    cos_h = jnp.concatenate([jnp.cos(ang), jnp.cos(ang)], axis=1)
    sin_h = jnp.concatenate([-jnp.sin(ang), jnp.sin(ang)], axis=1)
    scale = HEAD_DIM ** -0.5 * LOG2E
    cos_qk = jnp.stack([cos_h * scale, cos_h])
    sin_qk = jnp.stack([sin_h * scale, sin_h])
    inv_r = ROPE_THETA ** (-jnp.arange(0, MLA_ROPE, 2, dtype=jnp.float32) / MLA_ROPE)
    ang_r = pos * inv_r[None, :]
    z = jnp.zeros_like(ang_r)
    cos_r = jnp.concatenate([jnp.cos(ang_r), z, jnp.cos(ang_r), z], axis=1)
    sin_r = jnp.concatenate([-jnp.sin(ang_r), z, jnp.sin(ang_r), z], axis=1)
    return cos_qk, sin_qk, cos_r, sin_r


def _pad_rope_cols(w):
    half = MLA_ROPE // 2
    z = jnp.zeros(w.shape[:-1] + (half,), w.dtype)
    return jnp.concatenate([w[..., :half], z, w[..., half:], z], axis=-1)


def _prep_mla_weights(w_uq, w_uk, w_uv):
    bf = jnp.bfloat16
    uq = w_uq.reshape(MLA_Q_RANK, MLA_HEADS, MLA_NOPE + MLA_ROPE)
    uq = jnp.concatenate([uq[..., :MLA_NOPE], _pad_rope_cols(uq[..., MLA_NOPE:])], axis=-1)
    uq = uq.reshape(MLA_Q_RANK, MLA_HEADS * MLA_QK_PAD).astype(bf)
    return uq, w_uk.astype(bf), w_uv.astype(bf)


def kernel(x, ln_ffn1, w_ffn1_gate, w_ffn1_up, w_ffn1_down, ln_mix, w_in, g_mla_q, g_mla_kv,
           w_mla_uq, w_mla_uk, w_mla_uv, w_out, ln_ffn2, w_ffn2_gate, w_ffn2_up, w_ffn2_down,
           ln_final):
    cos_qk, sin_qk, cos_r, sin_r = _rope_tables()
    w_proj = w_in.astype(jnp.bfloat16)
    xt = x.reshape(TOKENS, D_MODEL)
    for l in range(DEPTH):
        uq, uk, uv = _prep_mla_weights(w_mla_uq[l], w_mla_uk[l], w_mla_uv[l])
        xt = _ffn(xt, ln_ffn1[l][None], w_ffn1_gate, w_ffn1_up, w_ffn1_down, l)
        p_mla, qkv_moba, qkv_dil = _proj(xt, ln_mix[l][None], w_proj, l, cos_qk, sin_qk)
        q_mla, k_mla, v_mla = _mla_prep(p_mla, g_mla_q[l][None], g_mla_kv[l][None],
                                        uq, uk, uv, cos_r, sin_r)
        o_mla = _mla_attn(q_mla, k_mla, v_mla)
        o_moba = _moba_attn(qkv_moba)
        o_dil = _dil_attn(qkv_dil)
        xt = _out_proj(xt, o_mla, o_moba, o_dil, w_out, l)
        xt = _ffn(xt, ln_ffn2[l][None], w_ffn2_gate, w_ffn2_up, w_ffn2_down, l,
                  ln_final=ln_final[None] if l == DEPTH - 1 else None)
    return xt.reshape(BATCH, SEQ, D_MODEL)
```

```python
import functools

import numpy as np

import jax
import jax.numpy as jnp
from jax import lax
from jax.experimental import pallas as pl
from jax.experimental.pallas import tpu as pltpu

D_MODEL = 2048
BATCH = 4
SEQ = 2048
DEPTH = 2
TOKENS = BATCH * SEQ

HEAD_DIM = 128
MLA_HEADS = 4
MLA_Q_RANK = 512
MLA_KV_RANK = 256
MLA_NOPE = 128
MLA_ROPE = 64
MLA_V = 128
MLA_QK_PAD = 256
MOBA_HEADS = 4
MOBA_BLOCK = 256
MOBA_TOPK = 3
MOBA_NBLK = SEQ // MOBA_BLOCK
DIL_HEADS = 8
DIL_PATTERNS = ((128, 1), (512, 4), (2048, 16))
D_FF = 5632
ROPE_THETA = 10000.0
NORM_EPS = 1e-6
NEG_INF = -1e30
LOG2E = 1.4426950408889634

MLA_IN = MLA_Q_RANK + MLA_KV_RANK + MLA_ROPE
MOBA_IN = 3 * MOBA_HEADS * HEAD_DIM
DIL_IN = 3 * DIL_HEADS * HEAD_DIM
MIX_WIDTH = MLA_HEADS * MLA_V + MOBA_HEADS * HEAD_DIM + DIL_HEADS * HEAD_DIM

LANES = 128
MIB = 1024 * 1024

FFN_TM = 1024
FFN_TF = 512
FFN_TF_HEAD = 256
PROJ_TM = 512
PROJ_TN = 512
ATT_T = 256
ATT_NT = SEQ // ATT_T
OUT_TM = 512
DIL_HPS = 2

assert ATT_T == MOBA_BLOCK


def _cparams(semantics, vmem_mib):
    return pltpu.CompilerParams(dimension_semantics=semantics,
                                vmem_limit_bytes=vmem_mib * MIB)


def _rms(x, g):
    ms = jnp.mean(x * x, axis=-1, keepdims=True)
    return x * lax.rsqrt(ms + NORM_EPS) * g


def _lane_col(a, j, lane):
    return jnp.sum(jnp.where(lane == j, a, 0.0), axis=1, keepdims=True)


def _dot_nt(a, b):
    return lax.dot_general(a, b, (((1,), (1,)), ((), ())),
                           preferred_element_type=jnp.float32)


def _cat(parts):
    return parts[0] if len(parts) == 1 else jnp.concatenate(parts, axis=1)


def _tile(c):
    return slice(c * ATT_T, (c + 1) * ATT_T)


def _causal_tri():
    r = lax.broadcasted_iota(jnp.int32, (ATT_T, ATT_T), 0)
    c = lax.broadcasted_iota(jnp.int32, (ATT_T, ATT_T), 1)
    return c <= r


def _softmax_pv(tiles, v_ref, weights=None):
    mx = tiles[0]
    for t in tiles[1:]:
        mx = jnp.maximum(mx, t)
    m = jnp.max(mx, axis=1, keepdims=True)
    res = None
    for j, t in enumerate(tiles):
        e = jnp.exp2(t - m)
        if weights is not None and weights[j] is not None:
            e = e * weights[j]
        pv = jnp.dot(e.astype(jnp.bfloat16), v_ref[_tile(j), :],
                     preferred_element_type=jnp.float32)
        res = pv if res is None else res + pv
    half = res.shape[1] // 2
    return (res[:, :half] / res[:, half:]).astype(jnp.bfloat16)


def _head_spec(width, col_offset):
    return pl.BlockSpec((SEQ, width), lambda b, h: (b, col_offset + h))


def _ffn_begin(x_ref, ln_ref, h_ref, o_ref):
    @pl.when(pl.program_id(1) == 0)
    def _():
        x = x_ref[...]
        h_ref[...] = _rms(x, ln_ref[...]).astype(jnp.bfloat16)
        o_ref[...] = x


def _ffn_tile(h, wgu, wd):
    tf = wd.shape[0]
    gu = jnp.dot(h, wgu, preferred_element_type=jnp.float32)
    g = gu[:, :tf]
    u = gu[:, tf:]
    a = (g * (1.0 / (1.0 + jnp.exp(-g))) * u * 0.5).astype(jnp.bfloat16)
    return jnp.dot(a, wd, preferred_element_type=jnp.float32)


def _ffn_end(o_ref, lnf_ref):
    if lnf_ref is not None:
        @pl.when(pl.program_id(1) == pl.num_programs(1) - 1)
        def _():
            o_ref[...] = _rms(o_ref[...], lnf_ref[...])


def _ffn_head_kernel(x_ref, ln_ref, wg_ref, wu_ref, wd_ref, *rest, final_norm):
    rest = list(rest)
    lnf_ref = rest.pop(0) if final_norm else None
    o_ref, wgu16_ref, wd16_ref, h_ref = rest
    _ffn_begin(x_ref, ln_ref, h_ref, o_ref)
    wgu = jnp.concatenate([wg_ref[...].astype(jnp.bfloat16),
                           wu_ref[...].astype(jnp.bfloat16)], axis=1)
    wd = wd_ref[...].astype(jnp.bfloat16)
    wgu16_ref[...] = wgu
    wd16_ref[...] = wd
    o_ref[...] += _ffn_tile(h_ref[...], wgu, wd)
    _ffn_end(o_ref, lnf_ref)


def _ffn_rest_kernel(x_ref, ln_ref, wgu_ref, wd_ref, *rest, final_norm):
    rest = list(rest)
    lnf_ref = rest.pop(0) if final_norm else None
    _, o_ref, h_ref = rest
    _ffn_begin(x_ref, ln_ref, h_ref, o_ref)
    h = h_ref[...]
    tf = wgu_ref.shape[2] // 2
    y = None
    for t in range(wgu_ref.shape[0]):
        yt = _ffn_tile(h, wgu_ref[t], wd_ref[t * tf:(t + 1) * tf, :])
        y = yt if y is None else y + yt
    o_ref[...] += y
    _ffn_end(o_ref, lnf_ref)


def _ffn(x, ln, wg, wu, wd, layer, ln_final=None):
    final_norm = ln_final is not None
    tm = FFN_TM
    bf = jnp.bfloat16
    out_shape = jax.ShapeDtypeStruct((TOKENS, D_MODEL), jnp.float32)
    vec_spec = pl.BlockSpec((1, D_MODEL), lambda i, j: (0, 0))
    tail_specs = [vec_spec] if final_norm else []
    tail_args = [ln_final] if final_norm else []
    suffix = "_final" if final_norm else ""

    th = FFN_TF_HEAD
    n_head_tiles = D_FF // th
    group = FFN_TF // th
    scratch = [pltpu.VMEM((tm, D_MODEL), bf)]
    cparams = _cparams(("parallel", "arbitrary"), 60)

    y, wgu16, wd16 = pl.pallas_call(
        functools.partial(_ffn_head_kernel, final_norm=final_norm),
        grid=(1, n_head_tiles),
        in_specs=[pl.BlockSpec((tm, D_MODEL), lambda i, j: (0, 0), pipeline_mode=pl.Buffered(1)),
                  vec_spec,
                  pl.BlockSpec((None, D_MODEL, th), lambda i, j: (layer, 0, j)),
                  pl.BlockSpec((None, D_MODEL, th), lambda i, j: (layer, 0, j)),
                  pl.BlockSpec((None, th, D_MODEL), lambda i, j: (layer, j, 0))] + tail_specs,
        out_specs=[pl.BlockSpec((tm, D_MODEL), lambda i, j: (0, 0)),
                   pl.BlockSpec((None, D_MODEL, 2 * th), lambda i, j: (j, 0, 0)),
                   pl.BlockSpec((th, D_MODEL), lambda i, j: (j, 0))],
        out_shape=[out_shape, jax.ShapeDtypeStruct((n_head_tiles, D_MODEL, 2 * th), bf),
                   jax.ShapeDtypeStruct((D_FF, D_MODEL), bf)],
        scratch_shapes=scratch,
        compiler_params=cparams,
        name="ffn_head" + suffix,
    )(x, ln, wg, wu, wd, *tail_args)

    alias_index = 4 + len(tail_args)
    return pl.pallas_call(
        functools.partial(_ffn_rest_kernel, final_norm=final_norm),
        grid=(TOKENS // tm - 1, n_head_tiles // group),
        in_specs=[pl.BlockSpec((tm, D_MODEL), lambda i, j: (i + 1, 0)),
                  vec_spec,
                  pl.BlockSpec((group, D_MODEL, 2 * th), lambda i, j: (j, 0, 0)),
                  pl.BlockSpec((group * th, D_MODEL), lambda i, j: (j, 0))] + tail_specs
        + [pl.BlockSpec(memory_space=pl.ANY)],
        out_specs=pl.BlockSpec((tm, D_MODEL), lambda i, j: (i + 1, 0)),
        out_shape=out_shape,
        scratch_shapes=scratch,
        input_output_aliases={alias_index: 0},
        compiler_params=cparams,
        name="ffn_rest" + suffix,
    )(x, ln, wgu16, wd16, *tail_args, y)


PROJ_WIDTH = MLA_IN + MOBA_IN + DIL_IN
_N_LAT_GROUPS = (MLA_Q_RANK + MLA_KV_RANK) // LANES
_MOBA_GROUPS = MOBA_IN // LANES
_DIL_GROUPS = DIL_IN // LANES


def _mla_prep(lat, kr, gq_ref, gkv_ref, wuq_ref, wuk_ref, wuv_ref, cos_ref, sin_ref,
              q_ref, k_ref, v_ref):
    scale = (MLA_NOPE + MLA_ROPE) ** -0.5 * LOG2E
    c = cos_ref[...]
    s = sin_ref[...]

    def rope(t):
        return t * c + pltpu.roll(t, LANES // 2, 1) * s

    nq = MLA_Q_RANK // LANES
    cq = _rms(jnp.concatenate(lat[:nq], axis=1), gq_ref[...]).astype(jnp.bfloat16)
    q = jnp.dot(cq, wuq_ref[...], preferred_element_type=jnp.float32)
    ckv = _rms(jnp.concatenate(lat[nq:], axis=1), gkv_ref[...]).astype(jnp.bfloat16)
    kn = jnp.dot(ckv, wuk_ref[...], preferred_element_type=jnp.float32)
    v_ref[...] = jnp.dot(ckv, wuv_ref[...],
                         preferred_element_type=jnp.float32).astype(jnp.bfloat16)
    kr = rope(kr).astype(jnp.bfloat16)
    for h in range(MLA_HEADS):
        b0 = h * MLA_QK_PAD
        q_ref[:, b0:b0 + LANES] = (q[:, b0:b0 + LANES] * scale).astype(jnp.bfloat16)
        q_ref[:, b0 + LANES:b0 + 2 * LANES] = (
            rope(q[:, b0 + LANES:b0 + 2 * LANES]) * scale).astype(jnp.bfloat16)
        k_ref[:, b0:b0 + LANES] = kn[:, h * LANES:(h + 1) * LANES].astype(jnp.bfloat16)
        k_ref[:, b0 + LANES:b0 + 2 * LANES] = kr


def _proj_kernel(x_ref, ln_ref, w_ref, cos_ref, sin_ref, *rest):
    mla_args, (moba_ref, dil_ref) = rest[:-2], rest[-2:]
    h = _rms(x_ref[...], ln_ref[...]).astype(jnp.bfloat16)
    lane = lax.broadcasted_iota(jnp.int32, (x_ref.shape[0], LANES), 1)
    upper = lane >= LANES // 2
    half = MLA_ROPE // 2
    prev = None
    lat = []
    for c0 in range(0, PROJ_WIDTH, PROJ_TN):
        cw = min(PROJ_TN, PROJ_WIDTH - c0)
        y = jnp.dot(h, w_ref[:, c0:c0 + cw], preferred_element_type=jnp.float32)
        for g0 in range(0, cw, LANES):
            g = (c0 + g0) // LANES
            gw = min(LANES, cw - g0)
            yg = y[:, g0:g0 + gw]
            if gw < LANES:
                yg = jnp.concatenate([yg, jnp.zeros((yg.shape[0], LANES - gw), yg.dtype)], axis=1)
            if g < _N_LAT_GROUPS:
                lat.append(yg)
                continue
            if g == _N_LAT_GROUPS:
                kr = (jnp.where(lane < half, yg, 0.0)
                      + jnp.where(upper & (lane < LANES // 2 + half),
                                  pltpu.roll(yg, half, 1), 0.0))
                _mla_prep(lat, kr, *mla_args)
                prev = yg
                continue
            u = jnp.where(upper, prev, yg)
            prev = yg
            og = g - _N_LAT_GROUPS - 1
            if og < _MOBA_GROUPS:
                o_ref, section = moba_ref, og // (_MOBA_GROUPS // 3)
            else:
                og -= _MOBA_GROUPS
                o_ref, section = dil_ref, og // (_DIL_GROUPS // 3)
            head = pltpu.roll(u, LANES // 2, 1)
            if section < 2:
                head = head * cos_ref[section] + u * sin_ref[section]
            o_ref[:, og * LANES:(og + 1) * LANES] = head.astype(o_ref.dtype)


def _proj(x, ln, w, layer, cos_tab, sin_tab, gq, gkv, wuq, wuk, wuv, cos_r, sin_r):
    tm = PROJ_TM
    pos_blocks = SEQ // tm
    qk_w = MLA_HEADS * MLA_QK_PAD
    v_w = MLA_HEADS * MLA_V
    bf = jnp.bfloat16

    def rows(width):
        return pl.BlockSpec((tm, width), lambda i: (i, 0))

    def full(shape):
        return pl.BlockSpec(shape, lambda i: (0, 0))

    def tab_spec():
        return pl.BlockSpec((2, tm, LANES), lambda i: (0, i % pos_blocks, 0))

    def rtab_spec():
        return pl.BlockSpec((tm, LANES), lambda i: (i % pos_blocks, 0))

    return pl.pallas_call(
        _proj_kernel,
        grid=(TOKENS // tm,),
        in_specs=[
            rows(D_MODEL),
            full((1, D_MODEL)),
            pl.BlockSpec((None, D_MODEL, PROJ_WIDTH), lambda i: (layer, 0, 0),
                         pipeline_mode=pl.Buffered(1)),
            tab_spec(), tab_spec(),
            full((1, MLA_Q_RANK)), full((1, MLA_KV_RANK)),
            full((MLA_Q_RANK, qk_w)), full((MLA_KV_RANK, v_w)), full((MLA_KV_RANK, v_w)),
            rtab_spec(), rtab_spec(),
        ],
        out_specs=[rows(qk_w), rows(qk_w), rows(v_w), rows(MOBA_IN), rows(DIL_IN)],
        out_shape=[
            jax.ShapeDtypeStruct((TOKENS, qk_w), bf),
            jax.ShapeDtypeStruct((TOKENS, qk_w), bf),
            jax.ShapeDtypeStruct((TOKENS, v_w), bf),
            jax.ShapeDtypeStruct((TOKENS, MOBA_IN), bf),
            jax.ShapeDtypeStruct((TOKENS, DIL_IN), bf),
        ],
        compiler_params=_cparams(("parallel",), 56),
        name="mix_proj",
    )(x, ln, w, cos_tab, sin_tab, gq, gkv, wuq, wuk, wuv, cos_r, sin_r)


def _fill_v_ones(v_ref, va_ref):
    width = v_ref.shape[1]
    va_ref[:, :width] = v_ref[...]
    va_ref[:, width:] = jnp.ones((v_ref.shape[0], va_ref.shape[1] - width), va_ref.dtype)


_V_ONES_SCRATCH = [pltpu.VMEM((SEQ, 2 * HEAD_DIM), jnp.bfloat16)]


def _mla_attn_kernel(q_ref, k_ref, v_ref, o_ref, va_ref):
    _fill_v_ones(v_ref, va_ref)
    tri = _causal_tri()
    for c in reversed(range(ATT_NT)):
        n = (c + 1) * ATT_T
        s = _dot_nt(q_ref[_tile(c), :], k_ref[0:n, :])
        parts = [s[:, _tile(j)] for j in range(c)]
        parts.append(jnp.where(tri, s[:, _tile(c)], NEG_INF))
        o_ref[_tile(c), :] = _softmax_pv(parts, va_ref)


def _mla_attn(q, k, v):
    return pl.pallas_call(
        _mla_attn_kernel,
        grid=(BATCH, MLA_HEADS),
        in_specs=[_head_spec(MLA_QK_PAD, 0), _head_spec(MLA_QK_PAD, 0), _head_spec(MLA_V, 0)],
        out_specs=_head_spec(MLA_V, 0),
        out_shape=jax.ShapeDtypeStruct((TOKENS, MLA_HEADS * MLA_V), jnp.bfloat16),
        scratch_shapes=_V_ONES_SCRATCH,
        compiler_params=_cparams(("parallel", "parallel"), 32),
        name="mla_attn",
    )(q, k, v)


def _moba_attn_kernel(q_ref, k_ref, v_ref, o_ref, va_ref):
    _fill_v_ones(v_ref, va_ref)
    rid = lax.broadcasted_iota(jnp.int32, (LANES, HEAD_DIM), 0)
    km = jnp.zeros((LANES, HEAD_DIM), jnp.float32)
    for j in range(MOBA_NBLK):
        kj = k_ref[_tile(j), :].astype(jnp.float32)
        mean_j = jnp.sum(kj, axis=0, keepdims=True) * (1.0 / MOBA_BLOCK)
        km = jnp.where(rid == j, mean_j, km)
    km_hi = km.astype(jnp.bfloat16)
    km_lo = (km - km_hi.astype(jnp.float32)).astype(jnp.bfloat16)
    tri = _causal_tri()

    for c in reversed(range(ATT_NT)):
        n = (c + 1) * ATT_T
        q = q_ref[_tile(c), :]
        s = _dot_nt(q, k_ref[0:n, :])
        parts = [s[:, _tile(j)] for j in range(c)]
        if c > MOBA_TOPK:
            gate = _dot_nt(q, km_hi) + _dot_nt(q, km_lo)
            lane = lax.broadcasted_iota(jnp.int32, gate.shape, 1)
            ahead = jnp.zeros(gate.shape, jnp.float32)
            for jp in range(c):
                cj = _lane_col(gate, jp, lane)
                wins = (cj > gate) | ((cj == gate) & (lane > jp))
                ahead = ahead + jnp.where(wins, 1.0, 0.0)
            sel = jnp.where(ahead < MOBA_TOPK, 1.0, 0.0)
            parts = [jnp.where(_lane_col(sel, j, lane) > 0.5, parts[j], NEG_INF)
                     for j in range(c)]
        parts.append(jnp.where(tri, s[:, _tile(c)], NEG_INF))
        o_ref[_tile(c), :] = _softmax_pv(parts, va_ref)


def _moba_attn(qkv):
    return pl.pallas_call(
        _moba_attn_kernel,
        grid=(BATCH, MOBA_HEADS),
        in_specs=[_head_spec(HEAD_DIM, 0), _head_spec(HEAD_DIM, MOBA_HEADS),
                  _head_spec(HEAD_DIM, 2 * MOBA_HEADS)],
        out_specs=_head_spec(HEAD_DIM, 0),
        out_shape=jax.ShapeDtypeStruct((TOKENS, MOBA_HEADS * HEAD_DIM), jnp.bfloat16),
        scratch_shapes=_V_ONES_SCRATCH,
        compiler_params=_cparams(("parallel", "parallel"), 32),
        name="moba_attn",
    )(qkv, qkv, qkv)


def _dil_tables():
    r = np.arange(ATT_T)[:, None]
    c = np.arange(ATT_T)[None, :]
    cnts = []
    for d in range(ATT_NT):
        delta = r - c + ATT_T * d
        cnts.append(sum(((delta >= 0) & (delta <= w) & (delta % dil == 0)).astype(np.float32)
                        for w, dil in DIL_PATTERNS))
    cnt = np.stack(cnts)
    bias = np.where(cnt > 0, 0.0, NEG_INF).astype(np.float32)
    n_weighted = max(d + 1 for d in range(ATT_NT) if cnt[d].max() > 1)
    return bias, cnt[:n_weighted]


def _dil_attn_kernel(q_ref, k_ref, v_ref, bias_ref, cnt_ref, o_ref, va_ref):
    n_weighted = cnt_ref.shape[0]
    for hh in range(DIL_HPS):
        hs = slice(hh * HEAD_DIM, (hh + 1) * HEAD_DIM)
        va = va_ref.at[hh]
        va[:, :HEAD_DIM] = v_ref[:, hs]
        va[:, HEAD_DIM:] = jnp.ones((SEQ, HEAD_DIM), va_ref.dtype)
    for c in reversed(range(ATT_NT)):
        n = (c + 1) * ATT_T
        weights = [cnt_ref[c - j] if c - j < n_weighted else None for j in range(c + 1)]
        for hh in range(DIL_HPS):
            hs = slice(hh * HEAD_DIM, (hh + 1) * HEAD_DIM)
            s = _dot_nt(q_ref[_tile(c), hs], k_ref[0:n, hs])
            tiles = [s[:, _tile(j)] + bias_ref[c - j] for j in range(c + 1)]
            o_ref[_tile(c), hs] = _softmax_pv(tiles, va_ref.at[hh], weights)


def _dil_attn(qkv):
    bias, cnt = _dil_tables()
    return pl.pallas_call(
        _dil_attn_kernel,
        grid=(BATCH, DIL_HEADS // DIL_HPS),
        in_specs=[_head_spec(DIL_HPS * HEAD_DIM, 0),
                  _head_spec(DIL_HPS * HEAD_DIM, DIL_HEADS // DIL_HPS),
                  _head_spec(DIL_HPS * HEAD_DIM, 2 * DIL_HEADS // DIL_HPS),
                  pl.BlockSpec(bias.shape, lambda b, h: (0, 0, 0)),
                  pl.BlockSpec(cnt.shape, lambda b, h: (0, 0, 0))],
        out_specs=_head_spec(DIL_HPS * HEAD_DIM, 0),
        out_shape=jax.ShapeDtypeStruct((TOKENS, DIL_HEADS * HEAD_DIM), jnp.bfloat16),
        scratch_shapes=[pltpu.VMEM((DIL_HPS, SEQ, 2 * HEAD_DIM), jnp.bfloat16)],
        compiler_params=_cparams(("parallel", "parallel"), 32),
        name="dil_attn",
    )(qkv, qkv, qkv, jnp.asarray(bias), jnp.asarray(cnt))


def _out_proj_kernel(x_ref, mla_ref, moba_ref, dil_ref, w_ref, o_ref, wb_ref):
    @pl.when(pl.program_id(0) == 0)
    def _():
        wb_ref[...] = w_ref[...].astype(jnp.bfloat16)

    mix = jnp.concatenate([mla_ref[...], moba_ref[...], dil_ref[...]], axis=1)
    o_ref[...] = x_ref[...] + jnp.dot(mix, wb_ref[...], preferred_element_type=jnp.float32)


def _out_proj(x, o_mla, o_moba, o_dil, w_out, layer):
    tm = OUT_TM

    def rows(width):
        return pl.BlockSpec((tm, width), lambda i: (i, 0))

    return pl.pallas_call(
        _out_proj_kernel,
        grid=(TOKENS // tm,),
        in_specs=[rows(D_MODEL), rows(MLA_HEADS * MLA_V), rows(MOBA_HEADS * HEAD_DIM),
                  rows(DIL_HEADS * HEAD_DIM),
                  pl.BlockSpec((None, MIX_WIDTH, D_MODEL), lambda i: (layer, 0, 0),
                               pipeline_mode=pl.Buffered(1))],
        out_specs=rows(D_MODEL),
        out_shape=jax.ShapeDtypeStruct((TOKENS, D_MODEL), jnp.float32),
        scratch_shapes=[pltpu.VMEM((MIX_WIDTH, D_MODEL), jnp.bfloat16)],
        compiler_params=_cparams(("arbitrary",), 56),
        name="out_proj",
    )(x, o_mla, o_moba, o_dil, w_out)


def _rope_tables():
    pos = jnp.arange(SEQ, dtype=jnp.float32)[:, None]
    inv_h = ROPE_THETA ** (-jnp.arange(0, HEAD_DIM, 2, dtype=jnp.float32) / HEAD_DIM)
    ang = pos * inv_h[None, :]
    cos_h = jnp.concatenate([jnp.cos(ang), jnp.cos(ang)], axis=1)
    sin_h = jnp.concatenate([-jnp.sin(ang), jnp.sin(ang)], axis=1)
    scale = HEAD_DIM ** -0.5 * LOG2E
    cos_qk = jnp.stack([cos_h * scale, cos_h])
    sin_qk = jnp.stack([sin_h * scale, sin_h])
    inv_r = ROPE_THETA ** (-jnp.arange(0, MLA_ROPE, 2, dtype=jnp.float32) / MLA_ROPE)
    ang_r = pos * inv_r[None, :]
    z = jnp.zeros_like(ang_r)
    cos_r = jnp.concatenate([jnp.cos(ang_r), z, jnp.cos(ang_r), z], axis=1)
    sin_r = jnp.concatenate([-jnp.sin(ang_r), z, jnp.sin(ang_r), z], axis=1)
    return cos_qk, sin_qk, cos_r, sin_r


def _pad_rope_cols(w):
    half = MLA_ROPE // 2
    z = jnp.zeros(w.shape[:-1] + (half,), w.dtype)
    return jnp.concatenate([w[..., :half], z, w[..., half:], z], axis=-1)


def _prep_mla_weights(w_uq, w_uk, w_uv):
    bf = jnp.bfloat16
    uq = w_uq.reshape(MLA_Q_RANK, MLA_HEADS, MLA_NOPE + MLA_ROPE)
    uq = jnp.concatenate([uq[..., :MLA_NOPE], _pad_rope_cols(uq[..., MLA_NOPE:])], axis=-1)
    uq = uq.reshape(MLA_Q_RANK, MLA_HEADS * MLA_QK_PAD).astype(bf)
    return uq, w_uk.astype(bf), w_uv.astype(bf)


def kernel(x, ln_ffn1, w_ffn1_gate, w_ffn1_up, w_ffn1_down, ln_mix, w_in, g_mla_q, g_mla_kv,
           w_mla_uq, w_mla_uk, w_mla_uv, w_out, ln_ffn2, w_ffn2_gate, w_ffn2_up, w_ffn2_down,
           ln_final):
    cos_qk, sin_qk, cos_r, sin_r = _rope_tables()
    w_proj = w_in.astype(jnp.bfloat16)
    xt = x.reshape(TOKENS, D_MODEL)
    for l in range(DEPTH):
        uq, uk, uv = _prep_mla_weights(w_mla_uq[l], w_mla_uk[l], w_mla_uv[l])
        xt = _ffn(xt, ln_ffn1[l][None], w_ffn1_gate, w_ffn1_up, w_ffn1_down, l)
        q_mla, k_mla, v_mla, qkv_moba, qkv_dil = _proj(
            xt, ln_mix[l][None], w_proj, l, cos_qk, sin_qk,
            g_mla_q[l][None], g_mla_kv[l][None], uq, uk, uv, cos_r, sin_r)
        o_mla = _mla_attn(q_mla, k_mla, v_mla)
        o_moba = _moba_attn(qkv_moba)
        o_dil = _dil_attn(qkv_dil)
        xt = _out_proj(xt, o_mla, o_moba, o_dil, w_out, l)
        xt = _ffn(xt, ln_ffn2[l][None], w_ffn2_gate, w_ffn2_up, w_ffn2_down, l,
                  ln_final=ln_final[None] if l == DEPTH - 1 else None)
    return xt.reshape(BATCH, SEQ, D_MODEL)
```

```python
import functools

import numpy as np

import jax
import jax.numpy as jnp
from jax import lax
from jax.experimental import pallas as pl
from jax.experimental.pallas import tpu as pltpu

D_MODEL = 2048
BATCH = 4
SEQ = 2048
DEPTH = 2
TOKENS = BATCH * SEQ

HEAD_DIM = 128
MLA_HEADS = 4
MLA_Q_RANK = 512
MLA_KV_RANK = 256
MLA_NOPE = 128
MLA_ROPE = 64
MLA_V = 128
MLA_QK_PAD = 256
MOBA_HEADS = 4
MOBA_BLOCK = 256
MOBA_TOPK = 3
MOBA_NBLK = SEQ // MOBA_BLOCK
DIL_HEADS = 8
DIL_PATTERNS = ((128, 1), (512, 4), (2048, 16))
D_FF = 5632
ROPE_THETA = 10000.0
NORM_EPS = 1e-6
NEG_INF = -1e30
LOG2E = 1.4426950408889634

MLA_IN = MLA_Q_RANK + MLA_KV_RANK + MLA_ROPE
MOBA_IN = 3 * MOBA_HEADS * HEAD_DIM
DIL_IN = 3 * DIL_HEADS * HEAD_DIM
MIX_WIDTH = MLA_HEADS * MLA_V + MOBA_HEADS * HEAD_DIM + DIL_HEADS * HEAD_DIM

LANES = 128
MIB = 1024 * 1024

FFN_TM = 1024
FFN_TF = 512
FFN_TF_HEAD = 256
PROJ_TM = 512
PROJ_TN = 512
ATT_T = 256
ATT_NT = SEQ // ATT_T
ATT_ORDER = tuple(reversed(range(ATT_NT)))
OUT_TM = 512
DIL_HPS = 2
CAST_ROWS = 256

assert ATT_T == MOBA_BLOCK


def _cparams(semantics, vmem_mib):
    return pltpu.CompilerParams(dimension_semantics=semantics,
                                vmem_limit_bytes=vmem_mib * MIB)


def _rms(x, g):
    ms = jnp.mean(x * x, axis=-1, keepdims=True)
    return x * lax.rsqrt(ms + NORM_EPS) * g


def _lane_col(a, j, lane):
    return jnp.sum(jnp.where(lane == j, a, 0.0), axis=1, keepdims=True)


def _dot_nt(a, b):
    return lax.dot_general(a, b, (((1,), (1,)), ((), ())),
                           preferred_element_type=jnp.float32)


def _cat(parts):
    return parts[0] if len(parts) == 1 else jnp.concatenate(parts, axis=1)


def _tile(c):
    return slice(c * ATT_T, (c + 1) * ATT_T)


def _causal_tri():
    r = lax.broadcasted_iota(jnp.int32, (ATT_T, ATT_T), 0)
    c = lax.broadcasted_iota(jnp.int32, (ATT_T, ATT_T), 1)
    return c <= r


def _softmax_pv(tiles, v_ref, weights=None):
    mx = tiles[0]
    for t in tiles[1:]:
        mx = jnp.maximum(mx, t)
    m = jnp.max(mx, axis=1, keepdims=True)
    res = None
    for j, t in enumerate(tiles):
        e = jnp.exp2(t - m)
        if weights is not None and weights[j] is not None:
            e = e * weights[j]
        pv = jnp.dot(e.astype(jnp.bfloat16), v_ref[_tile(j), :],
                     preferred_element_type=jnp.float32)
        res = pv if res is None else res + pv
    half = res.shape[1] // 2
    return (res[:, :half] / res[:, half:]).astype(jnp.bfloat16)


def _head_spec(width, col_offset):
    return pl.BlockSpec((SEQ, width), lambda b, h: (b, col_offset + h))


def _ffn_begin(x_ref, ln_ref, h_ref, o_ref):
    @pl.when(pl.program_id(1) == 0)
    def _():
        x = x_ref[...]
        h_ref[...] = _rms(x, ln_ref[...]).astype(jnp.bfloat16)
        o_ref[...] = x


def _ffn_tile(h, wgu, wd):
    tf = wd.shape[0]
    gu = jnp.dot(h, wgu, preferred_element_type=jnp.float32)
    g = gu[:, :tf]
    u = gu[:, tf:]
    a = (g * (1.0 / (1.0 + jnp.exp(-g))) * u * 0.5).astype(jnp.bfloat16)
    return jnp.dot(a, wd, preferred_element_type=jnp.float32)


def _ffn_end(o_ref, lnf_ref):
    if lnf_ref is not None:
        @pl.when(pl.program_id(1) == pl.num_programs(1) - 1)
        def _():
            o_ref[...] = _rms(o_ref[...], lnf_ref[...])


def _ffn_head_kernel(x_ref, ln_ref, wg_ref, wu_ref, wd_ref, *rest, final_norm):
    rest = list(rest)
    lnf_ref = rest.pop(0) if final_norm else None
    o_ref, wgu16_ref, wd16_ref, h_ref = rest
    _ffn_begin(x_ref, ln_ref, h_ref, o_ref)
    wgu = jnp.concatenate([wg_ref[...].astype(jnp.bfloat16),
                           wu_ref[...].astype(jnp.bfloat16)], axis=1)
    wd = wd_ref[...].astype(jnp.bfloat16)
    wgu16_ref[...] = wgu
    wd16_ref[...] = wd
    o_ref[...] += _ffn_tile(h_ref[...], wgu, wd)
    _ffn_end(o_ref, lnf_ref)


def _ffn_rest_kernel(x_ref, ln_ref, wgu_ref, wd_ref, *rest, final_norm):
    rest = list(rest)
    lnf_ref = rest.pop(0) if final_norm else None
    _, o_ref, h_ref = rest
    _ffn_begin(x_ref, ln_ref, h_ref, o_ref)
    h = h_ref[...]
    tf = wgu_ref.shape[2] // 2
    y = None
    for t in range(wgu_ref.shape[0]):
        yt = _ffn_tile(h, wgu_ref[t], wd_ref[t * tf:(t + 1) * tf, :])
        y = yt if y is None else y + yt
    o_ref[...] += y
    _ffn_end(o_ref, lnf_ref)


def _ffn(x, ln, wg, wu, wd, layer, ln_final=None):
    final_norm = ln_final is not None
    tm = FFN_TM
    bf = jnp.bfloat16
    out_shape = jax.ShapeDtypeStruct((TOKENS, D_MODEL), jnp.float32)
    vec_spec = pl.BlockSpec((1, D_MODEL), lambda i, j: (0, 0))
    tail_specs = [vec_spec] if final_norm else []
    tail_args = [ln_final] if final_norm else []
    suffix = "_final" if final_norm else ""

    th = FFN_TF_HEAD
    n_head_tiles = D_FF // th
    group = FFN_TF // th
    scratch = [pltpu.VMEM((tm, D_MODEL), bf)]
    cparams = _cparams(("parallel", "arbitrary"), 60)

    y, wgu16, wd16 = pl.pallas_call(
        functools.partial(_ffn_head_kernel, final_norm=final_norm),
        grid=(1, n_head_tiles),
        in_specs=[pl.BlockSpec((tm, D_MODEL), lambda i, j: (0, 0), pipeline_mode=pl.Buffered(1)),
                  vec_spec,
                  pl.BlockSpec((None, D_MODEL, th), lambda i, j: (layer, 0, j)),
                  pl.BlockSpec((None, D_MODEL, th), lambda i, j: (layer, 0, j)),
                  pl.BlockSpec((None, th, D_MODEL), lambda i, j: (layer, j, 0))] + tail_specs,
        out_specs=[pl.BlockSpec((tm, D_MODEL), lambda i, j: (0, 0)),
                   pl.BlockSpec((None, D_MODEL, 2 * th), lambda i, j: (j, 0, 0)),
                   pl.BlockSpec((th, D_MODEL), lambda i, j: (j, 0))],
        out_shape=[out_shape, jax.ShapeDtypeStruct((n_head_tiles, D_MODEL, 2 * th), bf),
                   jax.ShapeDtypeStruct((D_FF, D_MODEL), bf)],
        scratch_shapes=scratch,
        compiler_params=cparams,
        name="ffn_head" + suffix,
    )(x, ln, wg, wu, wd, *tail_args)

    alias_index = 4 + len(tail_args)
    return pl.pallas_call(
        functools.partial(_ffn_rest_kernel, final_norm=final_norm),
        grid=(TOKENS // tm - 1, n_head_tiles // group),
        in_specs=[pl.BlockSpec((tm, D_MODEL), lambda i, j: (i + 1, 0)),
                  vec_spec,
                  pl.BlockSpec((group, D_MODEL, 2 * th), lambda i, j: (j, 0, 0)),
                  pl.BlockSpec((group * th, D_MODEL), lambda i, j: (j, 0))] + tail_specs
        + [pl.BlockSpec(memory_space=pl.ANY)],
        out_specs=pl.BlockSpec((tm, D_MODEL), lambda i, j: (i + 1, 0)),
        out_shape=out_shape,
        scratch_shapes=scratch,
        input_output_aliases={alias_index: 0},
        compiler_params=cparams,
        name="ffn_rest" + suffix,
    )(x, ln, wgu16, wd16, *tail_args, y)


PROJ_WIDTH = MLA_IN + MOBA_IN + DIL_IN
_N_LAT_GROUPS = (MLA_Q_RANK + MLA_KV_RANK) // LANES
_MOBA_GROUPS = MOBA_IN // LANES
_DIL_GROUPS = DIL_IN // LANES


def _mla_prep(lat, kr, gq_ref, gkv_ref, wuq_ref, wuk_ref, wuv_ref, cos_ref, sin_ref,
              q_ref, k_ref, v_ref):
    scale = (MLA_NOPE + MLA_ROPE) ** -0.5 * LOG2E
    c = cos_ref[...]
    s = sin_ref[...]

    def rope(t):
        return t * c + pltpu.roll(t, LANES // 2, 1) * s

    nq = MLA_Q_RANK // LANES
    cq = _rms(jnp.concatenate(lat[:nq], axis=1), gq_ref[...]).astype(jnp.bfloat16)
    q = jnp.dot(cq, wuq_ref[...], preferred_element_type=jnp.float32)
    ckv = _rms(jnp.concatenate(lat[nq:], axis=1), gkv_ref[...]).astype(jnp.bfloat16)
    kn = jnp.dot(ckv, wuk_ref[...], preferred_element_type=jnp.float32)
    v_ref[...] = jnp.dot(ckv, wuv_ref[...],
                         preferred_element_type=jnp.float32).astype(jnp.bfloat16)
    kr = rope(kr).astype(jnp.bfloat16)
    for h in range(MLA_HEADS):
        b0 = h * MLA_QK_PAD
        q_ref[:, b0:b0 + LANES] = (q[:, b0:b0 + LANES] * scale).astype(jnp.bfloat16)
        q_ref[:, b0 + LANES:b0 + 2 * LANES] = (
            rope(q[:, b0 + LANES:b0 + 2 * LANES]) * scale).astype(jnp.bfloat16)
        k_ref[:, b0:b0 + LANES] = kn[:, h * LANES:(h + 1) * LANES].astype(jnp.bfloat16)
        k_ref[:, b0 + LANES:b0 + 2 * LANES] = kr


def _proj_kernel(x_ref, ln_ref, w_ref, cos_ref, sin_ref, *rest):
    mla_args, (moba_ref, dil_ref) = rest[:-2], rest[-2:]
    h = _rms(x_ref[...], ln_ref[...]).astype(jnp.bfloat16)
    lane = lax.broadcasted_iota(jnp.int32, (x_ref.shape[0], LANES), 1)
    upper = lane >= LANES // 2
    half = MLA_ROPE // 2
    prev = None
    lat = []
    for c0 in range(0, PROJ_WIDTH, PROJ_TN):
        cw = min(PROJ_TN, PROJ_WIDTH - c0)
        y = jnp.dot(h, w_ref[:, c0:c0 + cw], preferred_element_type=jnp.float32)
        for g0 in range(0, cw, LANES):
            g = (c0 + g0) // LANES
            gw = min(LANES, cw - g0)
            yg = y[:, g0:g0 + gw]
            if gw < LANES:
                yg = jnp.concatenate([yg, jnp.zeros((yg.shape[0], LANES - gw), yg.dtype)], axis=1)
            if g < _N_LAT_GROUPS:
                lat.append(yg)
                continue
            if g == _N_LAT_GROUPS:
                kr = (jnp.where(lane < half, yg, 0.0)
                      + jnp.where(upper & (lane < LANES // 2 + half),
                                  pltpu.roll(yg, half, 1), 0.0))
                _mla_prep(lat, kr, *mla_args)
                prev = yg
                continue
            u = jnp.where(upper, prev, yg)
            prev = yg
            og = g - _N_LAT_GROUPS - 1
            if og < _MOBA_GROUPS:
                o_ref, section = moba_ref, og // (_MOBA_GROUPS // 3)
            else:
                og -= _MOBA_GROUPS
                o_ref, section = dil_ref, og // (_DIL_GROUPS // 3)
            head = pltpu.roll(u, LANES // 2, 1)
            if section < 2:
                head = head * cos_ref[section] + u * sin_ref[section]
            o_ref[:, og * LANES:(og + 1) * LANES] = head.astype(o_ref.dtype)


def _proj(x, ln, w, layer, cos_tab, sin_tab, gq, gkv, wuq, wuk, wuv, cos_r, sin_r):
    tm = PROJ_TM
    pos_blocks = SEQ // tm
    qk_w = MLA_HEADS * MLA_QK_PAD
    v_w = MLA_HEADS * MLA_V
    bf = jnp.bfloat16

    def rows(width):
        return pl.BlockSpec((tm, width), lambda i: (i, 0))

    def full(shape):
        return pl.BlockSpec(shape, lambda i: (0, 0))

    def tab_spec():
        return pl.BlockSpec((2, tm, LANES), lambda i: (0, i % pos_blocks, 0))

    def rtab_spec():
        return pl.BlockSpec((tm, LANES), lambda i: (i % pos_blocks, 0))

    return pl.pallas_call(
        _proj_kernel,
        grid=(TOKENS // tm,),
        in_specs=[
            rows(D_MODEL),
            full((1, D_MODEL)),
            pl.BlockSpec((None, D_MODEL, PROJ_WIDTH), lambda i: (layer, 0, 0),
                         pipeline_mode=pl.Buffered(1)),
            tab_spec(), tab_spec(),
            full((1, MLA_Q_RANK)), full((1, MLA_KV_RANK)),
            full((MLA_Q_RANK, qk_w)), full((MLA_KV_RANK, v_w)), full((MLA_KV_RANK, v_w)),
            rtab_spec(), rtab_spec(),
        ],
        out_specs=[rows(qk_w), rows(qk_w), rows(v_w), rows(MOBA_IN), rows(DIL_IN)],
        out_shape=[
            jax.ShapeDtypeStruct((TOKENS, qk_w), bf),
            jax.ShapeDtypeStruct((TOKENS, qk_w), bf),
            jax.ShapeDtypeStruct((TOKENS, v_w), bf),
            jax.ShapeDtypeStruct((TOKENS, MOBA_IN), bf),
            jax.ShapeDtypeStruct((TOKENS, DIL_IN), bf),
        ],
        compiler_params=_cparams(("parallel",), 56),
        name="mix_proj",
    )(x, ln, w, cos_tab, sin_tab, gq, gkv, wuq, wuk, wuv, cos_r, sin_r)


def _fill_v_ones(v_ref, va_ref):
    width = v_ref.shape[1]
    va_ref[:, :width] = v_ref[...]
    va_ref[:, width:] = jnp.ones((v_ref.shape[0], va_ref.shape[1] - width), va_ref.dtype)


_V_ONES_SCRATCH = [pltpu.VMEM((SEQ, 2 * HEAD_DIM), jnp.bfloat16)]


def _mla_attn_kernel(q_ref, k_ref, v_ref, o_ref, va_ref):
    _fill_v_ones(v_ref, va_ref)
    tri = _causal_tri()
    for c in ATT_ORDER:
        n = (c + 1) * ATT_T
        s = _dot_nt(q_ref[_tile(c), :], k_ref[0:n, :])
        parts = [s[:, _tile(j)] for j in range(c)]
        parts.append(jnp.where(tri, s[:, _tile(c)], NEG_INF))
        o_ref[_tile(c), :] = _softmax_pv(parts, va_ref)


def _mla_attn(q, k, v):
    return pl.pallas_call(
        _mla_attn_kernel,
        grid=(BATCH, MLA_HEADS),
        in_specs=[_head_spec(MLA_QK_PAD, 0), _head_spec(MLA_QK_PAD, 0), _head_spec(MLA_V, 0)],
        out_specs=_head_spec(MLA_V, 0),
        out_shape=jax.ShapeDtypeStruct((TOKENS, MLA_HEADS * MLA_V), jnp.bfloat16),
        scratch_shapes=_V_ONES_SCRATCH,
        compiler_params=_cparams(("parallel", "parallel"), 32),
        name="mla_attn",
    )(q, k, v)


def _moba_attn_kernel(q_ref, k_ref, v_ref, o_ref, va_ref):
    _fill_v_ones(v_ref, va_ref)
    rid = lax.broadcasted_iota(jnp.int32, (LANES, HEAD_DIM), 0)
    km = jnp.zeros((LANES, HEAD_DIM), jnp.float32)
    for j in range(MOBA_NBLK):
        kj = k_ref[_tile(j), :].astype(jnp.float32)
        mean_j = jnp.sum(kj, axis=0, keepdims=True) * (1.0 / MOBA_BLOCK)
        km = jnp.where(rid == j, mean_j, km)
    km_hi = km.astype(jnp.bfloat16)
    km_lo = (km - km_hi.astype(jnp.float32)).astype(jnp.bfloat16)
    tri = _causal_tri()

    for c in ATT_ORDER:
        n = (c + 1) * ATT_T
        q = q_ref[_tile(c), :]
        s = _dot_nt(q, k_ref[0:n, :])
        parts = [s[:, _tile(j)] for j in range(c)]
        if c > MOBA_TOPK:
            gate = _dot_nt(q, km_hi) + _dot_nt(q, km_lo)
            lane = lax.broadcasted_iota(jnp.int32, gate.shape, 1)
            ahead = jnp.zeros(gate.shape, jnp.float32)
            for jp in range(c):
                cj = _lane_col(gate, jp, lane)
                wins = (cj > gate) | ((cj == gate) & (lane > jp))
                ahead = ahead + jnp.where(wins, 1.0, 0.0)
            sel = jnp.where(ahead < MOBA_TOPK, 1.0, 0.0)
            parts = [jnp.where(_lane_col(sel, j, lane) > 0.5, parts[j], NEG_INF)
                     for j in range(c)]
        parts.append(jnp.where(tri, s[:, _tile(c)], NEG_INF))
        o_ref[_tile(c), :] = _softmax_pv(parts, va_ref)


def _moba_attn(qkv):
    return pl.pallas_call(
        _moba_attn_kernel,
        grid=(BATCH, MOBA_HEADS),
        in_specs=[_head_spec(HEAD_DIM, 0), _head_spec(HEAD_DIM, MOBA_HEADS),
                  _head_spec(HEAD_DIM, 2 * MOBA_HEADS)],
        out_specs=_head_spec(HEAD_DIM, 0),
        out_shape=jax.ShapeDtypeStruct((TOKENS, MOBA_HEADS * HEAD_DIM), jnp.bfloat16),
        scratch_shapes=_V_ONES_SCRATCH,
        compiler_params=_cparams(("parallel", "parallel"), 32),
        name="moba_attn",
    )(qkv, qkv, qkv)


def _dil_tables():
    r = np.arange(ATT_T)[:, None]
    c = np.arange(ATT_T)[None, :]
    cnts = []
    for d in range(ATT_NT):
        delta = r - c + ATT_T * d
        cnts.append(sum(((delta >= 0) & (delta <= w) & (delta % dil == 0)).astype(np.float32)
                        for w, dil in DIL_PATTERNS))
    cnt = np.stack(cnts)
    bias = np.where(cnt > 0, 0.0, NEG_INF).astype(np.float32)
    n_weighted = max(d + 1 for d in range(ATT_NT) if cnt[d].max() > 1)
    return bias, cnt[:n_weighted]


def _dil_attn_kernel(q_ref, k_ref, v_ref, bias_ref, cnt_ref, o_ref, va_ref):
    n_weighted = cnt_ref.shape[0]
    for hh in range(DIL_HPS):
        hs = slice(hh * HEAD_DIM, (hh + 1) * HEAD_DIM)
        va = va_ref.at[hh]
        va[:, :HEAD_DIM] = v_ref[:, hs]
        va[:, HEAD_DIM:] = jnp.ones((SEQ, HEAD_DIM), va_ref.dtype)
    for c in ATT_ORDER:
        n = (c + 1) * ATT_T
        weights = [cnt_ref[c - j] if c - j < n_weighted else None for j in range(c + 1)]
        for hh in range(DIL_HPS):
            hs = slice(hh * HEAD_DIM, (hh + 1) * HEAD_DIM)
            s = _dot_nt(q_ref[_tile(c), hs], k_ref[0:n, hs])
            tiles = [s[:, _tile(j)] + bias_ref[c - j] for j in range(c + 1)]
            o_ref[_tile(c), hs] = _softmax_pv(tiles, va_ref.at[hh], weights)


def _dil_attn(qkv):
    bias, cnt = _dil_tables()
    return pl.pallas_call(
        _dil_attn_kernel,
        grid=(BATCH, DIL_HEADS // DIL_HPS),
        in_specs=[_head_spec(DIL_HPS * HEAD_DIM, 0),
                  _head_spec(DIL_HPS * HEAD_DIM, DIL_HEADS // DIL_HPS),
                  _head_spec(DIL_HPS * HEAD_DIM, 2 * DIL_HEADS // DIL_HPS),
                  pl.BlockSpec(bias.shape, lambda b, h: (0, 0, 0)),
                  pl.BlockSpec(cnt.shape, lambda b, h: (0, 0, 0))],
        out_specs=_head_spec(DIL_HPS * HEAD_DIM, 0),
        out_shape=jax.ShapeDtypeStruct((TOKENS, DIL_HEADS * HEAD_DIM), jnp.bfloat16),
        scratch_shapes=[pltpu.VMEM((DIL_HPS, SEQ, 2 * HEAD_DIM), jnp.bfloat16)],
        compiler_params=_cparams(("parallel", "parallel"), 32),
        name="dil_attn",
    )(qkv, qkv, qkv, jnp.asarray(bias), jnp.asarray(cnt))


def _out_proj_kernel(x_ref, mla_ref, moba_ref, dil_ref, w_ref, o_ref, wb_ref):
    @pl.when(pl.program_id(0) == 0)
    def _():
        wb_ref[...] = w_ref[...].astype(jnp.bfloat16)

    mix = jnp.concatenate([mla_ref[...], moba_ref[...], dil_ref[...]], axis=1)
    o_ref[...] = x_ref[...] + jnp.dot(mix, wb_ref[...], preferred_element_type=jnp.float32)


def _out_proj(x, o_mla, o_moba, o_dil, w_out, layer):
    tm = OUT_TM

    def rows(width):
        return pl.BlockSpec((tm, width), lambda i: (i, 0))

    return pl.pallas_call(
        _out_proj_kernel,
        grid=(TOKENS // tm,),
        in_specs=[rows(D_MODEL), rows(MLA_HEADS * MLA_V), rows(MOBA_HEADS * HEAD_DIM),
                  rows(DIL_HEADS * HEAD_DIM),
                  pl.BlockSpec((None, MIX_WIDTH, D_MODEL), lambda i: (layer, 0, 0),
                               pipeline_mode=pl.Buffered(1))],
        out_specs=rows(D_MODEL),
        out_shape=jax.ShapeDtypeStruct((TOKENS, D_MODEL), jnp.float32),
        scratch_shapes=[pltpu.VMEM((MIX_WIDTH, D_MODEL), jnp.bfloat16)],
        compiler_params=_cparams(("arbitrary",), 56),
        name="out_proj",
    )(x, o_mla, o_moba, o_dil, w_out)


def _cast_kernel(w_ref, o_ref):
    o_ref[...] = w_ref[...].astype(o_ref.dtype)


def _cast_bf16(w):
    depth, rows, cols = w.shape
    tr = CAST_ROWS
    spec = pl.BlockSpec((None, tr, cols), lambda l, i: (l, i, 0))
    return pl.pallas_call(
        _cast_kernel,
        grid=(depth, rows // tr),
        in_specs=[spec],
        out_specs=spec,
        out_shape=jax.ShapeDtypeStruct(w.shape, jnp.bfloat16),
        compiler_params=_cparams(("parallel", "parallel"), 32),
        name="cast_bf16",
    )(w)


def _rope_tables():
    pos = jnp.arange(SEQ, dtype=jnp.float32)[:, None]
    inv_h = ROPE_THETA ** (-jnp.arange(0, HEAD_DIM, 2, dtype=jnp.float32) / HEAD_DIM)
    ang = pos * inv_h[None, :]
    cos_h = jnp.concatenate([jnp.cos(ang), jnp.cos(ang)], axis=1)
    sin_h = jnp.concatenate([-jnp.sin(ang), jnp.sin(ang)], axis=1)
    scale = HEAD_DIM ** -0.5 * LOG2E
    cos_qk = jnp.stack([cos_h * scale, cos_h])
    sin_qk = jnp.stack([sin_h * scale, sin_h])
    inv_r = ROPE_THETA ** (-jnp.arange(0, MLA_ROPE, 2, dtype=jnp.float32) / MLA_ROPE)
    ang_r = pos * inv_r[None, :]
    z = jnp.zeros_like(ang_r)
    cos_r = jnp.concatenate([jnp.cos(ang_r), z, jnp.cos(ang_r), z], axis=1)
    sin_r = jnp.concatenate([-jnp.sin(ang_r), z, jnp.sin(ang_r), z], axis=1)
    return cos_qk, sin_qk, cos_r, sin_r


def _pad_rope_cols(w):
    half = MLA_ROPE // 2
    z = jnp.zeros(w.shape[:-1] + (half,), w.dtype)
    return jnp.concatenate([w[..., :half], z, w[..., half:], z], axis=-1)


def _prep_mla_weights(w_uq, w_uk, w_uv):
    bf = jnp.bfloat16
    uq = w_uq.reshape(MLA_Q_RANK, MLA_HEADS, MLA_NOPE + MLA_ROPE)
    uq = jnp.concatenate([uq[..., :MLA_NOPE], _pad_rope_cols(uq[..., MLA_NOPE:])], axis=-1)
    uq = uq.reshape(MLA_Q_RANK, MLA_HEADS * MLA_QK_PAD).astype(bf)
    return uq, w_uk.astype(bf), w_uv.astype(bf)


def kernel(x, ln_ffn1, w_ffn1_gate, w_ffn1_up, w_ffn1_down, ln_mix, w_in, g_mla_q, g_mla_kv,
           w_mla_uq, w_mla_uk, w_mla_uv, w_out, ln_ffn2, w_ffn2_gate, w_ffn2_up, w_ffn2_down,
           ln_final):
    cos_qk, sin_qk, cos_r, sin_r = _rope_tables()
    w_proj = _cast_bf16(w_in)
    xt = x.reshape(TOKENS, D_MODEL)
    for l in range(DEPTH):
        uq, uk, uv = _prep_mla_weights(w_mla_uq[l], w_mla_uk[l], w_mla_uv[l])
        xt = _ffn(xt, ln_ffn1[l][None], w_ffn1_gate, w_ffn1_up, w_ffn1_down, l)
        q_mla, k_mla, v_mla, qkv_moba, qkv_dil = _proj(
            xt, ln_mix[l][None], w_proj, l, cos_qk, sin_qk,
            g_mla_q[l][None], g_mla_kv[l][None], uq, uk, uv, cos_r, sin_r)
        o_mla = _mla_attn(q_mla, k_mla, v_mla)
        o_moba = _moba_attn(qkv_moba)
        o_dil = _dil_attn(qkv_dil)
        xt = _out_proj(xt, o_mla, o_moba, o_dil, w_out, l)
        xt = _ffn(xt, ln_ffn2[l][None], w_ffn2_gate, w_ffn2_up, w_ffn2_down, l,
                  ln_final=ln_final[None] if l == DEPTH - 1 else None)
    return xt.reshape(BATCH, SEQ, D_MODEL)
```

```python
import functools

import numpy as np

import jax
import jax.numpy as jnp
from jax import lax
from jax.experimental import pallas as pl
from jax.experimental.pallas import tpu as pltpu

D_MODEL = 2048
BATCH = 4
SEQ = 2048
DEPTH = 2
TOKENS = BATCH * SEQ

HEAD_DIM = 128
MLA_HEADS = 4
MLA_Q_RANK = 512
MLA_KV_RANK = 256
MLA_NOPE = 128
MLA_ROPE = 64
MLA_V = 128
MLA_QK_PAD = 256
MOBA_HEADS = 4
MOBA_BLOCK = 256
MOBA_TOPK = 3
MOBA_NBLK = SEQ // MOBA_BLOCK
DIL_HEADS = 8
DIL_PATTERNS = ((128, 1), (512, 4), (2048, 16))
D_FF = 5632
ROPE_THETA = 10000.0
NORM_EPS = 1e-6
NEG_INF = -1e30
LOG2E = 1.4426950408889634

MLA_IN = MLA_Q_RANK + MLA_KV_RANK + MLA_ROPE
MOBA_IN = 3 * MOBA_HEADS * HEAD_DIM
DIL_IN = 3 * DIL_HEADS * HEAD_DIM
MIX_WIDTH = MLA_HEADS * MLA_V + MOBA_HEADS * HEAD_DIM + DIL_HEADS * HEAD_DIM

LANES = 128
MIB = 1024 * 1024

FFN_TM = 1024
FFN_TF = 512
FFN_TF_HEAD = 256
PROJ_TM = 512
PROJ_TN = 512
ATT_T = 256
ATT_NT = SEQ // ATT_T
ATT_ORDER = tuple(reversed(range(ATT_NT)))
OUT_TM = 512
DIL_HPS = 2

assert ATT_T == MOBA_BLOCK


def _cparams(semantics, vmem_mib):
    return pltpu.CompilerParams(dimension_semantics=semantics,
                                vmem_limit_bytes=vmem_mib * MIB)


def _rms(x, g):
    ms = jnp.mean(x * x, axis=-1, keepdims=True)
    return x * lax.rsqrt(ms + NORM_EPS) * g


def _lane_col(a, j, lane):
    return jnp.sum(jnp.where(lane == j, a, 0.0), axis=1, keepdims=True)


def _dot_nt(a, b):
    return lax.dot_general(a, b, (((1,), (1,)), ((), ())),
                           preferred_element_type=jnp.float32)


def _cat(parts):
    return parts[0] if len(parts) == 1 else jnp.concatenate(parts, axis=1)


def _tile(c):
    return slice(c * ATT_T, (c + 1) * ATT_T)


def _causal_tri():
    r = lax.broadcasted_iota(jnp.int32, (ATT_T, ATT_T), 0)
    c = lax.broadcasted_iota(jnp.int32, (ATT_T, ATT_T), 1)
    return c <= r


def _softmax_pv(tiles, v_ref, weights=None):
    mx = tiles[0]
    for t in tiles[1:]:
        mx = jnp.maximum(mx, t)
    m = jnp.max(mx, axis=1, keepdims=True)
    res = None
    for j, t in enumerate(tiles):
        e = jnp.exp2(t - m)
        if weights is not None and weights[j] is not None:
            e = e * weights[j]
        pv = jnp.dot(e.astype(jnp.bfloat16), v_ref[_tile(j), :],
                     preferred_element_type=jnp.float32)
        res = pv if res is None else res + pv
    half = res.shape[1] // 2
    return (res[:, :half] / res[:, half:]).astype(jnp.bfloat16)


def _head_spec(width, col_offset):
    return pl.BlockSpec((SEQ, width), lambda b, h: (b, col_offset + h))


def _ffn_begin(x_ref, ln_ref, h_ref, o_ref):
    @pl.when(pl.program_id(1) == 0)
    def _():
        x = x_ref[...]
        h_ref[...] = _rms(x, ln_ref[...]).astype(jnp.bfloat16)
        o_ref[...] = x


def _ffn_tile(h, wgu, wd):
    tf = wd.shape[0]
    gu = jnp.dot(h, wgu, preferred_element_type=jnp.float32)
    g = gu[:, :tf]
    u = gu[:, tf:]
    a = (g * (1.0 / (1.0 + jnp.exp(-g))) * u * 0.5).astype(jnp.bfloat16)
    return jnp.dot(a, wd, preferred_element_type=jnp.float32)


def _ffn_end(o_ref, lnf_ref):
    if lnf_ref is not None:
        @pl.when(pl.program_id(1) == pl.num_programs(1) - 1)
        def _():
            o_ref[...] = _rms(o_ref[...], lnf_ref[...])


def _ffn_head_kernel(x_ref, ln_ref, wg_ref, wu_ref, wd_ref, *rest, final_norm):
    rest = list(rest)
    lnf_ref = rest.pop(0) if final_norm else None
    o_ref, wgu16_ref, wd16_ref, h_ref = rest
    _ffn_begin(x_ref, ln_ref, h_ref, o_ref)
    wgu = jnp.concatenate([wg_ref[...].astype(jnp.bfloat16),
                           wu_ref[...].astype(jnp.bfloat16)], axis=1)
    wd = wd_ref[...].astype(jnp.bfloat16)
    wgu16_ref[...] = wgu
    wd16_ref[...] = wd
    o_ref[...] += _ffn_tile(h_ref[...], wgu, wd)
    _ffn_end(o_ref, lnf_ref)


def _ffn_rest_kernel(x_ref, ln_ref, wgu_ref, wd_ref, *rest, final_norm):
    rest = list(rest)
    lnf_ref = rest.pop(0) if final_norm else None
    _, o_ref, h_ref = rest
    _ffn_begin(x_ref, ln_ref, h_ref, o_ref)
    h = h_ref[...]
    tf = wgu_ref.shape[2] // 2
    y = None
    for t in range(wgu_ref.shape[0]):
        yt = _ffn_tile(h, wgu_ref[t], wd_ref[t * tf:(t + 1) * tf, :])
        y = yt if y is None else y + yt
    o_ref[...] += y
    _ffn_end(o_ref, lnf_ref)


def _ffn(x, ln, wg, wu, wd, layer, ln_final=None):
    final_norm = ln_final is not None
    tm = FFN_TM
    bf = jnp.bfloat16
    out_shape = jax.ShapeDtypeStruct((TOKENS, D_MODEL), jnp.float32)
    vec_spec = pl.BlockSpec((1, D_MODEL), lambda i, j: (0, 0))
    tail_specs = [vec_spec] if final_norm else []
    tail_args = [ln_final] if final_norm else []
    suffix = "_final" if final_norm else ""

    th = FFN_TF_HEAD
    n_head_tiles = D_FF // th
    group = FFN_TF // th
    scratch = [pltpu.VMEM((tm, D_MODEL), bf)]
    cparams = _cparams(("parallel", "arbitrary"), 60)

    y, wgu16, wd16 = pl.pallas_call(
        functools.partial(_ffn_head_kernel, final_norm=final_norm),
        grid=(1, n_head_tiles),
        in_specs=[pl.BlockSpec((tm, D_MODEL), lambda i, j: (0, 0), pipeline_mode=pl.Buffered(1)),
                  vec_spec,
                  pl.BlockSpec((None, D_MODEL, th), lambda i, j: (layer, 0, j)),
                  pl.BlockSpec((None, D_MODEL, th), lambda i, j: (layer, 0, j)),
                  pl.BlockSpec((None, th, D_MODEL), lambda i, j: (layer, j, 0))] + tail_specs,
        out_specs=[pl.BlockSpec((tm, D_MODEL), lambda i, j: (0, 0)),
                   pl.BlockSpec((None, D_MODEL, 2 * th), lambda i, j: (j, 0, 0)),
                   pl.BlockSpec((th, D_MODEL), lambda i, j: (j, 0))],
        out_shape=[out_shape, jax.ShapeDtypeStruct((n_head_tiles, D_MODEL, 2 * th), bf),
                   jax.ShapeDtypeStruct((D_FF, D_MODEL), bf)],
        scratch_shapes=scratch,
        compiler_params=cparams,
        name="ffn_head" + suffix,
    )(x, ln, wg, wu, wd, *tail_args)

    alias_index = 4 + len(tail_args)
    return pl.pallas_call(
        functools.partial(_ffn_rest_kernel, final_norm=final_norm),
        grid=(TOKENS // tm - 1, n_head_tiles // group),
        in_specs=[pl.BlockSpec((tm, D_MODEL), lambda i, j: (i + 1, 0)),
                  vec_spec,
                  pl.BlockSpec((group, D_MODEL, 2 * th), lambda i, j: (j, 0, 0)),
                  pl.BlockSpec((group * th, D_MODEL), lambda i, j: (j, 0))] + tail_specs
        + [pl.BlockSpec(memory_space=pl.ANY)],
        out_specs=pl.BlockSpec((tm, D_MODEL), lambda i, j: (i + 1, 0)),
        out_shape=out_shape,
        scratch_shapes=scratch,
        input_output_aliases={alias_index: 0},
        compiler_params=cparams,
        name="ffn_rest" + suffix,
    )(x, ln, wgu16, wd16, *tail_args, y)


PROJ_WIDTH = MLA_IN + MOBA_IN + DIL_IN


def _mla_prep(lat, kr, gq_ref, gkv_ref, wuq_ref, wuk_ref, wuv_ref, cos_ref, sin_ref,
              q_ref, k_ref, v_ref):
    scale = (MLA_NOPE + MLA_ROPE) ** -0.5 * LOG2E
    c = cos_ref[...]
    s = sin_ref[...]

    def rope(t):
        return t * c + pltpu.roll(t, LANES // 2, 1) * s

    nq = MLA_Q_RANK // LANES
    cq = _rms(jnp.concatenate(lat[:nq], axis=1), gq_ref[...]).astype(jnp.bfloat16)
    q = jnp.dot(cq, wuq_ref[...], preferred_element_type=jnp.float32)
    ckv = _rms(jnp.concatenate(lat[nq:], axis=1), gkv_ref[...]).astype(jnp.bfloat16)
    kn = jnp.dot(ckv, wuk_ref[...], preferred_element_type=jnp.float32)
    v_ref[...] = jnp.dot(ckv, wuv_ref[...],
                         preferred_element_type=jnp.float32).astype(jnp.bfloat16)
    kr = rope(kr).astype(jnp.bfloat16)
    for h in range(MLA_HEADS):
        b0 = h * MLA_QK_PAD
        q_ref[:, b0:b0 + LANES] = (q[:, b0:b0 + LANES] * scale).astype(jnp.bfloat16)
        q_ref[:, b0 + LANES:b0 + 2 * LANES] = (
            rope(q[:, b0 + LANES:b0 + 2 * LANES]) * scale).astype(jnp.bfloat16)
        k_ref[:, b0:b0 + LANES] = kn[:, h * LANES:(h + 1) * LANES].astype(jnp.bfloat16)
        k_ref[:, b0 + LANES:b0 + 2 * LANES] = kr


def _proj_kernel(x_ref, ln_ref, wt_ref, cos_ref, sin_ref, *rest):
    mla_args, (moba_ref, dil_ref) = rest[:-2], rest[-2:]
    h = _rms(x_ref[...], ln_ref[...]).astype(jnp.bfloat16)

    def proj(r0, rows):
        return _dot_nt(h, wt_ref[r0:r0 + rows, :])

    n_lat = MLA_Q_RANK + MLA_KV_RANK
    lat = []
    for r0 in range(0, n_lat, PROJ_TN):
        y = proj(r0, min(PROJ_TN, n_lat - r0))
        lat += [y[:, g0:g0 + LANES] for g0 in range(0, y.shape[1], LANES)]
    yk = proj(n_lat, LANES)
    lane = lax.broadcasted_iota(jnp.int32, yk.shape, 1)
    half = MLA_ROPE // 2
    kr = (jnp.where(lane < half, yk, 0.0)
          + jnp.where((lane >= LANES // 2) & (lane < LANES // 2 + half),
                      pltpu.roll(yk, half, 1), 0.0))
    _mla_prep(lat, kr, *mla_args)

    row = MLA_IN
    for o_ref, width in ((moba_ref, MOBA_IN), (dil_ref, DIL_IN)):
        for c0 in range(0, width, PROJ_TN):
            y = proj(row + c0, PROJ_TN)
            section = c0 // (width // 3)
            if section == 2:
                o_ref[:, c0:c0 + PROJ_TN] = y.astype(o_ref.dtype)
                continue
            c = cos_ref[section]
            s = sin_ref[section]
            for g0 in range(0, PROJ_TN, LANES):
                yg = y[:, g0:g0 + LANES]
                o_ref[:, c0 + g0:c0 + g0 + LANES] = (
                    yg * c + pltpu.roll(yg, LANES // 2, 1) * s).astype(o_ref.dtype)
        row += width


def _proj(x, ln, w, layer, cos_tab, sin_tab, gq, gkv, wuq, wuk, wuv, cos_r, sin_r):
    tm = PROJ_TM
    pos_blocks = SEQ // tm
    qk_w = MLA_HEADS * MLA_QK_PAD
    v_w = MLA_HEADS * MLA_V
    bf = jnp.bfloat16

    def rows(width):
        return pl.BlockSpec((tm, width), lambda i: (i, 0))

    def full(shape):
        return pl.BlockSpec(shape, lambda i: (0, 0))

    def tab_spec():
        return pl.BlockSpec((2, tm, LANES), lambda i: (0, i % pos_blocks, 0))

    def rtab_spec():
        return pl.BlockSpec((tm, LANES), lambda i: (i % pos_blocks, 0))

    return pl.pallas_call(
        _proj_kernel,
        grid=(TOKENS // tm,),
        in_specs=[
            rows(D_MODEL),
            full((1, D_MODEL)),
            pl.BlockSpec((None, PROJ_WIDTH, D_MODEL), lambda i: (layer, 0, 0),
                         pipeline_mode=pl.Buffered(1)),
            tab_spec(), tab_spec(),
            full((1, MLA_Q_RANK)), full((1, MLA_KV_RANK)),
            full((MLA_Q_RANK, qk_w)), full((MLA_KV_RANK, v_w)), full((MLA_KV_RANK, v_w)),
            rtab_spec(), rtab_spec(),
        ],
        out_specs=[rows(qk_w), rows(qk_w), rows(v_w), rows(MOBA_IN), rows(DIL_IN)],
        out_shape=[
            jax.ShapeDtypeStruct((TOKENS, qk_w), bf),
            jax.ShapeDtypeStruct((TOKENS, qk_w), bf),
            jax.ShapeDtypeStruct((TOKENS, v_w), bf),
            jax.ShapeDtypeStruct((TOKENS, MOBA_IN), bf),
            jax.ShapeDtypeStruct((TOKENS, DIL_IN), bf),
        ],
        compiler_params=_cparams(("parallel",), 56),
        name="mix_proj",
    )(x, ln, w, cos_tab, sin_tab, gq, gkv, wuq, wuk, wuv, cos_r, sin_r)


def _fill_v_ones(v_ref, va_ref):
    width = v_ref.shape[1]
    va_ref[:, :width] = v_ref[...]
    va_ref[:, width:] = jnp.ones((v_ref.shape[0], va_ref.shape[1] - width), va_ref.dtype)


_V_ONES_SCRATCH = [pltpu.VMEM((SEQ, 2 * HEAD_DIM), jnp.bfloat16)]


def _mla_attn_kernel(q_ref, k_ref, v_ref, o_ref, va_ref):
    _fill_v_ones(v_ref, va_ref)
    tri = _causal_tri()
    for c in ATT_ORDER:
        n = (c + 1) * ATT_T
        s = _dot_nt(q_ref[_tile(c), :], k_ref[0:n, :])
        parts = [s[:, _tile(j)] for j in range(c)]
        parts.append(jnp.where(tri, s[:, _tile(c)], NEG_INF))
        o_ref[_tile(c), :] = _softmax_pv(parts, va_ref)


def _mla_attn(q, k, v):
    return pl.pallas_call(
        _mla_attn_kernel,
        grid=(BATCH, MLA_HEADS),
        in_specs=[_head_spec(MLA_QK_PAD, 0), _head_spec(MLA_QK_PAD, 0), _head_spec(MLA_V, 0)],
        out_specs=_head_spec(MLA_V, 0),
        out_shape=jax.ShapeDtypeStruct((TOKENS, MLA_HEADS * MLA_V), jnp.bfloat16),
        scratch_shapes=_V_ONES_SCRATCH,
        compiler_params=_cparams(("parallel", "parallel"), 32),
        name="mla_attn",
    )(q, k, v)


def _moba_attn_kernel(q_ref, k_ref, v_ref, o_ref, va_ref):
    _fill_v_ones(v_ref, va_ref)
    rid = lax.broadcasted_iota(jnp.int32, (LANES, HEAD_DIM), 0)
    km = jnp.zeros((LANES, HEAD_DIM), jnp.float32)
    for j in range(MOBA_NBLK):
        kj = k_ref[_tile(j), :].astype(jnp.float32)
        mean_j = jnp.sum(kj, axis=0, keepdims=True) * (1.0 / MOBA_BLOCK)
        km = jnp.where(rid == j, mean_j, km)
    km_hi = km.astype(jnp.bfloat16)
    km_lo = (km - km_hi.astype(jnp.float32)).astype(jnp.bfloat16)
    tri = _causal_tri()

    for c in ATT_ORDER:
        n = (c + 1) * ATT_T
        q = q_ref[_tile(c), :]
        s = _dot_nt(q, k_ref[0:n, :])
        parts = [s[:, _tile(j)] for j in range(c)]
        if c > MOBA_TOPK:
            gate = _dot_nt(q, km_hi) + _dot_nt(q, km_lo)
            lane = lax.broadcasted_iota(jnp.int32, gate.shape, 1)
            ahead = jnp.zeros(gate.shape, jnp.float32)
            for jp in range(c):
                cj = _lane_col(gate, jp, lane)
                wins = (cj > gate) | ((cj == gate) & (lane > jp))
                ahead = ahead + jnp.where(wins, 1.0, 0.0)
            sel = jnp.where(ahead < MOBA_TOPK, 1.0, 0.0)
            parts = [jnp.where(_lane_col(sel, j, lane) > 0.5, parts[j], NEG_INF)
                     for j in range(c)]
        parts.append(jnp.where(tri, s[:, _tile(c)], NEG_INF))
        o_ref[_tile(c), :] = _softmax_pv(parts, va_ref)


def _moba_attn(qkv):
    return pl.pallas_call(
        _moba_attn_kernel,
        grid=(BATCH, MOBA_HEADS),
        in_specs=[_head_spec(HEAD_DIM, 0), _head_spec(HEAD_DIM, MOBA_HEADS),
                  _head_spec(HEAD_DIM, 2 * MOBA_HEADS)],
        out_specs=_head_spec(HEAD_DIM, 0),
        out_shape=jax.ShapeDtypeStruct((TOKENS, MOBA_HEADS * HEAD_DIM), jnp.bfloat16),
        scratch_shapes=_V_ONES_SCRATCH,
        compiler_params=_cparams(("parallel", "parallel"), 32),
        name="moba_attn",
    )(qkv, qkv, qkv)


def _dil_tables():
    r = np.arange(ATT_T)[:, None]
    c = np.arange(ATT_T)[None, :]
    cnts = []
    for d in range(ATT_NT):
        delta = r - c + ATT_T * d
        cnts.append(sum(((delta >= 0) & (delta <= w) & (delta % dil == 0)).astype(np.float32)
                        for w, dil in DIL_PATTERNS))
    cnt = np.stack(cnts)
    bias = np.where(cnt > 0, 0.0, NEG_INF).astype(np.float32)
    n_weighted = max(d + 1 for d in range(ATT_NT) if cnt[d].max() > 1)
    return bias, cnt[:n_weighted]


def _dil_attn_kernel(q_ref, k_ref, v_ref, bias_ref, cnt_ref, o_ref, va_ref):
    n_weighted = cnt_ref.shape[0]
    for hh in range(DIL_HPS):
        hs = slice(hh * HEAD_DIM, (hh + 1) * HEAD_DIM)
        va = va_ref.at[hh]
        va[:, :HEAD_DIM] = v_ref[:, hs]
        va[:, HEAD_DIM:] = jnp.ones((SEQ, HEAD_DIM), va_ref.dtype)
    for c in ATT_ORDER:
        n = (c + 1) * ATT_T
        weights = [cnt_ref[c - j] if c - j < n_weighted else None for j in range(c + 1)]
        for hh in range(DIL_HPS):
            hs = slice(hh * HEAD_DIM, (hh + 1) * HEAD_DIM)
            s = _dot_nt(q_ref[_tile(c), hs], k_ref[0:n, hs])
            tiles = [s[:, _tile(j)] + bias_ref[c - j] for j in range(c + 1)]
            o_ref[_tile(c), hs] = _softmax_pv(tiles, va_ref.at[hh], weights)


def _dil_attn(qkv):
    bias, cnt = _dil_tables()
    return pl.pallas_call(
        _dil_attn_kernel,
        grid=(BATCH, DIL_HEADS // DIL_HPS),
        in_specs=[_head_spec(DIL_HPS * HEAD_DIM, 0),
                  _head_spec(DIL_HPS * HEAD_DIM, DIL_HEADS // DIL_HPS),
                  _head_spec(DIL_HPS * HEAD_DIM, 2 * DIL_HEADS // DIL_HPS),
                  pl.BlockSpec(bias.shape, lambda b, h: (0, 0, 0)),
                  pl.BlockSpec(cnt.shape, lambda b, h: (0, 0, 0))],
        out_specs=_head_spec(DIL_HPS * HEAD_DIM, 0),
        out_shape=jax.ShapeDtypeStruct((TOKENS, DIL_HEADS * HEAD_DIM), jnp.bfloat16),
        scratch_shapes=[pltpu.VMEM((DIL_HPS, SEQ, 2 * HEAD_DIM), jnp.bfloat16)],
        compiler_params=_cparams(("parallel", "parallel"), 32),
        name="dil_attn",
    )(qkv, qkv, qkv, jnp.asarray(bias), jnp.asarray(cnt))


def _out_proj_kernel(x_ref, mla_ref, moba_ref, dil_ref, w_ref, o_ref, wb_ref):
    @pl.when(pl.program_id(0) == 0)
    def _():
        wb_ref[...] = w_ref[...].astype(jnp.bfloat16)

    mix = jnp.concatenate([mla_ref[...], moba_ref[...], dil_ref[...]], axis=1)
    o_ref[...] = x_ref[...] + jnp.dot(mix, wb_ref[...], preferred_element_type=jnp.float32)


def _out_proj(x, o_mla, o_moba, o_dil, w_out, layer):
    tm = OUT_TM

    def rows(width):
        return pl.BlockSpec((tm, width), lambda i: (i, 0))

    return pl.pallas_call(
        _out_proj_kernel,
        grid=(TOKENS // tm,),
        in_specs=[rows(D_MODEL), rows(MLA_HEADS * MLA_V), rows(MOBA_HEADS * HEAD_DIM),
                  rows(DIL_HEADS * HEAD_DIM),
                  pl.BlockSpec((None, MIX_WIDTH, D_MODEL), lambda i: (layer, 0, 0),
                               pipeline_mode=pl.Buffered(1))],
        out_specs=rows(D_MODEL),
        out_shape=jax.ShapeDtypeStruct((TOKENS, D_MODEL), jnp.float32),
        scratch_shapes=[pltpu.VMEM((MIX_WIDTH, D_MODEL), jnp.bfloat16)],
        compiler_params=_cparams(("arbitrary",), 56),
        name="out_proj",
    )(x, o_mla, o_moba, o_dil, w_out)


def _rope_tables():
    pos = jnp.arange(SEQ, dtype=jnp.float32)[:, None]
    inv_h = ROPE_THETA ** (-jnp.arange(0, HEAD_DIM, 2, dtype=jnp.float32) / HEAD_DIM)
    ang = pos * inv_h[None, :]
    cos_h = jnp.concatenate([jnp.cos(ang), jnp.cos(ang)], axis=1)
    sin_h = jnp.concatenate([-jnp.sin(ang), jnp.sin(ang)], axis=1)
    scale = HEAD_DIM ** -0.5 * LOG2E
    cos_qk = jnp.stack([cos_h * scale, cos_h])
    sin_qk = jnp.stack([sin_h * scale, sin_h])
    inv_r = ROPE_THETA ** (-jnp.arange(0, MLA_ROPE, 2, dtype=jnp.float32) / MLA_ROPE)
    ang_r = pos * inv_r[None, :]
    z = jnp.zeros_like(ang_r)
    cos_r = jnp.concatenate([jnp.cos(ang_r), z, jnp.cos(ang_r), z], axis=1)
    sin_r = jnp.concatenate([-jnp.sin(ang_r), z, jnp.sin(ang_r), z], axis=1)
    return cos_qk, sin_qk, cos_r, sin_r


def _pad_rope_cols(w):
    half = MLA_ROPE // 2
    z = jnp.zeros(w.shape[:-1] + (half,), w.dtype)
    return jnp.concatenate([w[..., :half], z, w[..., half:], z], axis=-1)


def _prep_mla_weights(w_uq, w_uk, w_uv):
    bf = jnp.bfloat16
    uq = w_uq.reshape(MLA_Q_RANK, MLA_HEADS, MLA_NOPE + MLA_ROPE)
    uq = jnp.concatenate([uq[..., :MLA_NOPE], _pad_rope_cols(uq[..., MLA_NOPE:])], axis=-1)
    uq = uq.reshape(MLA_Q_RANK, MLA_HEADS * MLA_QK_PAD).astype(bf)
    return uq, w_uk.astype(bf), w_uv.astype(bf)


def kernel(x, ln_ffn1, w_ffn1_gate, w_ffn1_up, w_ffn1_down, ln_mix, w_in, g_mla_q, g_mla_kv,
           w_mla_uq, w_mla_uk, w_mla_uv, w_out, ln_ffn2, w_ffn2_gate, w_ffn2_up, w_ffn2_down,
           ln_final):
    cos_qk, sin_qk, cos_r, sin_r = _rope_tables()
    w_proj = jnp.swapaxes(w_in, 1, 2).astype(jnp.bfloat16)
    xt = x.reshape(TOKENS, D_MODEL)
    for l in range(DEPTH):
        uq, uk, uv = _prep_mla_weights(w_mla_uq[l], w_mla_uk[l], w_mla_uv[l])
        xt = _ffn(xt, ln_ffn1[l][None], w_ffn1_gate, w_ffn1_up, w_ffn1_down, l)
        q_mla, k_mla, v_mla, qkv_moba, qkv_dil = _proj(
            xt, ln_mix[l][None], w_proj, l, cos_qk, sin_qk,
            g_mla_q[l][None], g_mla_kv[l][None], uq, uk, uv, cos_r, sin_r)
        o_mla = _mla_attn(q_mla, k_mla, v_mla)
        o_moba = _moba_attn(qkv_moba)
        o_dil = _dil_attn(qkv_dil)
        xt = _out_proj(xt, o_mla, o_moba, o_dil, w_out, l)
        xt = _ffn(xt, ln_ffn2[l][None], w_ffn2_gate, w_ffn2_up, w_ffn2_down, l,
                  ln_final=ln_final[None] if l == DEPTH - 1 else None)
    return xt.reshape(BATCH, SEQ, D_MODEL)
```

```python
import functools

import numpy as np

import jax
import jax.numpy as jnp
from jax import lax
from jax.experimental import pallas as pl
from jax.experimental.pallas import tpu as pltpu

D_MODEL = 2048
BATCH = 4
SEQ = 2048
DEPTH = 2
TOKENS = BATCH * SEQ

HEAD_DIM = 128
MLA_HEADS = 4
MLA_Q_RANK = 512
MLA_KV_RANK = 256
MLA_NOPE = 128
MLA_ROPE = 64
MLA_V = 128
MLA_QK_PAD = 256
MOBA_HEADS = 4
MOBA_BLOCK = 256
MOBA_TOPK = 3
MOBA_NBLK = SEQ // MOBA_BLOCK
DIL_HEADS = 8
DIL_PATTERNS = ((128, 1), (512, 4), (2048, 16))
D_FF = 5632
ROPE_THETA = 10000.0
NORM_EPS = 1e-6
NEG_INF = -1e30
LOG2E = 1.4426950408889634

MLA_IN = MLA_Q_RANK + MLA_KV_RANK + MLA_ROPE
MOBA_IN = 3 * MOBA_HEADS * HEAD_DIM
DIL_IN = 3 * DIL_HEADS * HEAD_DIM
MIX_WIDTH = MLA_HEADS * MLA_V + MOBA_HEADS * HEAD_DIM + DIL_HEADS * HEAD_DIM

LANES = 128
MIB = 1024 * 1024

FFN_TM = 1024
FFN_TF = 512
FFN_TF_HEAD = 256
PROJ_TM = 512
PROJ_TN = 512
ATT_T = 256
ATT_NT = SEQ // ATT_T
ATT_ORDER = tuple(reversed(range(ATT_NT)))
ATT_VMEM_MIB = 56
OUT_TM = 512
DIL_HPS = 2

assert ATT_T == MOBA_BLOCK


def _cparams(semantics, vmem_mib):
    return pltpu.CompilerParams(dimension_semantics=semantics,
                                vmem_limit_bytes=vmem_mib * MIB)


def _rms(x, g):
    ms = jnp.mean(x * x, axis=-1, keepdims=True)
    return x * lax.rsqrt(ms + NORM_EPS) * g


def _lane_col(a, j, lane):
    return jnp.sum(jnp.where(lane == j, a, 0.0), axis=1, keepdims=True)


def _dot_nt(a, b):
    return lax.dot_general(a, b, (((1,), (1,)), ((), ())),
                           preferred_element_type=jnp.float32)


def _cat(parts):
    return parts[0] if len(parts) == 1 else jnp.concatenate(parts, axis=1)


def _tile(c):
    return slice(c * ATT_T, (c + 1) * ATT_T)


def _causal_tri():
    r = lax.broadcasted_iota(jnp.int32, (ATT_T, ATT_T), 0)
    c = lax.broadcasted_iota(jnp.int32, (ATT_T, ATT_T), 1)
    return c <= r


def _softmax_pv(tiles, v_ref, weights=None):
    mx = tiles[0]
    for t in tiles[1:]:
        mx = jnp.maximum(mx, t)
    m = jnp.max(mx, axis=1, keepdims=True)
    res = None
    for j, t in enumerate(tiles):
        e = jnp.exp2(t - m)
        if weights is not None and weights[j] is not None:
            e = e * weights[j]
        pv = jnp.dot(e.astype(jnp.bfloat16), v_ref[_tile(j), :],
                     preferred_element_type=jnp.float32)
        res = pv if res is None else res + pv
    half = res.shape[1] // 2
    return (res[:, :half] / res[:, half:]).astype(jnp.bfloat16)


def _head_spec(width, col_offset):
    return pl.BlockSpec((SEQ, width), lambda b, h: (b, col_offset + h))


def _ffn_begin(x_ref, ln_ref, h_ref, o_ref):
    @pl.when(pl.program_id(1) == 0)
    def _():
        x = x_ref[...]
        h_ref[...] = _rms(x, ln_ref[...]).astype(jnp.bfloat16)
        o_ref[...] = x


def _ffn_tile(h, wgu, wd):
    tf = wd.shape[0]
    gu = jnp.dot(h, wgu, preferred_element_type=jnp.float32)
    g = gu[:, :tf]
    u = gu[:, tf:]
    a = (g * (1.0 / (1.0 + jnp.exp(-g))) * u * 0.5).astype(jnp.bfloat16)
    return jnp.dot(a, wd, preferred_element_type=jnp.float32)


def _ffn_end(o_ref, lnf_ref):
    if lnf_ref is not None:
        @pl.when(pl.program_id(1) == pl.num_programs(1) - 1)
        def _():
            o_ref[...] = _rms(o_ref[...], lnf_ref[...])


def _ffn_head_kernel(x_ref, ln_ref, wg_ref, wu_ref, wd_ref, *rest, final_norm):
    rest = list(rest)
    lnf_ref = rest.pop(0) if final_norm else None
    o_ref, wgu16_ref, wd16_ref, h_ref = rest
    _ffn_begin(x_ref, ln_ref, h_ref, o_ref)
    wgu = jnp.concatenate([wg_ref[...].astype(jnp.bfloat16),
                           wu_ref[...].astype(jnp.bfloat16)], axis=1)
    wd = wd_ref[...].astype(jnp.bfloat16)
    wgu16_ref[...] = wgu
    wd16_ref[...] = wd
    o_ref[...] += _ffn_tile(h_ref[...], wgu, wd)
    _ffn_end(o_ref, lnf_ref)


def _ffn_rest_kernel(x_ref, ln_ref, wgu_ref, wd_ref, *rest, final_norm):
    rest = list(rest)
    lnf_ref = rest.pop(0) if final_norm else None
    _, o_ref, h_ref = rest
    _ffn_begin(x_ref, ln_ref, h_ref, o_ref)
    h = h_ref[...]
    tf = wgu_ref.shape[2] // 2
    y = None
    for t in range(wgu_ref.shape[0]):
        yt = _ffn_tile(h, wgu_ref[t], wd_ref[t * tf:(t + 1) * tf, :])
        y = yt if y is None else y + yt
    o_ref[...] += y
    _ffn_end(o_ref, lnf_ref)


def _ffn(x, ln, wg, wu, wd, layer, ln_final=None):
    final_norm = ln_final is not None
    tm = FFN_TM
    bf = jnp.bfloat16
    out_shape = jax.ShapeDtypeStruct((TOKENS, D_MODEL), jnp.float32)
    vec_spec = pl.BlockSpec((1, D_MODEL), lambda i, j: (0, 0))
    tail_specs = [vec_spec] if final_norm else []
    tail_args = [ln_final] if final_norm else []
    suffix = "_final" if final_norm else ""

    th = FFN_TF_HEAD
    n_head_tiles = D_FF // th
    group = FFN_TF // th
    scratch = [pltpu.VMEM((tm, D_MODEL), bf)]
    cparams = _cparams(("parallel", "arbitrary"), 60)

    y, wgu16, wd16 = pl.pallas_call(
        functools.partial(_ffn_head_kernel, final_norm=final_norm),
        grid=(1, n_head_tiles),
        in_specs=[pl.BlockSpec((tm, D_MODEL), lambda i, j: (0, 0), pipeline_mode=pl.Buffered(1)),
                  vec_spec,
                  pl.BlockSpec((None, D_MODEL, th), lambda i, j: (layer, 0, j)),
                  pl.BlockSpec((None, D_MODEL, th), lambda i, j: (layer, 0, j)),
                  pl.BlockSpec((None, th, D_MODEL), lambda i, j: (layer, j, 0))] + tail_specs,
        out_specs=[pl.BlockSpec((tm, D_MODEL), lambda i, j: (0, 0)),
                   pl.BlockSpec((None, D_MODEL, 2 * th), lambda i, j: (j, 0, 0)),
                   pl.BlockSpec((th, D_MODEL), lambda i, j: (j, 0))],
        out_shape=[out_shape, jax.ShapeDtypeStruct((n_head_tiles, D_MODEL, 2 * th), bf),
                   jax.ShapeDtypeStruct((D_FF, D_MODEL), bf)],
        scratch_shapes=scratch,
        compiler_params=cparams,
        name="ffn_head" + suffix,
    )(x, ln, wg, wu, wd, *tail_args)

    alias_index = 4 + len(tail_args)
    return pl.pallas_call(
        functools.partial(_ffn_rest_kernel, final_norm=final_norm),
        grid=(TOKENS // tm - 1, n_head_tiles // group),
        in_specs=[pl.BlockSpec((tm, D_MODEL), lambda i, j: (i + 1, 0)),
                  vec_spec,
                  pl.BlockSpec((group, D_MODEL, 2 * th), lambda i, j: (j, 0, 0)),
                  pl.BlockSpec((group * th, D_MODEL), lambda i, j: (j, 0))] + tail_specs
        + [pl.BlockSpec(memory_space=pl.ANY)],
        out_specs=pl.BlockSpec((tm, D_MODEL), lambda i, j: (i + 1, 0)),
        out_shape=out_shape,
        scratch_shapes=scratch,
        input_output_aliases={alias_index: 0},
        compiler_params=cparams,
        name="ffn_rest" + suffix,
    )(x, ln, wgu16, wd16, *tail_args, y)


PROJ_WIDTH = MLA_IN + MOBA_IN + DIL_IN


def _mla_prep(lat, kr, gq_ref, gkv_ref, wuq_ref, wuk_ref, wuv_ref, cos_ref, sin_ref,
              q_ref, k_ref, v_ref):
    scale = (MLA_NOPE + MLA_ROPE) ** -0.5 * LOG2E
    c = cos_ref[...]
    s = sin_ref[...]

    def rope(t):
        return t * c + pltpu.roll(t, LANES // 2, 1) * s

    nq = MLA_Q_RANK // LANES
    cq = _rms(jnp.concatenate(lat[:nq], axis=1), gq_ref[...]).astype(jnp.bfloat16)
    q = jnp.dot(cq, wuq_ref[...], preferred_element_type=jnp.float32)
    ckv = _rms(jnp.concatenate(lat[nq:], axis=1), gkv_ref[...]).astype(jnp.bfloat16)
    kn = jnp.dot(ckv, wuk_ref[...], preferred_element_type=jnp.float32)
    v_ref[...] = jnp.dot(ckv, wuv_ref[...],
                         preferred_element_type=jnp.float32).astype(jnp.bfloat16)
    kr = rope(kr).astype(jnp.bfloat16)
    for h in range(MLA_HEADS):
        b0 = h * MLA_QK_PAD
        q_ref[:, b0:b0 + LANES] = (q[:, b0:b0 + LANES] * scale).astype(jnp.bfloat16)
        q_ref[:, b0 + LANES:b0 + 2 * LANES] = (
            rope(q[:, b0 + LANES:b0 + 2 * LANES]) * scale).astype(jnp.bfloat16)
        k_ref[:, b0:b0 + LANES] = kn[:, h * LANES:(h + 1) * LANES].astype(jnp.bfloat16)
        k_ref[:, b0 + LANES:b0 + 2 * LANES] = kr


def _proj_kernel(x_ref, ln_ref, wt_ref, cos_ref, sin_ref, *rest):
    mla_args, (moba_ref, dil_ref) = rest[:-2], rest[-2:]
    h = _rms(x_ref[...], ln_ref[...]).astype(jnp.bfloat16)

    def proj(r0, rows):
        return _dot_nt(h, wt_ref[r0:r0 + rows, :])

    n_lat = MLA_Q_RANK + MLA_KV_RANK
    lat = []
    for r0 in range(0, n_lat, PROJ_TN):
        y = proj(r0, min(PROJ_TN, n_lat - r0))
        lat += [y[:, g0:g0 + LANES] for g0 in range(0, y.shape[1], LANES)]
    yk = proj(n_lat, LANES)
    lane = lax.broadcasted_iota(jnp.int32, yk.shape, 1)
    half = MLA_ROPE // 2
    kr = (jnp.where(lane < half, yk, 0.0)
          + jnp.where((lane >= LANES // 2) & (lane < LANES // 2 + half),
                      pltpu.roll(yk, half, 1), 0.0))
    _mla_prep(lat, kr, *mla_args)

    row = MLA_IN
    for o_ref, width in ((moba_ref, MOBA_IN), (dil_ref, DIL_IN)):
        for c0 in range(0, width, PROJ_TN):
            y = proj(row + c0, PROJ_TN)
            section = c0 // (width // 3)
            if section == 2:
                o_ref[:, c0:c0 + PROJ_TN] = y.astype(o_ref.dtype)
                continue
            c = cos_ref[section]
            s = sin_ref[section]
            for g0 in range(0, PROJ_TN, LANES):
                yg = y[:, g0:g0 + LANES]
                o_ref[:, c0 + g0:c0 + g0 + LANES] = (
                    yg * c + pltpu.roll(yg, LANES // 2, 1) * s).astype(o_ref.dtype)
        row += width


def _proj(x, ln, w, layer, cos_tab, sin_tab, gq, gkv, wuq, wuk, wuv, cos_r, sin_r):
    tm = PROJ_TM
    pos_blocks = SEQ // tm
    qk_w = MLA_HEADS * MLA_QK_PAD
    v_w = MLA_HEADS * MLA_V
    bf = jnp.bfloat16

    def rows(width):
        return pl.BlockSpec((tm, width), lambda i: (i, 0))

    def full(shape):
        return pl.BlockSpec(shape, lambda i: (0, 0))

    def tab_spec():
        return pl.BlockSpec((2, tm, LANES), lambda i: (0, i % pos_blocks, 0))

    def rtab_spec():
        return pl.BlockSpec((tm, LANES), lambda i: (i % pos_blocks, 0))

    return pl.pallas_call(
        _proj_kernel,
        grid=(TOKENS // tm,),
        in_specs=[
            rows(D_MODEL),
            full((1, D_MODEL)),
            pl.BlockSpec((None, PROJ_WIDTH, D_MODEL), lambda i: (layer, 0, 0),
                         pipeline_mode=pl.Buffered(1)),
            tab_spec(), tab_spec(),
            full((1, MLA_Q_RANK)), full((1, MLA_KV_RANK)),
            full((MLA_Q_RANK, qk_w)), full((MLA_KV_RANK, v_w)), full((MLA_KV_RANK, v_w)),
            rtab_spec(), rtab_spec(),
        ],
        out_specs=[rows(qk_w), rows(qk_w), rows(v_w), rows(MOBA_IN), rows(DIL_IN)],
        out_shape=[
            jax.ShapeDtypeStruct((TOKENS, qk_w), bf),
            jax.ShapeDtypeStruct((TOKENS, qk_w), bf),
            jax.ShapeDtypeStruct((TOKENS, v_w), bf),
            jax.ShapeDtypeStruct((TOKENS, MOBA_IN), bf),
            jax.ShapeDtypeStruct((TOKENS, DIL_IN), bf),
        ],
        compiler_params=_cparams(("parallel",), 56),
        name="mix_proj",
    )(x, ln, w, cos_tab, sin_tab, gq, gkv, wuq, wuk, wuv, cos_r, sin_r)


def _fill_v_ones(v_ref, va_ref):
    width = v_ref.shape[1]
    va_ref[:, :width] = v_ref[...]
    va_ref[:, width:] = jnp.ones((v_ref.shape[0], va_ref.shape[1] - width), va_ref.dtype)


_V_ONES_SCRATCH = [pltpu.VMEM((SEQ, 2 * HEAD_DIM), jnp.bfloat16)]


def _mla_attn_kernel(q_ref, k_ref, v_ref, o_ref, va_ref):
    _fill_v_ones(v_ref, va_ref)
    tri = _causal_tri()
    for c in ATT_ORDER:
        n = (c + 1) * ATT_T
        s = _dot_nt(q_ref[_tile(c), :], k_ref[0:n, :])
        parts = [s[:, _tile(j)] for j in range(c)]
        parts.append(jnp.where(tri, s[:, _tile(c)], NEG_INF))
        o_ref[_tile(c), :] = _softmax_pv(parts, va_ref)


def _mla_attn(q, k, v):
    return pl.pallas_call(
        _mla_attn_kernel,
        grid=(BATCH, MLA_HEADS),
        in_specs=[_head_spec(MLA_QK_PAD, 0), _head_spec(MLA_QK_PAD, 0), _head_spec(MLA_V, 0)],
        out_specs=_head_spec(MLA_V, 0),
        out_shape=jax.ShapeDtypeStruct((TOKENS, MLA_HEADS * MLA_V), jnp.bfloat16),
        scratch_shapes=_V_ONES_SCRATCH,
        compiler_params=_cparams(("parallel", "parallel"), ATT_VMEM_MIB),
        name="mla_attn",
    )(q, k, v)


def _moba_attn_kernel(q_ref, k_ref, v_ref, o_ref, va_ref):
    _fill_v_ones(v_ref, va_ref)
    rid = lax.broadcasted_iota(jnp.int32, (LANES, HEAD_DIM), 0)
    km = jnp.zeros((LANES, HEAD_DIM), jnp.float32)
    for j in range(MOBA_NBLK):
        kj = k_ref[_tile(j), :].astype(jnp.float32)
        mean_j = jnp.sum(kj, axis=0, keepdims=True) * (1.0 / MOBA_BLOCK)
        km = jnp.where(rid == j, mean_j, km)
    km_hi = km.astype(jnp.bfloat16)
    km_lo = (km - km_hi.astype(jnp.float32)).astype(jnp.bfloat16)
    tri = _causal_tri()

    for c in ATT_ORDER:
        n = (c + 1) * ATT_T
        q = q_ref[_tile(c), :]
        s = _dot_nt(q, k_ref[0:n, :])
        parts = [s[:, _tile(j)] for j in range(c)]
        if c > MOBA_TOPK:
            gate = _dot_nt(q, km_hi) + _dot_nt(q, km_lo)
            lane = lax.broadcasted_iota(jnp.int32, gate.shape, 1)
            ahead = jnp.zeros(gate.shape, jnp.float32)
            for jp in range(c):
                cj = _lane_col(gate, jp, lane)
                wins = (cj > gate) | ((cj == gate) & (lane > jp))
                ahead = ahead + jnp.where(wins, 1.0, 0.0)
            sel = jnp.where(ahead < MOBA_TOPK, 1.0, 0.0)
            parts = [jnp.where(_lane_col(sel, j, lane) > 0.5, parts[j], NEG_INF)
                     for j in range(c)]
        parts.append(jnp.where(tri, s[:, _tile(c)], NEG_INF))
        o_ref[_tile(c), :] = _softmax_pv(parts, va_ref)


def _moba_attn(qkv):
    return pl.pallas_call(
        _moba_attn_kernel,
        grid=(BATCH, MOBA_HEADS),
        in_specs=[_head_spec(HEAD_DIM, 0), _head_spec(HEAD_DIM, MOBA_HEADS),
                  _head_spec(HEAD_DIM, 2 * MOBA_HEADS)],
        out_specs=_head_spec(HEAD_DIM, 0),
        out_shape=jax.ShapeDtypeStruct((TOKENS, MOBA_HEADS * HEAD_DIM), jnp.bfloat16),
        scratch_shapes=_V_ONES_SCRATCH,
        compiler_params=_cparams(("parallel", "parallel"), ATT_VMEM_MIB),
        name="moba_attn",
    )(qkv, qkv, qkv)


def _dil_tables():
    r = np.arange(ATT_T)[:, None]
    c = np.arange(ATT_T)[None, :]
    cnts = []
    for d in range(ATT_NT):
        delta = r - c + ATT_T * d
        cnts.append(sum(((delta >= 0) & (delta <= w) & (delta % dil == 0)).astype(np.float32)
                        for w, dil in DIL_PATTERNS))
    cnt = np.stack(cnts)
    bias = np.where(cnt > 0, 0.0, NEG_INF).astype(np.float32)
    n_weighted = max(d + 1 for d in range(ATT_NT) if cnt[d].max() > 1)
    return bias, cnt[:n_weighted]


def _dil_attn_kernel(q_ref, k_ref, v_ref, bias_ref, cnt_ref, o_ref, va_ref):
    n_weighted = cnt_ref.shape[0]
    for hh in range(DIL_HPS):
        hs = slice(hh * HEAD_DIM, (hh + 1) * HEAD_DIM)
        va = va_ref.at[hh]
        va[:, :HEAD_DIM] = v_ref[:, hs]
        va[:, HEAD_DIM:] = jnp.ones((SEQ, HEAD_DIM), va_ref.dtype)
    for c in ATT_ORDER:
        n = (c + 1) * ATT_T
        weights = [cnt_ref[c - j] if c - j < n_weighted else None for j in range(c + 1)]
        for hh in range(DIL_HPS):
            hs = slice(hh * HEAD_DIM, (hh + 1) * HEAD_DIM)
            s = _dot_nt(q_ref[_tile(c), hs], k_ref[0:n, hs])
            tiles = [s[:, _tile(j)] + bias_ref[c - j] for j in range(c + 1)]
            o_ref[_tile(c), hs] = _softmax_pv(tiles, va_ref.at[hh], weights)


def _dil_attn(qkv):
    bias, cnt = _dil_tables()
    return pl.pallas_call(
        _dil_attn_kernel,
        grid=(BATCH, DIL_HEADS // DIL_HPS),
        in_specs=[_head_spec(DIL_HPS * HEAD_DIM, 0),
                  _head_spec(DIL_HPS * HEAD_DIM, DIL_HEADS // DIL_HPS),
                  _head_spec(DIL_HPS * HEAD_DIM, 2 * DIL_HEADS // DIL_HPS),
                  pl.BlockSpec(bias.shape, lambda b, h: (0, 0, 0)),
                  pl.BlockSpec(cnt.shape, lambda b, h: (0, 0, 0))],
        out_specs=_head_spec(DIL_HPS * HEAD_DIM, 0),
        out_shape=jax.ShapeDtypeStruct((TOKENS, DIL_HEADS * HEAD_DIM), jnp.bfloat16),
        scratch_shapes=[pltpu.VMEM((DIL_HPS, SEQ, 2 * HEAD_DIM), jnp.bfloat16)],
        compiler_params=_cparams(("parallel", "parallel"), ATT_VMEM_MIB),
        name="dil_attn",
    )(qkv, qkv, qkv, jnp.asarray(bias), jnp.asarray(cnt))


def _out_proj_kernel(x_ref, mla_ref, moba_ref, dil_ref, w_ref, o_ref, wb_ref):
    @pl.when(pl.program_id(0) == 0)
    def _():
        wb_ref[...] = w_ref[...].astype(jnp.bfloat16)

    mix = jnp.concatenate([mla_ref[...], moba_ref[...], dil_ref[...]], axis=1)
    o_ref[...] = x_ref[...] + jnp.dot(mix, wb_ref[...], preferred_element_type=jnp.float32)


def _out_proj(x, o_mla, o_moba, o_dil, w_out, layer):
    tm = OUT_TM

    def rows(width):
        return pl.BlockSpec((tm, width), lambda i: (i, 0))

    return pl.pallas_call(
        _out_proj_kernel,
        grid=(TOKENS // tm,),
        in_specs=[rows(D_MODEL), rows(MLA_HEADS * MLA_V), rows(MOBA_HEADS * HEAD_DIM),
                  rows(DIL_HEADS * HEAD_DIM),
                  pl.BlockSpec((None, MIX_WIDTH, D_MODEL), lambda i: (layer, 0, 0),
                               pipeline_mode=pl.Buffered(1))],
        out_specs=rows(D_MODEL),
        out_shape=jax.ShapeDtypeStruct((TOKENS, D_MODEL), jnp.float32),
        scratch_shapes=[pltpu.VMEM((MIX_WIDTH, D_MODEL), jnp.bfloat16)],
        compiler_params=_cparams(("arbitrary",), 56),
        name="out_proj",
    )(x, o_mla, o_moba, o_dil, w_out)


def _rope_tables():
    pos = jnp.arange(SEQ, dtype=jnp.float32)[:, None]
    inv_h = ROPE_THETA ** (-jnp.arange(0, HEAD_DIM, 2, dtype=jnp.float32) / HEAD_DIM)
    ang = pos * inv_h[None, :]
    cos_h = jnp.concatenate([jnp.cos(ang), jnp.cos(ang)], axis=1)
    sin_h = jnp.concatenate([-jnp.sin(ang), jnp.sin(ang)], axis=1)
    scale = HEAD_DIM ** -0.5 * LOG2E
    cos_qk = jnp.stack([cos_h * scale, cos_h])
    sin_qk = jnp.stack([sin_h * scale, sin_h])
    inv_r = ROPE_THETA ** (-jnp.arange(0, MLA_ROPE, 2, dtype=jnp.float32) / MLA_ROPE)
    ang_r = pos * inv_r[None, :]
    z = jnp.zeros_like(ang_r)
    cos_r = jnp.concatenate([jnp.cos(ang_r), z, jnp.cos(ang_r), z], axis=1)
    sin_r = jnp.concatenate([-jnp.sin(ang_r), z, jnp.sin(ang_r), z], axis=1)
    return cos_qk, sin_qk, cos_r, sin_r


def _pad_rope_cols(w):
    half = MLA_ROPE // 2
    z = jnp.zeros(w.shape[:-1] + (half,), w.dtype)
    return jnp.concatenate([w[..., :half], z, w[..., half:], z], axis=-1)


def _prep_mla_weights(w_uq, w_uk, w_uv):
    bf = jnp.bfloat16
    uq = w_uq.reshape(MLA_Q_RANK, MLA_HEADS, MLA_NOPE + MLA_ROPE)
    uq = jnp.concatenate([uq[..., :MLA_NOPE], _pad_rope_cols(uq[..., MLA_NOPE:])], axis=-1)
    uq = uq.reshape(MLA_Q_RANK, MLA_HEADS * MLA_QK_PAD).astype(bf)
    return uq, w_uk.astype(bf), w_uv.astype(bf)


def kernel(x, ln_ffn1, w_ffn1_gate, w_ffn1_up, w_ffn1_down, ln_mix, w_in, g_mla_q, g_mla_kv,
           w_mla_uq, w_mla_uk, w_mla_uv, w_out, ln_ffn2, w_ffn2_gate, w_ffn2_up, w_ffn2_down,
           ln_final):
    cos_qk, sin_qk, cos_r, sin_r = _rope_tables()
    w_proj = jnp.swapaxes(w_in, 1, 2).astype(jnp.bfloat16)
    xt = x.reshape(TOKENS, D_MODEL)
    for l in range(DEPTH):
        uq, uk, uv = _prep_mla_weights(w_mla_uq[l], w_mla_uk[l], w_mla_uv[l])
        xt = _ffn(xt, ln_ffn1[l][None], w_ffn1_gate, w_ffn1_up, w_ffn1_down, l)
        q_mla, k_mla, v_mla, qkv_moba, qkv_dil = _proj(
            xt, ln_mix[l][None], w_proj, l, cos_qk, sin_qk,
            g_mla_q[l][None], g_mla_kv[l][None], uq, uk, uv, cos_r, sin_r)
        o_mla = _mla_attn(q_mla, k_mla, v_mla)
        o_moba = _moba_attn(qkv_moba)
        o_dil = _dil_attn(qkv_dil)
        xt = _out_proj(xt, o_mla, o_moba, o_dil, w_out, l)
        xt = _ffn(xt, ln_ffn2[l][None], w_ffn2_gate, w_ffn2_up, w_ffn2_down, l,
                  ln_final=ln_final[None] if l == DEPTH - 1 else None)
    return xt.reshape(BATCH, SEQ, D_MODEL)
```

```python
import functools

import numpy as np

import jax
import jax.numpy as jnp
from jax import lax
from jax.experimental import pallas as pl
from jax.experimental.pallas import tpu as pltpu

D_MODEL = 2048
BATCH = 4
SEQ = 2048
DEPTH = 2
TOKENS = BATCH * SEQ

HEAD_DIM = 128
MLA_HEADS = 4
MLA_Q_RANK = 512
MLA_KV_RANK = 256
MLA_NOPE = 128
MLA_ROPE = 64
MLA_V = 128
MLA_QK_PAD = 256
MOBA_HEADS = 4
MOBA_BLOCK = 256
MOBA_TOPK = 3
MOBA_NBLK = SEQ // MOBA_BLOCK
DIL_HEADS = 8
DIL_PATTERNS = ((128, 1), (512, 4), (2048, 16))
D_FF = 5632
ROPE_THETA = 10000.0
NORM_EPS = 1e-6
NEG_INF = -1e30
LOG2E = 1.4426950408889634

MLA_IN = MLA_Q_RANK + MLA_KV_RANK + MLA_ROPE
MOBA_IN = 3 * MOBA_HEADS * HEAD_DIM
DIL_IN = 3 * DIL_HEADS * HEAD_DIM
MIX_WIDTH = MLA_HEADS * MLA_V + MOBA_HEADS * HEAD_DIM + DIL_HEADS * HEAD_DIM

LANES = 128
MIB = 1024 * 1024

FFN_TM = 1024
FFN_TF = 512
FFN_TF_HEAD = 256
PROJ_TM = 512
PROJ_TN = 512
ATT_T = 256
ATT_NT = SEQ // ATT_T
MLA_T = 512
DIL_T = 256
ATT_ORDER = tuple(reversed(range(ATT_NT)))
ATT_VMEM_MIB = 56
OUT_TM = 512
DIL_HPS = 2

assert ATT_T == MOBA_BLOCK


def _cparams(semantics, vmem_mib):
    return pltpu.CompilerParams(dimension_semantics=semantics,
                                vmem_limit_bytes=vmem_mib * MIB)


def _rms(x, g):
    ms = jnp.mean(x * x, axis=-1, keepdims=True)
    return x * lax.rsqrt(ms + NORM_EPS) * g


def _lane_col(a, j, lane):
    return jnp.sum(jnp.where(lane == j, a, 0.0), axis=1, keepdims=True)


def _dot_nt(a, b):
    return lax.dot_general(a, b, (((1,), (1,)), ((), ())),
                           preferred_element_type=jnp.float32)


def _cat(parts):
    return parts[0] if len(parts) == 1 else jnp.concatenate(parts, axis=1)


def _tile(c, t=ATT_T):
    return slice(c * t, (c + 1) * t)


def _causal_tri(t=ATT_T):
    r = lax.broadcasted_iota(jnp.int32, (t, t), 0)
    c = lax.broadcasted_iota(jnp.int32, (t, t), 1)
    return c <= r


def _softmax_pv(tiles, v_ref, weights=None):
    mx = tiles[0]
    for t in tiles[1:]:
        mx = jnp.maximum(mx, t)
    m = jnp.max(mx, axis=1, keepdims=True)
    res = None
    for j, t in enumerate(tiles):
        e = jnp.exp2(t - m)
        if weights is not None and weights[j] is not None:
            e = e * weights[j]
        pv = jnp.dot(e.astype(jnp.bfloat16), v_ref[_tile(j, t.shape[1]), :],
                     preferred_element_type=jnp.float32)
        res = pv if res is None else res + pv
    half = res.shape[1] // 2
    return (res[:, :half] / res[:, half:]).astype(jnp.bfloat16)


def _head_spec(width, col_offset):
    return pl.BlockSpec((SEQ, width), lambda b, h: (b, col_offset + h))


def _ffn_begin(x_ref, ln_ref, h_ref, o_ref):
    @pl.when(pl.program_id(1) == 0)
    def _():
        x = x_ref[...]
        h_ref[...] = _rms(x, ln_ref[...]).astype(jnp.bfloat16)
        o_ref[...] = x


def _ffn_tile(h, wgu, wd):
    tf = wd.shape[0]
    gu = jnp.dot(h, wgu, preferred_element_type=jnp.float32)
    g = gu[:, :tf]
    u = gu[:, tf:]
    a = (g * (1.0 / (1.0 + jnp.exp(-g))) * u * 0.5).astype(jnp.bfloat16)
    return jnp.dot(a, wd, preferred_element_type=jnp.float32)


def _ffn_end(o_ref, lnf_ref):
    if lnf_ref is not None:
        @pl.when(pl.program_id(1) == pl.num_programs(1) - 1)
        def _():
            o_ref[...] = _rms(o_ref[...], lnf_ref[...])


def _ffn_head_kernel(x_ref, ln_ref, wg_ref, wu_ref, wd_ref, *rest, final_norm):
    rest = list(rest)
    lnf_ref = rest.pop(0) if final_norm else None
    o_ref, wgu16_ref, wd16_ref, h_ref = rest
    _ffn_begin(x_ref, ln_ref, h_ref, o_ref)
    wgu = jnp.concatenate([wg_ref[...].astype(jnp.bfloat16),
                           wu_ref[...].astype(jnp.bfloat16)], axis=1)
    wd = wd_ref[...].astype(jnp.bfloat16)
    wgu16_ref[...] = wgu
    wd16_ref[...] = wd
    o_ref[...] += _ffn_tile(h_ref[...], wgu, wd)
    _ffn_end(o_ref, lnf_ref)


def _ffn_rest_kernel(x_ref, ln_ref, wgu_ref, wd_ref, *rest, final_norm):
    rest = list(rest)
    lnf_ref = rest.pop(0) if final_norm else None
    _, o_ref, h_ref = rest
    _ffn_begin(x_ref, ln_ref, h_ref, o_ref)
    h = h_ref[...]
    tf = wgu_ref.shape[2] // 2
    y = None
    for t in range(wgu_ref.shape[0]):
        yt = _ffn_tile(h, wgu_ref[t], wd_ref[t * tf:(t + 1) * tf, :])
        y = yt if y is None else y + yt
    o_ref[...] += y
    _ffn_end(o_ref, lnf_ref)


def _ffn(x, ln, wg, wu, wd, layer, ln_final=None):
    final_norm = ln_final is not None
    tm = FFN_TM
    bf = jnp.bfloat16
    out_shape = jax.ShapeDtypeStruct((TOKENS, D_MODEL), jnp.float32)
    vec_spec = pl.BlockSpec((1, D_MODEL), lambda i, j: (0, 0))
    tail_specs = [vec_spec] if final_norm else []
    tail_args = [ln_final] if final_norm else []
    suffix = "_final" if final_norm else ""

    th = FFN_TF_HEAD
    n_head_tiles = D_FF // th
    group = FFN_TF // th
    scratch = [pltpu.VMEM((tm, D_MODEL), bf)]
    cparams = _cparams(("parallel", "arbitrary"), 60)

    y, wgu16, wd16 = pl.pallas_call(
        functools.partial(_ffn_head_kernel, final_norm=final_norm),
        grid=(1, n_head_tiles),
        in_specs=[pl.BlockSpec((tm, D_MODEL), lambda i, j: (0, 0), pipeline_mode=pl.Buffered(1)),
                  vec_spec,
                  pl.BlockSpec((None, D_MODEL, th), lambda i, j: (layer, 0, j)),
                  pl.BlockSpec((None, D_MODEL, th), lambda i, j: (layer, 0, j)),
                  pl.BlockSpec((None, th, D_MODEL), lambda i, j: (layer, j, 0))] + tail_specs,
        out_specs=[pl.BlockSpec((tm, D_MODEL), lambda i, j: (0, 0)),
                   pl.BlockSpec((None, D_MODEL, 2 * th), lambda i, j: (j, 0, 0)),
                   pl.BlockSpec((th, D_MODEL), lambda i, j: (j, 0))],
        out_shape=[out_shape, jax.ShapeDtypeStruct((n_head_tiles, D_MODEL, 2 * th), bf),
                   jax.ShapeDtypeStruct((D_FF, D_MODEL), bf)],
        scratch_shapes=scratch,
        compiler_params=cparams,
        name="ffn_head" + suffix,
    )(x, ln, wg, wu, wd, *tail_args)

    alias_index = 4 + len(tail_args)
    return pl.pallas_call(
        functools.partial(_ffn_rest_kernel, final_norm=final_norm),
        grid=(TOKENS // tm - 1, n_head_tiles // group),
        in_specs=[pl.BlockSpec((tm, D_MODEL), lambda i, j: (i + 1, 0)),
                  vec_spec,
                  pl.BlockSpec((group, D_MODEL, 2 * th), lambda i, j: (j, 0, 0)),
                  pl.BlockSpec((group * th, D_MODEL), lambda i, j: (j, 0))] + tail_specs
        + [pl.BlockSpec(memory_space=pl.ANY)],
        out_specs=pl.BlockSpec((tm, D_MODEL), lambda i, j: (i + 1, 0)),
        out_shape=out_shape,
        scratch_shapes=scratch,
        input_output_aliases={alias_index: 0},
        compiler_params=cparams,
        name="ffn_rest" + suffix,
    )(x, ln, wgu16, wd16, *tail_args, y)


PROJ_WIDTH = MLA_IN + MOBA_IN + DIL_IN


def _mla_prep(lat, kr, gq_ref, gkv_ref, wuq_ref, wuk_ref, wuv_ref, cos_ref, sin_ref,
              q_ref, k_ref, v_ref):
    scale = (MLA_NOPE + MLA_ROPE) ** -0.5 * LOG2E
    c = cos_ref[...]
    s = sin_ref[...]

    def rope(t):
        return t * c + pltpu.roll(t, LANES // 2, 1) * s

    nq = MLA_Q_RANK // LANES
    cq = _rms(jnp.concatenate(lat[:nq], axis=1), gq_ref[...]).astype(jnp.bfloat16)
    q = jnp.dot(cq, wuq_ref[...], preferred_element_type=jnp.float32)
    ckv = _rms(jnp.concatenate(lat[nq:], axis=1), gkv_ref[...]).astype(jnp.bfloat16)
    kn = jnp.dot(ckv, wuk_ref[...], preferred_element_type=jnp.float32)
    v_ref[...] = jnp.dot(ckv, wuv_ref[...],
                         preferred_element_type=jnp.float32).astype(jnp.bfloat16)
    kr = rope(kr).astype(jnp.bfloat16)
    for h in range(MLA_HEADS):
        b0 = h * MLA_QK_PAD
        q_ref[:, b0:b0 + LANES] = (q[:, b0:b0 + LANES] * scale).astype(jnp.bfloat16)
        q_ref[:, b0 + LANES:b0 + 2 * LANES] = (
            rope(q[:, b0 + LANES:b0 + 2 * LANES]) * scale).astype(jnp.bfloat16)
        k_ref[:, b0:b0 + LANES] = kn[:, h * LANES:(h + 1) * LANES].astype(jnp.bfloat16)
        k_ref[:, b0 + LANES:b0 + 2 * LANES] = kr


def _proj_kernel(x_ref, ln_ref, wt_ref, cos_ref, sin_ref, *rest):
    mla_args, (moba_ref, dil_ref) = rest[:-2], rest[-2:]
    h = _rms(x_ref[...], ln_ref[...]).astype(jnp.bfloat16)

    def proj(r0, rows):
        return _dot_nt(h, wt_ref[r0:r0 + rows, :])

    n_lat = MLA_Q_RANK + MLA_KV_RANK
    lat = []
    for r0 in range(0, n_lat, PROJ_TN):
        y = proj(r0, min(PROJ_TN, n_lat - r0))
        lat += [y[:, g0:g0 + LANES] for g0 in range(0, y.shape[1], LANES)]
    yk = proj(n_lat, LANES)
    lane = lax.broadcasted_iota(jnp.int32, yk.shape, 1)
    half = MLA_ROPE // 2
    kr = (jnp.where(lane < half, yk, 0.0)
          + jnp.where((lane >= LANES // 2) & (lane < LANES // 2 + half),
                      pltpu.roll(yk, half, 1), 0.0))
    _mla_prep(lat, kr, *mla_args)

    row = MLA_IN
    for o_ref, width in ((moba_ref, MOBA_IN), (dil_ref, DIL_IN)):
        for c0 in range(0, width, PROJ_TN):
            y = proj(row + c0, PROJ_TN)
            section = c0 // (width // 3)
            if section == 2:
                o_ref[:, c0:c0 + PROJ_TN] = y.astype(o_ref.dtype)
                continue
            c = cos_ref[section]
            s = sin_ref[section]
            for g0 in range(0, PROJ_TN, LANES):
                yg = y[:, g0:g0 + LANES]
                o_ref[:, c0 + g0:c0 + g0 + LANES] = (
                    yg * c + pltpu.roll(yg, LANES // 2, 1) * s).astype(o_ref.dtype)
        row += width


def _proj(x, ln, w, layer, cos_tab, sin_tab, gq, gkv, wuq, wuk, wuv, cos_r, sin_r):
    tm = PROJ_TM
    pos_blocks = SEQ // tm
    qk_w = MLA_HEADS * MLA_QK_PAD
    v_w = MLA_HEADS * MLA_V
    bf = jnp.bfloat16

    def rows(width):
        return pl.BlockSpec((tm, width), lambda i: (i, 0))

    def full(shape):
        return pl.BlockSpec(shape, lambda i: (0, 0))

    def tab_spec():
        return pl.BlockSpec((2, tm, LANES), lambda i: (0, i % pos_blocks, 0))

    def rtab_spec():
        return pl.BlockSpec((tm, LANES), lambda i: (i % pos_blocks, 0))

    return pl.pallas_call(
        _proj_kernel,
        grid=(TOKENS // tm,),
        in_specs=[
            rows(D_MODEL),
            full((1, D_MODEL)),
            pl.BlockSpec((None, PROJ_WIDTH, D_MODEL), lambda i: (layer, 0, 0),
                         pipeline_mode=pl.Buffered(1)),
            tab_spec(), tab_spec(),
            full((1, MLA_Q_RANK)), full((1, MLA_KV_RANK)),
            full((MLA_Q_RANK, qk_w)), full((MLA_KV_RANK, v_w)), full((MLA_KV_RANK, v_w)),
            rtab_spec(), rtab_spec(),
        ],
        out_specs=[rows(qk_w), rows(qk_w), rows(v_w), rows(MOBA_IN), rows(DIL_IN)],
        out_shape=[
            jax.ShapeDtypeStruct((TOKENS, qk_w), bf),
            jax.ShapeDtypeStruct((TOKENS, qk_w), bf),
            jax.ShapeDtypeStruct((TOKENS, v_w), bf),
            jax.ShapeDtypeStruct((TOKENS, MOBA_IN), bf),
            jax.ShapeDtypeStruct((TOKENS, DIL_IN), bf),
        ],
        compiler_params=_cparams(("parallel",), 56),
        name="mix_proj",
    )(x, ln, w, cos_tab, sin_tab, gq, gkv, wuq, wuk, wuv, cos_r, sin_r)


def _fill_v_ones(v_ref, va_ref):
    width = v_ref.shape[1]
    va_ref[:, :width] = v_ref[...]
    va_ref[:, width:] = jnp.ones((v_ref.shape[0], va_ref.shape[1] - width), va_ref.dtype)


_V_ONES_SCRATCH = [pltpu.VMEM((SEQ, 2 * HEAD_DIM), jnp.bfloat16)]


def _mla_attn_kernel(q_ref, k_ref, v_ref, o_ref, va_ref):
    _fill_v_ones(v_ref, va_ref)
    t = MLA_T
    tri = _causal_tri(t)
    for c in reversed(range(SEQ // t)):
        n = (c + 1) * t
        s = _dot_nt(q_ref[_tile(c, t), :], k_ref[0:n, :])
        parts = [s[:, _tile(j, t)] for j in range(c)]
        parts.append(jnp.where(tri, s[:, _tile(c, t)], NEG_INF))
        o_ref[_tile(c, t), :] = _softmax_pv(parts, va_ref)


def _mla_attn(q, k, v):
    return pl.pallas_call(
        _mla_attn_kernel,
        grid=(BATCH, MLA_HEADS),
        in_specs=[_head_spec(MLA_QK_PAD, 0), _head_spec(MLA_QK_PAD, 0), _head_spec(MLA_V, 0)],
        out_specs=_head_spec(MLA_V, 0),
        out_shape=jax.ShapeDtypeStruct((TOKENS, MLA_HEADS * MLA_V), jnp.bfloat16),
        scratch_shapes=_V_ONES_SCRATCH,
        compiler_params=_cparams(("parallel", "parallel"), ATT_VMEM_MIB),
        name="mla_attn",
    )(q, k, v)


def _moba_attn_kernel(q_ref, k_ref, v_ref, o_ref, va_ref):
    _fill_v_ones(v_ref, va_ref)
    rid = lax.broadcasted_iota(jnp.int32, (LANES, HEAD_DIM), 0)
    km = jnp.zeros((LANES, HEAD_DIM), jnp.float32)
    for j in range(MOBA_NBLK):
        kj = k_ref[_tile(j), :].astype(jnp.float32)
        mean_j = jnp.sum(kj, axis=0, keepdims=True) * (1.0 / MOBA_BLOCK)
        km = jnp.where(rid == j, mean_j, km)
    km_hi = km.astype(jnp.bfloat16)
    km_lo = (km - km_hi.astype(jnp.float32)).astype(jnp.bfloat16)
    tri = _causal_tri()

    for c in ATT_ORDER:
        n = (c + 1) * ATT_T
        q = q_ref[_tile(c), :]
        s = _dot_nt(q, k_ref[0:n, :])
        parts = [s[:, _tile(j)] for j in range(c)]
        if c > MOBA_TOPK:
            gate = _dot_nt(q, km_hi) + _dot_nt(q, km_lo)
            lane = lax.broadcasted_iota(jnp.int32, gate.shape, 1)
            ahead = jnp.zeros(gate.shape, jnp.float32)
            for jp in range(c):
                cj = _lane_col(gate, jp, lane)
                wins = (cj > gate) | ((cj == gate) & (lane > jp))
                ahead = ahead + jnp.where(wins, 1.0, 0.0)
            sel = jnp.where(ahead < MOBA_TOPK, 1.0, 0.0)
            parts = [jnp.where(_lane_col(sel, j, lane) > 0.5, parts[j], NEG_INF)
                     for j in range(c)]
        parts.append(jnp.where(tri, s[:, _tile(c)], NEG_INF))
        o_ref[_tile(c), :] = _softmax_pv(parts, va_ref)


def _moba_attn(qkv):
    return pl.pallas_call(
        _moba_attn_kernel,
        grid=(BATCH, MOBA_HEADS),
        in_specs=[_head_spec(HEAD_DIM, 0), _head_spec(HEAD_DIM, MOBA_HEADS),
                  _head_spec(HEAD_DIM, 2 * MOBA_HEADS)],
        out_specs=_head_spec(HEAD_DIM, 0),
        out_shape=jax.ShapeDtypeStruct((TOKENS, MOBA_HEADS * HEAD_DIM), jnp.bfloat16),
        scratch_shapes=_V_ONES_SCRATCH,
        compiler_params=_cparams(("parallel", "parallel"), ATT_VMEM_MIB),
        name="moba_attn",
    )(qkv, qkv, qkv)


def _dil_tables():
    t = DIL_T
    r = np.arange(t)[:, None]
    c = np.arange(t)[None, :]
    cnts = []
    for d in range(SEQ // t):
        delta = r - c + t * d
        cnts.append(sum(((delta >= 0) & (delta <= w) & (delta % dil == 0)).astype(np.float32)
                        for w, dil in DIL_PATTERNS))
    cnt = np.stack(cnts)
    bias = np.where(cnt > 0, 0.0, NEG_INF).astype(np.float32)
    n_weighted = max(d + 1 for d in range(SEQ // t) if cnt[d].max() > 1)
    return bias, cnt[:n_weighted]


def _dil_attn_kernel(q_ref, k_ref, v_ref, bias_ref, cnt_ref, o_ref, va_ref):
    n_weighted = cnt_ref.shape[0]
    for hh in range(DIL_HPS):
        hs = slice(hh * HEAD_DIM, (hh + 1) * HEAD_DIM)
        va = va_ref.at[hh]
        va[:, :HEAD_DIM] = v_ref[:, hs]
        va[:, HEAD_DIM:] = jnp.ones((SEQ, HEAD_DIM), va_ref.dtype)
    t = DIL_T
    for c in reversed(range(SEQ // t)):
        n = (c + 1) * t
        weights = [cnt_ref[c - j] if c - j < n_weighted else None for j in range(c + 1)]
        for hh in range(DIL_HPS):
            hs = slice(hh * HEAD_DIM, (hh + 1) * HEAD_DIM)
            s = _dot_nt(q_ref[_tile(c, t), hs], k_ref[0:n, hs])
            tiles = [s[:, _tile(j, t)] + bias_ref[c - j] for j in range(c + 1)]
            o_ref[_tile(c, t), hs] = _softmax_pv(tiles, va_ref.at[hh], weights)


def _dil_attn(qkv):
    bias, cnt = _dil_tables()
    return pl.pallas_call(
        _dil_attn_kernel,
        grid=(BATCH, DIL_HEADS // DIL_HPS),
        in_specs=[_head_spec(DIL_HPS * HEAD_DIM, 0),
                  _head_spec(DIL_HPS * HEAD_DIM, DIL_HEADS // DIL_HPS),
                  _head_spec(DIL_HPS * HEAD_DIM, 2 * DIL_HEADS // DIL_HPS),
                  pl.BlockSpec(bias.shape, lambda b, h: (0, 0, 0)),
                  pl.BlockSpec(cnt.shape, lambda b, h: (0, 0, 0))],
        out_specs=_head_spec(DIL_HPS * HEAD_DIM, 0),
        out_shape=jax.ShapeDtypeStruct((TOKENS, DIL_HEADS * HEAD_DIM), jnp.bfloat16),
        scratch_shapes=[pltpu.VMEM((DIL_HPS, SEQ, 2 * HEAD_DIM), jnp.bfloat16)],
        compiler_params=_cparams(("parallel", "parallel"), ATT_VMEM_MIB),
        name="dil_attn",
    )(qkv, qkv, qkv, jnp.asarray(bias), jnp.asarray(cnt))


def _out_proj_kernel(x_ref, mla_ref, moba_ref, dil_ref, w_ref, o_ref, wb_ref):
    @pl.when(pl.program_id(0) == 0)
    def _():
        wb_ref[...] = w_ref[...].astype(jnp.bfloat16)

    mix = jnp.concatenate([mla_ref[...], moba_ref[...], dil_ref[...]], axis=1)
    o_ref[...] = x_ref[...] + jnp.dot(mix, wb_ref[...], preferred_element_type=jnp.float32)


def _out_proj(x, o_mla, o_moba, o_dil, w_out, layer):
    tm = OUT_TM

    def rows(width):
        return pl.BlockSpec((tm, width), lambda i: (i, 0))

    return pl.pallas_call(
        _out_proj_kernel,
        grid=(TOKENS // tm,),
        in_specs=[rows(D_MODEL), rows(MLA_HEADS * MLA_V), rows(MOBA_HEADS * HEAD_DIM),
                  rows(DIL_HEADS * HEAD_DIM),
                  pl.BlockSpec((None, MIX_WIDTH, D_MODEL), lambda i: (layer, 0, 0),
                               pipeline_mode=pl.Buffered(1))],
        out_specs=rows(D_MODEL),
        out_shape=jax.ShapeDtypeStruct((TOKENS, D_MODEL), jnp.float32),
        scratch_shapes=[pltpu.VMEM((MIX_WIDTH, D_MODEL), jnp.bfloat16)],
        compiler_params=_cparams(("arbitrary",), 56),
        name="out_proj",
    )(x, o_mla, o_moba, o_dil, w_out)


def _rope_tables():
    pos = np.arange(SEQ, dtype=np.float64)[:, None]
    inv_h = ROPE_THETA ** (-np.arange(0, HEAD_DIM, 2, dtype=np.float64) / HEAD_DIM)
    ang = pos * inv_h[None, :]
    cos_h = np.concatenate([np.cos(ang), np.cos(ang)], axis=1)
    sin_h = np.concatenate([-np.sin(ang), np.sin(ang)], axis=1)
    scale = HEAD_DIM ** -0.5 * LOG2E
    cos_qk = np.stack([cos_h * scale, cos_h])
    sin_qk = np.stack([sin_h * scale, sin_h])
    inv_r = ROPE_THETA ** (-np.arange(0, MLA_ROPE, 2, dtype=np.float64) / MLA_ROPE)
    ang_r = pos * inv_r[None, :]
    z = np.zeros_like(ang_r)
    cos_r = np.concatenate([np.cos(ang_r), z, np.cos(ang_r), z], axis=1)
    sin_r = np.concatenate([-np.sin(ang_r), z, np.sin(ang_r), z], axis=1)
    return tuple(jnp.asarray(t, jnp.float32) for t in (cos_qk, sin_qk, cos_r, sin_r))


def _pad_rope_cols(w):
    half = MLA_ROPE // 2
    z = jnp.zeros(w.shape[:-1] + (half,), w.dtype)
    return jnp.concatenate([w[..., :half], z, w[..., half:], z], axis=-1)


def _prep_mla_weights(w_uq, w_uk, w_uv):
    bf = jnp.bfloat16
    uq = w_uq.reshape(MLA_Q_RANK, MLA_HEADS, MLA_NOPE + MLA_ROPE)
    uq = jnp.concatenate([uq[..., :MLA_NOPE], _pad_rope_cols(uq[..., MLA_NOPE:])], axis=-1)
    uq = uq.reshape(MLA_Q_RANK, MLA_HEADS * MLA_QK_PAD).astype(bf)
    return uq, w_uk.astype(bf), w_uv.astype(bf)


def kernel(x, ln_ffn1, w_ffn1_gate, w_ffn1_up, w_ffn1_down, ln_mix, w_in, g_mla_q, g_mla_kv,
           w_mla_uq, w_mla_uk, w_mla_uv, w_out, ln_ffn2, w_ffn2_gate, w_ffn2_up, w_ffn2_down,
           ln_final):
    cos_qk, sin_qk, cos_r, sin_r = _rope_tables()
    w_proj = jnp.swapaxes(w_in, 1, 2).astype(jnp.bfloat16)
    xt = x.reshape(TOKENS, D_MODEL)
    for l in range(DEPTH):
        uq, uk, uv = _prep_mla_weights(w_mla_uq[l], w_mla_uk[l], w_mla_uv[l])
        xt = _ffn(xt, ln_ffn1[l][None], w_ffn1_gate, w_ffn1_up, w_ffn1_down, l)
        q_mla, k_mla, v_mla, qkv_moba, qkv_dil = _proj(
            xt, ln_mix[l][None], w_proj, l, cos_qk, sin_qk,
            g_mla_q[l][None], g_mla_kv[l][None], uq, uk, uv, cos_r, sin_r)
        o_mla = _mla_attn(q_mla, k_mla, v_mla)
        o_moba = _moba_attn(qkv_moba)
        o_dil = _dil_attn(qkv_dil)
        xt = _out_proj(xt, o_mla, o_moba, o_dil, w_out, l)
        xt = _ffn(xt, ln_ffn2[l][None], w_ffn2_gate, w_ffn2_up, w_ffn2_down, l,
                  ln_final=ln_final[None] if l == DEPTH - 1 else None)
    return xt.reshape(BATCH, SEQ, D_MODEL)
```

```python
import functools

import numpy as np

import jax
import jax.numpy as jnp
from jax import lax
from jax.experimental import pallas as pl
from jax.experimental.pallas import tpu as pltpu

D_MODEL = 2048
BATCH = 4
SEQ = 2048
DEPTH = 2
TOKENS = BATCH * SEQ

HEAD_DIM = 128
MLA_HEADS = 4
MLA_Q_RANK = 512
MLA_KV_RANK = 256
MLA_NOPE = 128
MLA_ROPE = 64
MLA_V = 128
MLA_QK_PAD = 256
MOBA_HEADS = 4
MOBA_BLOCK = 256
MOBA_TOPK = 3
MOBA_NBLK = SEQ // MOBA_BLOCK
DIL_HEADS = 8
DIL_PATTERNS = ((128, 1), (512, 4), (2048, 16))
D_FF = 5632
ROPE_THETA = 10000.0
NORM_EPS = 1e-6
NEG_INF = -1e30
LOG2E = 1.4426950408889634

MLA_IN = MLA_Q_RANK + MLA_KV_RANK + MLA_ROPE
MOBA_IN = 3 * MOBA_HEADS * HEAD_DIM
DIL_IN = 3 * DIL_HEADS * HEAD_DIM
MIX_WIDTH = MLA_HEADS * MLA_V + MOBA_HEADS * HEAD_DIM + DIL_HEADS * HEAD_DIM

LANES = 128
MIB = 1024 * 1024

FFN_TM = 1024
FFN_TF = 512
FFN_TF_HEAD = 256
PROJ_TM = 512
PROJ_TN = 512
ATT_T = 256
ATT_NT = SEQ // ATT_T
MLA_T = 512
DIL_T = 256
ATT_ORDER = tuple(reversed(range(ATT_NT)))
ATT_VMEM_MIB = 56
OUT_TM = 512
DIL_HPS = 2

assert ATT_T == MOBA_BLOCK


def _cparams(semantics, vmem_mib):
    return pltpu.CompilerParams(dimension_semantics=semantics,
                                vmem_limit_bytes=vmem_mib * MIB)


def _rms(x, g):
    ms = jnp.mean(x * x, axis=-1, keepdims=True)
    return x * lax.rsqrt(ms + NORM_EPS) * g


def _lane_col(a, j, lane):
    return jnp.sum(jnp.where(lane == j, a, 0.0), axis=1, keepdims=True)


def _dot_nt(a, b):
    return lax.dot_general(a, b, (((1,), (1,)), ((), ())),
                           preferred_element_type=jnp.float32)


def _tile(c, t=ATT_T):
    return slice(c * t, (c + 1) * t)


def _causal_tri(t=ATT_T):
    r = lax.broadcasted_iota(jnp.int32, (t, t), 0)
    c = lax.broadcasted_iota(jnp.int32, (t, t), 1)
    return c <= r


def _softmax_pv(tiles, v_ref, weights=None):
    mx = tiles[0]
    for t in tiles[1:]:
        mx = jnp.maximum(mx, t)
    m = jnp.max(mx, axis=1, keepdims=True)
    res = None
    for j, t in enumerate(tiles):
        e = jnp.exp2(t - m)
        if weights is not None and weights[j] is not None:
            e = e * weights[j]
        pv = jnp.dot(e.astype(jnp.bfloat16), v_ref[_tile(j, t.shape[1]), :],
                     preferred_element_type=jnp.float32)
        res = pv if res is None else res + pv
    half = res.shape[1] // 2
    return (res[:, :half] / res[:, half:]).astype(jnp.bfloat16)


def _head_spec(width, col_offset):
    return pl.BlockSpec((SEQ, width), lambda b, h: (b, col_offset + h))


def _ffn_begin(x_ref, ln_ref, h_ref, o_ref):
    @pl.when(pl.program_id(1) == 0)
    def _():
        x = x_ref[...]
        h_ref[...] = _rms(x, ln_ref[...]).astype(jnp.bfloat16)
        o_ref[...] = x


def _ffn_act(h, wgu):
    tf = wgu.shape[1] // 2
    gu = jnp.dot(h, wgu, preferred_element_type=jnp.float32)
    g = gu[:, :tf]
    u = gu[:, tf:]
    return (g * (1.0 / (1.0 + jnp.exp(-g))) * u * 0.5).astype(jnp.bfloat16)


def _ffn_end(o_ref, lnf_ref):
    if lnf_ref is not None:
        @pl.when(pl.program_id(1) == pl.num_programs(1) - 1)
        def _():
            o_ref[...] = _rms(o_ref[...], lnf_ref[...])


def _ffn_head_kernel(x_ref, ln_ref, wg_ref, wu_ref, wd_ref, *rest, final_norm):
    rest = list(rest)
    lnf_ref = rest.pop(0) if final_norm else None
    o_ref, wgu16_ref, wd16_ref, h_ref = rest
    _ffn_begin(x_ref, ln_ref, h_ref, o_ref)
    wgu = jnp.concatenate([wg_ref[...].astype(jnp.bfloat16),
                           wu_ref[...].astype(jnp.bfloat16)], axis=1)
    wd = wd_ref[...].astype(jnp.bfloat16)
    wgu16_ref[...] = wgu
    wd16_ref[...] = wd
    o_ref[...] += jnp.dot(_ffn_act(h_ref[...], wgu), wd, preferred_element_type=jnp.float32)
    _ffn_end(o_ref, lnf_ref)


def _ffn_rest_kernel(x_ref, ln_ref, wgu_ref, wd_ref, *rest, final_norm):
    rest = list(rest)
    lnf_ref = rest.pop(0) if final_norm else None
    _, o_ref, h_ref = rest
    _ffn_begin(x_ref, ln_ref, h_ref, o_ref)
    h = h_ref[...]
    a = jnp.concatenate([_ffn_act(h, wgu_ref[t]) for t in range(wgu_ref.shape[0])], axis=1)
    o_ref[...] += jnp.dot(a, wd_ref[...], preferred_element_type=jnp.float32)
    _ffn_end(o_ref, lnf_ref)


def _ffn(x, ln, wg, wu, wd, layer, ln_final=None):
    final_norm = ln_final is not None
    tm = FFN_TM
    bf = jnp.bfloat16
    out_shape = jax.ShapeDtypeStruct((TOKENS, D_MODEL), jnp.float32)
    vec_spec = pl.BlockSpec((1, D_MODEL), lambda i, j: (0, 0))
    tail_specs = [vec_spec] if final_norm else []
    tail_args = [ln_final] if final_norm else []
    suffix = "_final" if final_norm else ""

    th = FFN_TF_HEAD
    n_head_tiles = D_FF // th
    group = FFN_TF // th
    scratch = [pltpu.VMEM((tm, D_MODEL), bf)]
    cparams = _cparams(("parallel", "arbitrary"), 60)

    y, wgu16, wd16 = pl.pallas_call(
        functools.partial(_ffn_head_kernel, final_norm=final_norm),
        grid=(1, n_head_tiles),
        in_specs=[pl.BlockSpec((tm, D_MODEL), lambda i, j: (0, 0), pipeline_mode=pl.Buffered(1)),
                  vec_spec,
                  pl.BlockSpec((None, D_MODEL, th), lambda i, j: (layer, 0, j)),
                  pl.BlockSpec((None, D_MODEL, th), lambda i, j: (layer, 0, j)),
                  pl.BlockSpec((None, th, D_MODEL), lambda i, j: (layer, j, 0))] + tail_specs,
        out_specs=[pl.BlockSpec((tm, D_MODEL), lambda i, j: (0, 0)),
                   pl.BlockSpec((None, D_MODEL, 2 * th), lambda i, j: (j, 0, 0)),
                   pl.BlockSpec((th, D_MODEL), lambda i, j: (j, 0))],
        out_shape=[out_shape, jax.ShapeDtypeStruct((n_head_tiles, D_MODEL, 2 * th), bf),
                   jax.ShapeDtypeStruct((D_FF, D_MODEL), bf)],
        scratch_shapes=scratch,
        compiler_params=cparams,
        name="ffn_head" + suffix,
    )(x, ln, wg, wu, wd, *tail_args)

    alias_index = 4 + len(tail_args)
    return pl.pallas_call(
        functools.partial(_ffn_rest_kernel, final_norm=final_norm),
        grid=(TOKENS // tm - 1, n_head_tiles // group),
        in_specs=[pl.BlockSpec((tm, D_MODEL), lambda i, j: (i + 1, 0)),
                  vec_spec,
                  pl.BlockSpec((group, D_MODEL, 2 * th), lambda i, j: (j, 0, 0)),
                  pl.BlockSpec((group * th, D_MODEL), lambda i, j: (j, 0))] + tail_specs
        + [pl.BlockSpec(memory_space=pl.ANY)],
        out_specs=pl.BlockSpec((tm, D_MODEL), lambda i, j: (i + 1, 0)),
        out_shape=out_shape,
        scratch_shapes=scratch,
        input_output_aliases={alias_index: 0},
        compiler_params=cparams,
        name="ffn_rest" + suffix,
    )(x, ln, wgu16, wd16, *tail_args, y)


PROJ_WIDTH = MLA_IN + MOBA_IN + DIL_IN


def _mla_prep(lat, kr, gq_ref, gkv_ref, wuq_ref, wuk_ref, wuv_ref, cos_ref, sin_ref,
              q_ref, k_ref, v_ref):
    scale = (MLA_NOPE + MLA_ROPE) ** -0.5 * LOG2E
    c = cos_ref[...]
    s = sin_ref[...]

    def rope(t):
        return t * c + pltpu.roll(t, LANES // 2, 1) * s

    nq = MLA_Q_RANK // LANES
    cq = _rms(jnp.concatenate(lat[:nq], axis=1), gq_ref[...]).astype(jnp.bfloat16)
    q = jnp.dot(cq, wuq_ref[...], preferred_element_type=jnp.float32)
    ckv = _rms(jnp.concatenate(lat[nq:], axis=1), gkv_ref[...]).astype(jnp.bfloat16)
    kn = jnp.dot(ckv, wuk_ref[...], preferred_element_type=jnp.float32)
    v_ref[...] = jnp.dot(ckv, wuv_ref[...],
                         preferred_element_type=jnp.float32).astype(jnp.bfloat16)
    kr = rope(kr).astype(jnp.bfloat16)
    for h in range(MLA_HEADS):
        b0 = h * MLA_QK_PAD
        q_ref[:, b0:b0 + LANES] = (q[:, b0:b0 + LANES] * scale).astype(jnp.bfloat16)
        q_ref[:, b0 + LANES:b0 + 2 * LANES] = (
            rope(q[:, b0 + LANES:b0 + 2 * LANES]) * scale).astype(jnp.bfloat16)
        k_ref[:, b0:b0 + LANES] = kn[:, h * LANES:(h + 1) * LANES].astype(jnp.bfloat16)
        k_ref[:, b0 + LANES:b0 + 2 * LANES] = kr


def _proj_kernel(x_ref, ln_ref, wt_ref, cos_ref, sin_ref, *rest):
    mla_args, (moba_ref, dil_ref) = rest[:-2], rest[-2:]
    h = _rms(x_ref[...], ln_ref[...]).astype(jnp.bfloat16)

    def proj(r0, rows):
        return _dot_nt(h, wt_ref[r0:r0 + rows, :])

    n_lat = MLA_Q_RANK + MLA_KV_RANK
    lat = []
    for r0 in range(0, n_lat, PROJ_TN):
        y = proj(r0, min(PROJ_TN, n_lat - r0))
        lat += [y[:, g0:g0 + LANES] for g0 in range(0, y.shape[1], LANES)]
    yk = proj(n_lat, LANES)
    lane = lax.broadcasted_iota(jnp.int32, yk.shape, 1)
    half = MLA_ROPE // 2
    kr = (jnp.where(lane < half, yk, 0.0)
          + jnp.where((lane >= LANES // 2) & (lane < LANES // 2 + half),
                      pltpu.roll(yk, half, 1), 0.0))
    _mla_prep(lat, kr, *mla_args)

    row = MLA_IN
    for o_ref, width in ((moba_ref, MOBA_IN), (dil_ref, DIL_IN)):
        for c0 in range(0, width, PROJ_TN):
            y = proj(row + c0, PROJ_TN)
            section = c0 // (width // 3)
            if section == 2:
                o_ref[:, c0:c0 + PROJ_TN] = y.astype(o_ref.dtype)
                continue
            c = cos_ref[section]
            s = sin_ref[section]
            for g0 in range(0, PROJ_TN, LANES):
                yg = y[:, g0:g0 + LANES]
                o_ref[:, c0 + g0:c0 + g0 + LANES] = (
                    yg * c + pltpu.roll(yg, LANES // 2, 1) * s).astype(o_ref.dtype)
        row += width


def _proj(x, ln, w, layer, cos_tab, sin_tab, gq, gkv, wuq, wuk, wuv, cos_r, sin_r):
    tm = PROJ_TM
    pos_blocks = SEQ // tm
    qk_w = MLA_HEADS * MLA_QK_PAD
    v_w = MLA_HEADS * MLA_V
    bf = jnp.bfloat16

    def rows(width):
        return pl.BlockSpec((tm, width), lambda i: (i, 0))

    def full(shape):
        return pl.BlockSpec(shape, lambda i: (0, 0))

    def tab_spec():
        return pl.BlockSpec((2, tm, LANES), lambda i: (0, i % pos_blocks, 0))

    def rtab_spec():
        return pl.BlockSpec((tm, LANES), lambda i: (i % pos_blocks, 0))

    return pl.pallas_call(
        _proj_kernel,
        grid=(TOKENS // tm,),
        in_specs=[
            rows(D_MODEL),
            full((1, D_MODEL)),
            pl.BlockSpec((None, PROJ_WIDTH, D_MODEL), lambda i: (layer, 0, 0),
                         pipeline_mode=pl.Buffered(1)),
            tab_spec(), tab_spec(),
            full((1, MLA_Q_RANK)), full((1, MLA_KV_RANK)),
            full((MLA_Q_RANK, qk_w)), full((MLA_KV_RANK, v_w)), full((MLA_KV_RANK, v_w)),
            rtab_spec(), rtab_spec(),
        ],
        out_specs=[rows(qk_w), rows(qk_w), rows(v_w), rows(MOBA_IN), rows(DIL_IN)],
        out_shape=[
            jax.ShapeDtypeStruct((TOKENS, qk_w), bf),
            jax.ShapeDtypeStruct((TOKENS, qk_w), bf),
            jax.ShapeDtypeStruct((TOKENS, v_w), bf),
            jax.ShapeDtypeStruct((TOKENS, MOBA_IN), bf),
            jax.ShapeDtypeStruct((TOKENS, DIL_IN), bf),
        ],
        compiler_params=_cparams(("parallel",), 56),
        name="mix_proj",
    )(x, ln, w, cos_tab, sin_tab, gq, gkv, wuq, wuk, wuv, cos_r, sin_r)


def _fill_v_ones(v_ref, va_ref):
    width = v_ref.shape[1]
    va_ref[:, :width] = v_ref[...]
    va_ref[:, width:] = jnp.ones((v_ref.shape[0], va_ref.shape[1] - width), va_ref.dtype)


_V_ONES_SCRATCH = [pltpu.VMEM((SEQ, 2 * HEAD_DIM), jnp.bfloat16)]


def _mla_attn_kernel(q_ref, k_ref, v_ref, o_ref, va_ref):
    _fill_v_ones(v_ref, va_ref)
    t = MLA_T
    tri = _causal_tri(t)
    for c in reversed(range(SEQ // t)):
        n = (c + 1) * t
        s = _dot_nt(q_ref[_tile(c, t), :], k_ref[0:n, :])
        parts = [s[:, _tile(j, t)] for j in range(c)]
        parts.append(jnp.where(tri, s[:, _tile(c, t)], NEG_INF))
        o_ref[_tile(c, t), :] = _softmax_pv(parts, va_ref)


def _moba_attn_kernel(q_ref, k_ref, v_ref, o_ref, va_ref):
    _fill_v_ones(v_ref, va_ref)
    rid = lax.broadcasted_iota(jnp.int32, (LANES, HEAD_DIM), 0)
    km = jnp.zeros((LANES, HEAD_DIM), jnp.float32)
    for j in range(MOBA_NBLK):
        kj = k_ref[_tile(j), :].astype(jnp.float32)
        mean_j = jnp.sum(kj, axis=0, keepdims=True) * (1.0 / MOBA_BLOCK)
        km = jnp.where(rid == j, mean_j, km)
    km_hi = km.astype(jnp.bfloat16)
    km_lo = (km - km_hi.astype(jnp.float32)).astype(jnp.bfloat16)
    tri = _causal_tri()

    for c in ATT_ORDER:
        n = (c + 1) * ATT_T
        q = q_ref[_tile(c), :]
        s = _dot_nt(q, k_ref[0:n, :])
        parts = [s[:, _tile(j)] for j in range(c)]
        if c > MOBA_TOPK:
            gate = _dot_nt(q, km_hi) + _dot_nt(q, km_lo)
            lane = lax.broadcasted_iota(jnp.int32, gate.shape, 1)
            ahead = jnp.zeros(gate.shape, jnp.float32)
            for jp in range(c):
                cj = _lane_col(gate, jp, lane)
                wins = (cj > gate) | ((cj == gate) & (lane > jp))
                ahead = ahead + jnp.where(wins, 1.0, 0.0)
            sel = jnp.where(ahead < MOBA_TOPK, 1.0, 0.0)
            parts = [jnp.where(_lane_col(sel, j, lane) > 0.5, parts[j], NEG_INF)
                     for j in range(c)]
        parts.append(jnp.where(tri, s[:, _tile(c)], NEG_INF))
        o_ref[_tile(c), :] = _softmax_pv(parts, va_ref)


def _mla_moba_attn_kernel(qm_ref, km_ref, vm_ref, q_ref, k_ref, v_ref, om_ref, o_ref,
                          vam_ref, va_ref):
    _moba_attn_kernel(q_ref, k_ref, v_ref, o_ref, va_ref)
    _mla_attn_kernel(qm_ref, km_ref, vm_ref, om_ref, vam_ref)


def _mla_moba_attn(q_mla, k_mla, v_mla, qkv):
    assert MLA_HEADS == MOBA_HEADS
    return pl.pallas_call(
        _mla_moba_attn_kernel,
        grid=(BATCH, MLA_HEADS),
        in_specs=[_head_spec(MLA_QK_PAD, 0), _head_spec(MLA_QK_PAD, 0), _head_spec(MLA_V, 0),
                  _head_spec(HEAD_DIM, 0), _head_spec(HEAD_DIM, MOBA_HEADS),
                  _head_spec(HEAD_DIM, 2 * MOBA_HEADS)],
        out_specs=[_head_spec(MLA_V, 0), _head_spec(HEAD_DIM, 0)],
        out_shape=[jax.ShapeDtypeStruct((TOKENS, MLA_HEADS * MLA_V), jnp.bfloat16),
                   jax.ShapeDtypeStruct((TOKENS, MOBA_HEADS * HEAD_DIM), jnp.bfloat16)],
        scratch_shapes=_V_ONES_SCRATCH + _V_ONES_SCRATCH,
        compiler_params=_cparams(("parallel", "parallel"), ATT_VMEM_MIB),
        name="mla_moba_attn",
    )(q_mla, k_mla, v_mla, qkv, qkv, qkv)


def _dil_tables():
    t = DIL_T
    r = np.arange(t)[:, None]
    c = np.arange(t)[None, :]
    cnts = []
    for d in range(SEQ // t):
        delta = r - c + t * d
        cnts.append(sum(((delta >= 0) & (delta <= w) & (delta % dil == 0)).astype(np.float32)
                        for w, dil in DIL_PATTERNS))
    cnt = np.stack(cnts)
    bias = np.where(cnt > 0, 0.0, NEG_INF).astype(np.float32)
    n_weighted = max(d + 1 for d in range(SEQ // t) if cnt[d].max() > 1)
    return bias, cnt[:n_weighted]


def _dil_attn_kernel(q_ref, k_ref, v_ref, bias_ref, cnt_ref, o_ref, va_ref):
    n_weighted = cnt_ref.shape[0]
    for hh in range(DIL_HPS):
        hs = slice(hh * HEAD_DIM, (hh + 1) * HEAD_DIM)
        va = va_ref.at[hh]
        va[:, :HEAD_DIM] = v_ref[:, hs]
        va[:, HEAD_DIM:] = jnp.ones((SEQ, HEAD_DIM), va_ref.dtype)
    t = DIL_T
    for c in reversed(range(SEQ // t)):
        n = (c + 1) * t
        weights = [cnt_ref[c - j] if c - j < n_weighted else None for j in range(c + 1)]
        for hh in range(DIL_HPS):
            hs = slice(hh * HEAD_DIM, (hh + 1) * HEAD_DIM)
            s = _dot_nt(q_ref[_tile(c, t), hs], k_ref[0:n, hs])
            tiles = [s[:, _tile(j, t)] + bias_ref[c - j] for j in range(c + 1)]
            o_ref[_tile(c, t), hs] = _softmax_pv(tiles, va_ref.at[hh], weights)


def _dil_attn(qkv):
    bias, cnt = _dil_tables()
    return pl.pallas_call(
        _dil_attn_kernel,
        grid=(BATCH, DIL_HEADS // DIL_HPS),
        in_specs=[_head_spec(DIL_HPS * HEAD_DIM, 0),
                  _head_spec(DIL_HPS * HEAD_DIM, DIL_HEADS // DIL_HPS),
                  _head_spec(DIL_HPS * HEAD_DIM, 2 * DIL_HEADS // DIL_HPS),
                  pl.BlockSpec(bias.shape, lambda b, h: (0, 0, 0)),
                  pl.BlockSpec(cnt.shape, lambda b, h: (0, 0, 0))],
        out_specs=_head_spec(DIL_HPS * HEAD_DIM, 0),
        out_shape=jax.ShapeDtypeStruct((TOKENS, DIL_HEADS * HEAD_DIM), jnp.bfloat16),
        scratch_shapes=[pltpu.VMEM((DIL_HPS, SEQ, 2 * HEAD_DIM), jnp.bfloat16)],
        compiler_params=_cparams(("parallel", "parallel"), ATT_VMEM_MIB),
        name="dil_attn",
    )(qkv, qkv, qkv, jnp.asarray(bias), jnp.asarray(cnt))


def _out_proj_kernel(x_ref, mla_ref, moba_ref, dil_ref, w_ref, o_ref, wb_ref):
    @pl.when(pl.program_id(0) == 0)
    def _():
        wb_ref[...] = w_ref[...].astype(jnp.bfloat16)

    mix = jnp.concatenate([mla_ref[...], moba_ref[...], dil_ref[...]], axis=1)
    o_ref[...] = x_ref[...] + jnp.dot(mix, wb_ref[...], preferred_element_type=jnp.float32)


def _out_proj(x, o_mla, o_moba, o_dil, w_out, layer):
    tm = OUT_TM

    def rows(width):
        return pl.BlockSpec((tm, width), lambda i: (i, 0))

    return pl.pallas_call(
        _out_proj_kernel,
        grid=(TOKENS // tm,),
        in_specs=[rows(D_MODEL), rows(MLA_HEADS * MLA_V), rows(MOBA_HEADS * HEAD_DIM),
                  rows(DIL_HEADS * HEAD_DIM),
                  pl.BlockSpec((None, MIX_WIDTH, D_MODEL), lambda i: (layer, 0, 0),
                               pipeline_mode=pl.Buffered(1))],
        out_specs=rows(D_MODEL),
        out_shape=jax.ShapeDtypeStruct((TOKENS, D_MODEL), jnp.float32),
        scratch_shapes=[pltpu.VMEM((MIX_WIDTH, D_MODEL), jnp.bfloat16)],
        compiler_params=_cparams(("arbitrary",), 56),
        name="out_proj",
    )(x, o_mla, o_moba, o_dil, w_out)


def _rope_tables():
    pos = np.arange(SEQ, dtype=np.float64)[:, None]
    inv_h = ROPE_THETA ** (-np.arange(0, HEAD_DIM, 2, dtype=np.float64) / HEAD_DIM)
    ang = pos * inv_h[None, :]
    cos_h = np.concatenate([np.cos(ang), np.cos(ang)], axis=1)
    sin_h = np.concatenate([-np.sin(ang), np.sin(ang)], axis=1)
    scale = HEAD_DIM ** -0.5 * LOG2E
    cos_qk = np.stack([cos_h * scale, cos_h])
    sin_qk = np.stack([sin_h * scale, sin_h])
    inv_r = ROPE_THETA ** (-np.arange(0, MLA_ROPE, 2, dtype=np.float64) / MLA_ROPE)
    ang_r = pos * inv_r[None, :]
    z = np.zeros_like(ang_r)
    cos_r = np.concatenate([np.cos(ang_r), z, np.cos(ang_r), z], axis=1)
    sin_r = np.concatenate([-np.sin(ang_r), z, np.sin(ang_r), z], axis=1)
    return tuple(jnp.asarray(t, jnp.float32) for t in (cos_qk, sin_qk, cos_r, sin_r))


def _pad_rope_cols(w):
    half = MLA_ROPE // 2
    z = jnp.zeros(w.shape[:-1] + (half,), w.dtype)
    return jnp.concatenate([w[..., :half], z, w[..., half:], z], axis=-1)


def _prep_mla_weights(w_uq, w_uk, w_uv):
    bf = jnp.bfloat16
    uq = w_uq.reshape(MLA_Q_RANK, MLA_HEADS, MLA_NOPE + MLA_ROPE)
    uq = jnp.concatenate([uq[..., :MLA_NOPE], _pad_rope_cols(uq[..., MLA_NOPE:])], axis=-1)
    uq = uq.reshape(MLA_Q_RANK, MLA_HEADS * MLA_QK_PAD).astype(bf)
    return uq, w_uk.astype(bf), w_uv.astype(bf)


def kernel(x, ln_ffn1, w_ffn1_gate, w_ffn1_up, w_ffn1_down, ln_mix, w_in, g_mla_q, g_mla_kv,
           w_mla_uq, w_mla_uk, w_mla_uv, w_out, ln_ffn2, w_ffn2_gate, w_ffn2_up, w_ffn2_down,
           ln_final):
    cos_qk, sin_qk, cos_r, sin_r = _rope_tables()
    w_proj = jnp.swapaxes(w_in, 1, 2).astype(jnp.bfloat16)
    xt = x.reshape(TOKENS, D_MODEL)
    for l in range(DEPTH):
        uq, uk, uv = _prep_mla_weights(w_mla_uq[l], w_mla_uk[l], w_mla_uv[l])
        xt = _ffn(xt, ln_ffn1[l][None], w_ffn1_gate, w_ffn1_up, w_ffn1_down, l)
        q_mla, k_mla, v_mla, qkv_moba, qkv_dil = _proj(
            xt, ln_mix[l][None], w_proj, l, cos_qk, sin_qk,
            g_mla_q[l][None], g_mla_kv[l][None], uq, uk, uv, cos_r, sin_r)
        o_mla, o_moba = _mla_moba_attn(q_mla, k_mla, v_mla, qkv_moba)
        o_dil = _dil_attn(qkv_dil)
        xt = _out_proj(xt, o_mla, o_moba, o_dil, w_out, l)
        xt = _ffn(xt, ln_ffn2[l][None], w_ffn2_gate, w_ffn2_up, w_ffn2_down, l,
                  ln_final=ln_final[None] if l == DEPTH - 1 else None)
    return xt.reshape(BATCH, SEQ, D_MODEL)
```

```python
import functools

import numpy as np

import jax
import jax.numpy as jnp
from jax import lax
from jax.experimental import pallas as pl
from jax.experimental.pallas import tpu as pltpu

D_MODEL = 2048
BATCH = 4
SEQ = 2048
DEPTH = 2
TOKENS = BATCH * SEQ

HEAD_DIM = 128
MLA_HEADS = 4
MLA_Q_RANK = 512
MLA_KV_RANK = 256
MLA_NOPE = 128
MLA_ROPE = 64
MLA_V = 128
MLA_QK_PAD = 256
MOBA_HEADS = 4
MOBA_BLOCK = 256
MOBA_TOPK = 3
MOBA_NBLK = SEQ // MOBA_BLOCK
DIL_HEADS = 8
DIL_PATTERNS = ((128, 1), (512, 4), (2048, 16))
D_FF = 5632
ROPE_THETA = 10000.0
NORM_EPS = 1e-6
NEG_INF = -1e30
LOG2E = 1.4426950408889634

MLA_IN = MLA_Q_RANK + MLA_KV_RANK + MLA_ROPE
MOBA_IN = 3 * MOBA_HEADS * HEAD_DIM
DIL_IN = 3 * DIL_HEADS * HEAD_DIM
MIX_WIDTH = MLA_HEADS * MLA_V + MOBA_HEADS * HEAD_DIM + DIL_HEADS * HEAD_DIM

LANES = 128
MIB = 1024 * 1024

FFN_TM = 1024
FFN_TF = 512
FFN_TF_HEAD = 256
SIDE_ROWS = 272
PROJ_TM = 512
PROJ_TN = 512
ATT_T = 256
ATT_NT = SEQ // ATT_T
MLA_T = 512
DIL_T = 256
ATT_ORDER = tuple(reversed(range(ATT_NT)))
ATT_VMEM_MIB = 56
OUT_TM = 512
DIL_HPS = 2

assert ATT_T == MOBA_BLOCK


def _cparams(semantics, vmem_mib):
    return pltpu.CompilerParams(dimension_semantics=semantics,
                                vmem_limit_bytes=vmem_mib * MIB)


def _rms(x, g):
    ms = jnp.mean(x * x, axis=-1, keepdims=True)
    return x * lax.rsqrt(ms + NORM_EPS) * g


def _lane_col(a, j, lane):
    return jnp.sum(jnp.where(lane == j, a, 0.0), axis=1, keepdims=True)


def _dot_nt(a, b):
    return lax.dot_general(a, b, (((1,), (1,)), ((), ())),
                           preferred_element_type=jnp.float32)


def _tile(c, t=ATT_T):
    return slice(c * t, (c + 1) * t)


def _causal_tri(t=ATT_T):
    r = lax.broadcasted_iota(jnp.int32, (t, t), 0)
    c = lax.broadcasted_iota(jnp.int32, (t, t), 1)
    return c <= r


def _softmax_pv(tiles, v_ref, weights=None):
    mx = tiles[0]
    for t in tiles[1:]:
        mx = jnp.maximum(mx, t)
    m = jnp.max(mx, axis=1, keepdims=True)
    res = None
    for j, t in enumerate(tiles):
        e = jnp.exp2(t - m)
        if weights is not None and weights[j] is not None:
            e = e * weights[j]
        pv = jnp.dot(e.astype(jnp.bfloat16), v_ref[_tile(j, t.shape[1]), :],
                     preferred_element_type=jnp.float32)
        res = pv if res is None else res + pv
    half = res.shape[1] // 2
    return (res[:, :half] / res[:, half:]).astype(jnp.bfloat16)


def _head_spec(width, col_offset):
    return pl.BlockSpec((SEQ, width), lambda b, h: (b, col_offset + h))


def _ffn_begin(x_ref, ln_ref, h_ref, o_ref):
    @pl.when(pl.program_id(1) == 0)
    def _():
        x = x_ref[...]
        h_ref[...] = _rms(x, ln_ref[...]).astype(jnp.bfloat16)
        o_ref[...] = x


def _ffn_act(h, wgu):
    tf = wgu.shape[1] // 2
    gu = jnp.dot(h, wgu, preferred_element_type=jnp.float32)
    g = gu[:, :tf]
    u = gu[:, tf:]
    return (g * (1.0 / (1.0 + jnp.exp(-g))) * u * 0.5).astype(jnp.bfloat16)


def _ffn_end(o_ref, lnf_ref):
    if lnf_ref is not None:
        @pl.when(pl.program_id(1) == pl.num_programs(1) - 1)
        def _():
            o_ref[...] = _rms(o_ref[...], lnf_ref[...])


def _ffn_head_kernel(x_ref, ln_ref, wg_ref, wu_ref, wd_ref, *rest, final_norm, side_cast):
    rest = list(rest)
    lnf_ref = rest.pop(0) if final_norm else None
    if side_cast:
        side_ref = rest.pop(0)
        side16_ref = rest.pop(3)
        side16_ref[...] = side_ref[...].astype(side16_ref.dtype)
    o_ref, wgu16_ref, wd16_ref, h_ref = rest
    _ffn_begin(x_ref, ln_ref, h_ref, o_ref)
    wgu = jnp.concatenate([wg_ref[...].astype(jnp.bfloat16),
                           wu_ref[...].astype(jnp.bfloat16)], axis=1)
    wd = wd_ref[...].astype(jnp.bfloat16)
    wgu16_ref[...] = wgu
    wd16_ref[...] = wd
    o_ref[...] += jnp.dot(_ffn_act(h_ref[...], wgu), wd, preferred_element_type=jnp.float32)
    _ffn_end(o_ref, lnf_ref)


def _ffn_rest_kernel(x_ref, ln_ref, wgu_ref, wd_ref, *rest, final_norm):
    rest = list(rest)
    lnf_ref = rest.pop(0) if final_norm else None
    _, o_ref, h_ref = rest
    _ffn_begin(x_ref, ln_ref, h_ref, o_ref)
    h = h_ref[...]
    a = jnp.concatenate([_ffn_act(h, wgu_ref[t]) for t in range(wgu_ref.shape[0])], axis=1)
    o_ref[...] += jnp.dot(a, wd_ref[...], preferred_element_type=jnp.float32)
    _ffn_end(o_ref, lnf_ref)


def _ffn(x, ln, wg, wu, wd, layer, ln_final=None, side=None):
    final_norm = ln_final is not None
    tm = FFN_TM
    bf = jnp.bfloat16
    out_shape = jax.ShapeDtypeStruct((TOKENS, D_MODEL), jnp.float32)
    vec_spec = pl.BlockSpec((1, D_MODEL), lambda i, j: (0, 0))
    tail_specs = [vec_spec] if final_norm else []
    tail_args = [ln_final] if final_norm else []
    suffix = "_final" if final_norm else ""

    th = FFN_TF_HEAD
    n_head_tiles = D_FF // th
    group = FFN_TF // th
    scratch = [pltpu.VMEM((tm, D_MODEL), bf)]
    cparams = _cparams(("parallel", "arbitrary"), 60)

    side_specs, side_args, side_out_specs, side_out_shapes = [], [], [], []
    if side is not None:
        _, rows, cols = side.shape
        n_side = rows // SIDE_ROWS
        assert rows % SIDE_ROWS == 0 and n_side <= n_head_tiles
        side_specs = [pl.BlockSpec((None, SIDE_ROWS, cols),
                                   lambda i, j: (layer, jnp.minimum(j, n_side - 1), 0))]
        side_args = [side]
        side_out_specs = [pl.BlockSpec((SIDE_ROWS, cols),
                                       lambda i, j: (jnp.minimum(j, n_side - 1), 0))]
        side_out_shapes = [jax.ShapeDtypeStruct((rows, cols), bf)]

    head_out = pl.pallas_call(
        functools.partial(_ffn_head_kernel, final_norm=final_norm, side_cast=side is not None),
        grid=(1, n_head_tiles),
        in_specs=[pl.BlockSpec((tm, D_MODEL), lambda i, j: (0, 0), pipeline_mode=pl.Buffered(1)),
                  vec_spec,
                  pl.BlockSpec((None, D_MODEL, th), lambda i, j: (layer, 0, j)),
                  pl.BlockSpec((None, D_MODEL, th), lambda i, j: (layer, 0, j)),
                  pl.BlockSpec((None, th, D_MODEL), lambda i, j: (layer, j, 0))]
        + tail_specs + side_specs,
        out_specs=[pl.BlockSpec((tm, D_MODEL), lambda i, j: (0, 0)),
                   pl.BlockSpec((None, D_MODEL, 2 * th), lambda i, j: (j, 0, 0)),
                   pl.BlockSpec((th, D_MODEL), lambda i, j: (j, 0))] + side_out_specs,
        out_shape=[out_shape, jax.ShapeDtypeStruct((n_head_tiles, D_MODEL, 2 * th), bf),
                   jax.ShapeDtypeStruct((D_FF, D_MODEL), bf)] + side_out_shapes,
        scratch_shapes=scratch,
        compiler_params=cparams,
        name="ffn_head" + suffix,
    )(x, ln, wg, wu, wd, *tail_args, *side_args)
    y, wgu16, wd16 = head_out[:3]

    alias_index = 4 + len(tail_args)
    out = pl.pallas_call(
        functools.partial(_ffn_rest_kernel, final_norm=final_norm),
        grid=(TOKENS // tm - 1, n_head_tiles // group),
        in_specs=[pl.BlockSpec((tm, D_MODEL), lambda i, j: (i + 1, 0)),
                  vec_spec,
                  pl.BlockSpec((group, D_MODEL, 2 * th), lambda i, j: (j, 0, 0)),
                  pl.BlockSpec((group * th, D_MODEL), lambda i, j: (j, 0))] + tail_specs
        + [pl.BlockSpec(memory_space=pl.ANY)],
        out_specs=pl.BlockSpec((tm, D_MODEL), lambda i, j: (i + 1, 0)),
        out_shape=out_shape,
        scratch_shapes=scratch,
        input_output_aliases={alias_index: 0},
        compiler_params=cparams,
        name="ffn_rest" + suffix,
    )(x, ln, wgu16, wd16, *tail_args, y)
    return out if side is None else (out, head_out[3])


PROJ_WIDTH = MLA_IN + MOBA_IN + DIL_IN


def _mla_prep(lat, kr, gq_ref, gkv_ref, wuq_ref, wuk_ref, wuv_ref, cos_ref, sin_ref,
              q_ref, k_ref, v_ref):
    scale = (MLA_NOPE + MLA_ROPE) ** -0.5 * LOG2E
    c = cos_ref[...]
    s = sin_ref[...]

    def rope(t):
        return t * c + pltpu.roll(t, LANES // 2, 1) * s

    nq = MLA_Q_RANK // LANES
    cq = _rms(jnp.concatenate(lat[:nq], axis=1), gq_ref[...]).astype(jnp.bfloat16)
    q = jnp.dot(cq, wuq_ref[...], preferred_element_type=jnp.float32)
    ckv = _rms(jnp.concatenate(lat[nq:], axis=1), gkv_ref[...]).astype(jnp.bfloat16)
    kn = jnp.dot(ckv, wuk_ref[...], preferred_element_type=jnp.float32)
    v_ref[...] = jnp.dot(ckv, wuv_ref[...],
                         preferred_element_type=jnp.float32).astype(jnp.bfloat16)
    kr = rope(kr).astype(jnp.bfloat16)
    for h in range(MLA_HEADS):
        b0 = h * MLA_QK_PAD
        q_ref[:, b0:b0 + LANES] = (q[:, b0:b0 + LANES] * scale).astype(jnp.bfloat16)
        q_ref[:, b0 + LANES:b0 + 2 * LANES] = (
            rope(q[:, b0 + LANES:b0 + 2 * LANES]) * scale).astype(jnp.bfloat16)
        k_ref[:, b0:b0 + LANES] = kn[:, h * LANES:(h + 1) * LANES].astype(jnp.bfloat16)
        k_ref[:, b0 + LANES:b0 + 2 * LANES] = kr


def _proj_kernel(x_ref, ln_ref, wt_ref, cos_ref, sin_ref, *rest):
    mla_args, (moba_ref, dil_ref) = rest[:-2], rest[-2:]
    h = _rms(x_ref[...], ln_ref[...]).astype(jnp.bfloat16)

    def proj(r0, rows):
        return _dot_nt(h, wt_ref[r0:r0 + rows, :])

    n_lat = MLA_Q_RANK + MLA_KV_RANK
    lat = []
    for r0 in range(0, n_lat, PROJ_TN):
        y = proj(r0, min(PROJ_TN, n_lat - r0))
        lat += [y[:, g0:g0 + LANES] for g0 in range(0, y.shape[1], LANES)]
    yk = proj(n_lat, LANES)
    lane = lax.broadcasted_iota(jnp.int32, yk.shape, 1)
    half = MLA_ROPE // 2
    kr = (jnp.where(lane < half, yk, 0.0)
          + jnp.where((lane >= LANES // 2) & (lane < LANES // 2 + half),
                      pltpu.roll(yk, half, 1), 0.0))
    _mla_prep(lat, kr, *mla_args)

    row = MLA_IN
    for o_ref, width in ((moba_ref, MOBA_IN), (dil_ref, DIL_IN)):
        for c0 in range(0, width, PROJ_TN):
            y = proj(row + c0, PROJ_TN)
            section = c0 // (width // 3)
            if section == 2:
                o_ref[:, c0:c0 + PROJ_TN] = y.astype(o_ref.dtype)
                continue
            c = cos_ref[section]
            s = sin_ref[section]
            for g0 in range(0, PROJ_TN, LANES):
                yg = y[:, g0:g0 + LANES]
                o_ref[:, c0 + g0:c0 + g0 + LANES] = (
                    yg * c + pltpu.roll(yg, LANES // 2, 1) * s).astype(o_ref.dtype)
        row += width


def _proj(x, ln, w, cos_tab, sin_tab, gq, gkv, wuq, wuk, wuv, cos_r, sin_r):
    tm = PROJ_TM
    pos_blocks = SEQ // tm
    qk_w = MLA_HEADS * MLA_QK_PAD
    v_w = MLA_HEADS * MLA_V
    bf = jnp.bfloat16

    def rows(width):
        return pl.BlockSpec((tm, width), lambda i: (i, 0))

    def full(shape):
        return pl.BlockSpec(shape, lambda i: (0, 0))

    def tab_spec():
        return pl.BlockSpec((2, tm, LANES), lambda i: (0, i % pos_blocks, 0))

    def rtab_spec():
        return pl.BlockSpec((tm, LANES), lambda i: (i % pos_blocks, 0))

    return pl.pallas_call(
        _proj_kernel,
        grid=(TOKENS // tm,),
        in_specs=[
            rows(D_MODEL),
            full((1, D_MODEL)),
            pl.BlockSpec((PROJ_WIDTH, D_MODEL), lambda i: (0, 0), pipeline_mode=pl.Buffered(1)),
            tab_spec(), tab_spec(),
            full((1, MLA_Q_RANK)), full((1, MLA_KV_RANK)),
            full((MLA_Q_RANK, qk_w)), full((MLA_KV_RANK, v_w)), full((MLA_KV_RANK, v_w)),
            rtab_spec(), rtab_spec(),
        ],
        out_specs=[rows(qk_w), rows(qk_w), rows(v_w), rows(MOBA_IN), rows(DIL_IN)],
        out_shape=[
            jax.ShapeDtypeStruct((TOKENS, qk_w), bf),
            jax.ShapeDtypeStruct((TOKENS, qk_w), bf),
            jax.ShapeDtypeStruct((TOKENS, v_w), bf),
            jax.ShapeDtypeStruct((TOKENS, MOBA_IN), bf),
            jax.ShapeDtypeStruct((TOKENS, DIL_IN), bf),
        ],
        compiler_params=_cparams(("parallel",), 56),
        name="mix_proj",
    )(x, ln, w, cos_tab, sin_tab, gq, gkv, wuq, wuk, wuv, cos_r, sin_r)


def _fill_v_ones(v_ref, va_ref):
    width = v_ref.shape[1]
    va_ref[:, :width] = v_ref[...]
    va_ref[:, width:] = jnp.ones((v_ref.shape[0], va_ref.shape[1] - width), va_ref.dtype)


_V_ONES_SCRATCH = [pltpu.VMEM((SEQ, 2 * HEAD_DIM), jnp.bfloat16)]


def _mla_attn_kernel(q_ref, k_ref, v_ref, o_ref, va_ref):
    _fill_v_ones(v_ref, va_ref)
    t = MLA_T
    tri = _causal_tri(t)
    for c in reversed(range(SEQ // t)):
        n = (c + 1) * t
        s = _dot_nt(q_ref[_tile(c, t), :], k_ref[0:n, :])
        parts = [s[:, _tile(j, t)] for j in range(c)]
        parts.append(jnp.where(tri, s[:, _tile(c, t)], NEG_INF))
        o_ref[_tile(c, t), :] = _softmax_pv(parts, va_ref)


def _moba_attn_kernel(q_ref, k_ref, v_ref, o_ref, va_ref):
    _fill_v_ones(v_ref, va_ref)
    rid = lax.broadcasted_iota(jnp.int32, (LANES, HEAD_DIM), 0)
    km = jnp.zeros((LANES, HEAD_DIM), jnp.float32)
    for j in range(MOBA_NBLK):
        kj = k_ref[_tile(j), :].astype(jnp.float32)
        mean_j = jnp.sum(kj, axis=0, keepdims=True) * (1.0 / MOBA_BLOCK)
        km = jnp.where(rid == j, mean_j, km)
    km_hi = km.astype(jnp.bfloat16)
    km_lo = (km - km_hi.astype(jnp.float32)).astype(jnp.bfloat16)
    tri = _causal_tri()

    for c in ATT_ORDER:
        n = (c + 1) * ATT_T
        q = q_ref[_tile(c), :]
        s = _dot_nt(q, k_ref[0:n, :])
        parts = [s[:, _tile(j)] for j in range(c)]
        if c > MOBA_TOPK:
            gate = _dot_nt(q, km_hi) + _dot_nt(q, km_lo)
            lane = lax.broadcasted_iota(jnp.int32, gate.shape, 1)
            ahead = jnp.zeros(gate.shape, jnp.float32)
            for jp in range(c):
                cj = _lane_col(gate, jp, lane)
                wins = (cj > gate) | ((cj == gate) & (lane > jp))
                ahead = ahead + jnp.where(wins, 1.0, 0.0)
            sel = jnp.where(ahead < MOBA_TOPK, 1.0, 0.0)
            parts = [jnp.where(_lane_col(sel, j, lane) > 0.5, parts[j], NEG_INF)
                     for j in range(c)]
        parts.append(jnp.where(tri, s[:, _tile(c)], NEG_INF))
        o_ref[_tile(c), :] = _softmax_pv(parts, va_ref)


def _mla_moba_attn_kernel(qm_ref, km_ref, vm_ref, q_ref, k_ref, v_ref, om_ref, o_ref,
                          vam_ref, va_ref):
    _moba_attn_kernel(q_ref, k_ref, v_ref, o_ref, va_ref)
    _mla_attn_kernel(qm_ref, km_ref, vm_ref, om_ref, vam_ref)


def _mla_moba_attn(q_mla, k_mla, v_mla, qkv):
    assert MLA_HEADS == MOBA_HEADS
    return pl.pallas_call(
        _mla_moba_attn_kernel,
        grid=(BATCH, MLA_HEADS),
        in_specs=[_head_spec(MLA_QK_PAD, 0), _head_spec(MLA_QK_PAD, 0), _head_spec(MLA_V, 0),
                  _head_spec(HEAD_DIM, 0), _head_spec(HEAD_DIM, MOBA_HEADS),
                  _head_spec(HEAD_DIM, 2 * MOBA_HEADS)],
        out_specs=[_head_spec(MLA_V, 0), _head_spec(HEAD_DIM, 0)],
        out_shape=[jax.ShapeDtypeStruct((TOKENS, MLA_HEADS * MLA_V), jnp.bfloat16),
                   jax.ShapeDtypeStruct((TOKENS, MOBA_HEADS * HEAD_DIM), jnp.bfloat16)],
        scratch_shapes=_V_ONES_SCRATCH + _V_ONES_SCRATCH,
        compiler_params=_cparams(("parallel", "parallel"), ATT_VMEM_MIB),
        name="mla_moba_attn",
    )(q_mla, k_mla, v_mla, qkv, qkv, qkv)


def _dil_tables():
    t = DIL_T
    r = np.arange(t)[:, None]
    c = np.arange(t)[None, :]
    cnts = []
    for d in range(SEQ // t):
        delta = r - c + t * d
        cnts.append(sum(((delta >= 0) & (delta <= w) & (delta % dil == 0)).astype(np.float32)
                        for w, dil in DIL_PATTERNS))
    cnt = np.stack(cnts)
    bias = np.where(cnt > 0, 0.0, NEG_INF).astype(np.float32)
    n_weighted = max(d + 1 for d in range(SEQ // t) if cnt[d].max() > 1)
    return bias, cnt[:n_weighted]


def _dil_attn_kernel(q_ref, k_ref, v_ref, bias_ref, cnt_ref, o_ref, va_ref):
    n_weighted = cnt_ref.shape[0]
    for hh in range(DIL_HPS):
        hs = slice(hh * HEAD_DIM, (hh + 1) * HEAD_DIM)
        va = va_ref.at[hh]
        va[:, :HEAD_DIM] = v_ref[:, hs]
        va[:, HEAD_DIM:] = jnp.ones((SEQ, HEAD_DIM), va_ref.dtype)
    t = DIL_T
    for c in reversed(range(SEQ // t)):
        n = (c + 1) * t
        weights = [cnt_ref[c - j] if c - j < n_weighted else None for j in range(c + 1)]
        for hh in range(DIL_HPS):
            hs = slice(hh * HEAD_DIM, (hh + 1) * HEAD_DIM)
            s = _dot_nt(q_ref[_tile(c, t), hs], k_ref[0:n, hs])
            tiles = [s[:, _tile(j, t)] + bias_ref[c - j] for j in range(c + 1)]
            o_ref[_tile(c, t), hs] = _softmax_pv(tiles, va_ref.at[hh], weights)


def _dil_attn(qkv):
    bias, cnt = _dil_tables()
    return pl.pallas_call(
        _dil_attn_kernel,
        grid=(BATCH, DIL_HEADS // DIL_HPS),
        in_specs=[_head_spec(DIL_HPS * HEAD_DIM, 0),
                  _head_spec(DIL_HPS * HEAD_DIM, DIL_HEADS // DIL_HPS),
                  _head_spec(DIL_HPS * HEAD_DIM, 2 * DIL_HEADS // DIL_HPS),
                  pl.BlockSpec(bias.shape, lambda b, h: (0, 0, 0)),
                  pl.BlockSpec(cnt.shape, lambda b, h: (0, 0, 0))],
        out_specs=_head_spec(DIL_HPS * HEAD_DIM, 0),
        out_shape=jax.ShapeDtypeStruct((TOKENS, DIL_HEADS * HEAD_DIM), jnp.bfloat16),
        scratch_shapes=[pltpu.VMEM((DIL_HPS, SEQ, 2 * HEAD_DIM), jnp.bfloat16)],
        compiler_params=_cparams(("parallel", "parallel"), ATT_VMEM_MIB),
        name="dil_attn",
    )(qkv, qkv, qkv, jnp.asarray(bias), jnp.asarray(cnt))


def _out_proj_kernel(x_ref, mla_ref, moba_ref, dil_ref, w_ref, o_ref, wb_ref):
    @pl.when(pl.program_id(0) == 0)
    def _():
        wb_ref[...] = w_ref[...].astype(jnp.bfloat16)

    mix = jnp.concatenate([mla_ref[...], moba_ref[...], dil_ref[...]], axis=1)
    o_ref[...] = x_ref[...] + jnp.dot(mix, wb_ref[...], preferred_element_type=jnp.float32)


def _out_proj(x, o_mla, o_moba, o_dil, w_out, layer):
    tm = OUT_TM

    def rows(width):
        return pl.BlockSpec((tm, width), lambda i: (i, 0))

    return pl.pallas_call(
        _out_proj_kernel,
        grid=(TOKENS // tm,),
        in_specs=[rows(D_MODEL), rows(MLA_HEADS * MLA_V), rows(MOBA_HEADS * HEAD_DIM),
                  rows(DIL_HEADS * HEAD_DIM),
                  pl.BlockSpec((None, MIX_WIDTH, D_MODEL), lambda i: (layer, 0, 0),
                               pipeline_mode=pl.Buffered(1))],
        out_specs=rows(D_MODEL),
        out_shape=jax.ShapeDtypeStruct((TOKENS, D_MODEL), jnp.float32),
        scratch_shapes=[pltpu.VMEM((MIX_WIDTH, D_MODEL), jnp.bfloat16)],
        compiler_params=_cparams(("arbitrary",), 56),
        name="out_proj",
    )(x, o_mla, o_moba, o_dil, w_out)


def _rope_tables():
    pos = np.arange(SEQ, dtype=np.float64)[:, None]
    inv_h = ROPE_THETA ** (-np.arange(0, HEAD_DIM, 2, dtype=np.float64) / HEAD_DIM)
    ang = pos * inv_h[None, :]
    cos_h = np.concatenate([np.cos(ang), np.cos(ang)], axis=1)
    sin_h = np.concatenate([-np.sin(ang), np.sin(ang)], axis=1)
    scale = HEAD_DIM ** -0.5 * LOG2E
    cos_qk = np.stack([cos_h * scale, cos_h])
    sin_qk = np.stack([sin_h * scale, sin_h])
    inv_r = ROPE_THETA ** (-np.arange(0, MLA_ROPE, 2, dtype=np.float64) / MLA_ROPE)
    ang_r = pos * inv_r[None, :]
    z = np.zeros_like(ang_r)
    cos_r = np.concatenate([np.cos(ang_r), z, np.cos(ang_r), z], axis=1)
    sin_r = np.concatenate([-np.sin(ang_r), z, np.sin(ang_r), z], axis=1)
    return tuple(jnp.asarray(t, jnp.float32) for t in (cos_qk, sin_qk, cos_r, sin_r))


def _pad_rope_cols(w):
    half = MLA_ROPE // 2
    z = jnp.zeros(w.shape[:-1] + (half,), w.dtype)
    return jnp.concatenate([w[..., :half], z, w[..., half:], z], axis=-1)


def _prep_mla_weights(w_uq, w_uk, w_uv):
    bf = jnp.bfloat16
    uq = w_uq.reshape(MLA_Q_RANK, MLA_HEADS, MLA_NOPE + MLA_ROPE)
    uq = jnp.concatenate([uq[..., :MLA_NOPE], _pad_rope_cols(uq[..., MLA_NOPE:])], axis=-1)
    uq = uq.reshape(MLA_Q_RANK, MLA_HEADS * MLA_QK_PAD).astype(bf)
    return uq, w_uk.astype(bf), w_uv.astype(bf)


def kernel(x, ln_ffn1, w_ffn1_gate, w_ffn1_up, w_ffn1_down, ln_mix, w_in, g_mla_q, g_mla_kv,
           w_mla_uq, w_mla_uk, w_mla_uv, w_out, ln_ffn2, w_ffn2_gate, w_ffn2_up, w_ffn2_down,
           ln_final):
    cos_qk, sin_qk, cos_r, sin_r = _rope_tables()
    w_in_t = jnp.swapaxes(w_in, 1, 2)
    xt = x.reshape(TOKENS, D_MODEL)
    for l in range(DEPTH):
        uq, uk, uv = _prep_mla_weights(w_mla_uq[l], w_mla_uk[l], w_mla_uv[l])
        xt, w_proj = _ffn(xt, ln_ffn1[l][None], w_ffn1_gate, w_ffn1_up, w_ffn1_down, l,
                          side=w_in_t)
        q_mla, k_mla, v_mla, qkv_moba, qkv_dil = _proj(
            xt, ln_mix[l][None], w_proj, cos_qk, sin_qk,
            g_mla_q[l][None], g_mla_kv[l][None], uq, uk, uv, cos_r, sin_r)
        o_mla, o_moba = _mla_moba_attn(q_mla, k_mla, v_mla, qkv_moba)
        o_dil = _dil_attn(qkv_dil)
        xt = _out_proj(xt, o_mla, o_moba, o_dil, w_out, l)
        xt = _ffn(xt, ln_ffn2[l][None], w_ffn2_gate, w_ffn2_up, w_ffn2_down, l,
                  ln_final=ln_final[None] if l == DEPTH - 1 else None)
    return xt.reshape(BATCH, SEQ, D_MODEL)
```

```python
import functools

import numpy as np

import jax
import jax.numpy as jnp
from jax import lax
from jax.experimental import pallas as pl
from jax.experimental.pallas import tpu as pltpu

D_MODEL = 2048
BATCH = 4
SEQ = 2048
DEPTH = 2
TOKENS = BATCH * SEQ

HEAD_DIM = 128
MLA_HEADS = 4
MLA_Q_RANK = 512
MLA_KV_RANK = 256
MLA_NOPE = 128
MLA_ROPE = 64
MLA_V = 128
MLA_QK_PAD = 256
MOBA_HEADS = 4
MOBA_BLOCK = 256
MOBA_TOPK = 3
MOBA_NBLK = SEQ // MOBA_BLOCK
DIL_HEADS = 8
DIL_PATTERNS = ((128, 1), (512, 4), (2048, 16))
D_FF = 5632
ROPE_THETA = 10000.0
NORM_EPS = 1e-6
NEG_INF = -1e30
LOG2E = 1.4426950408889634

MLA_IN = MLA_Q_RANK + MLA_KV_RANK + MLA_ROPE
MOBA_IN = 3 * MOBA_HEADS * HEAD_DIM
DIL_IN = 3 * DIL_HEADS * HEAD_DIM
MIX_WIDTH = MLA_HEADS * MLA_V + MOBA_HEADS * HEAD_DIM + DIL_HEADS * HEAD_DIM

LANES = 128
MIB = 1024 * 1024
FFN_VMEM_MIB = 60
PROJ_VMEM_MIB = 56

FFN_TM = 1024
FFN_TF = 512
FFN_TF_HEAD = 256
PROJ_TM = 512
PROJ_TN = 512
ATT_T = 256
ATT_NT = SEQ // ATT_T
MLA_T = 512
DIL_T = 256
ATT_ORDER = tuple(reversed(range(ATT_NT)))
ATT_VMEM_MIB = 56
OUT_TM = 512
DIL_HPS = 2

assert ATT_T == MOBA_BLOCK


def _cparams(semantics, vmem_mib):
    return pltpu.CompilerParams(dimension_semantics=semantics,
                                vmem_limit_bytes=vmem_mib * MIB)


def _rms(x, g):
    ms = jnp.mean(x * x, axis=-1, keepdims=True)
    return x * lax.rsqrt(ms + NORM_EPS) * g


def _lane_col(a, j, lane):
    return jnp.sum(jnp.where(lane == j, a, 0.0), axis=1, keepdims=True)


def _dot_nt(a, b):
    return lax.dot_general(a, b, (((1,), (1,)), ((), ())),
                           preferred_element_type=jnp.float32)


def _tile(c, t=ATT_T):
    return slice(c * t, (c + 1) * t)


def _causal_tri(t=ATT_T):
    r = lax.broadcasted_iota(jnp.int32, (t, t), 0)
    c = lax.broadcasted_iota(jnp.int32, (t, t), 1)
    return c <= r


def _softmax_pv(tiles, v_ref, weights=None):
    mx = tiles[0]
    for t in tiles[1:]:
        mx = jnp.maximum(mx, t)
    m = jnp.max(mx, axis=1, keepdims=True)
    res = None
    for j, t in enumerate(tiles):
        e = jnp.exp2(t - m)
        if weights is not None and weights[j] is not None:
            e = e * weights[j]
        pv = jnp.dot(e.astype(jnp.bfloat16), v_ref[_tile(j, t.shape[1]), :],
                     preferred_element_type=jnp.float32)
        res = pv if res is None else res + pv
    half = res.shape[1] // 2
    return (res[:, :half] / res[:, half:]).astype(jnp.bfloat16)


def _head_spec(width, col_offset):
    return pl.BlockSpec((SEQ, width), lambda b, h: (b, col_offset + h))


def _ffn_begin(x_ref, ln_ref, h_ref, o_ref):
    @pl.when(pl.program_id(1) == 0)
    def _():
        x = x_ref[...]
        h_ref[...] = _rms(x, ln_ref[...]).astype(jnp.bfloat16)
        o_ref[...] = x


def _ffn_act(h, wgu):
    tf = wgu.shape[1] // 2
    gu = jnp.dot(h, wgu, preferred_element_type=jnp.float32)
    g = gu[:, :tf]
    u = gu[:, tf:]
    return (g * (1.0 / (1.0 + jnp.exp(-g))) * u * 0.5).astype(jnp.bfloat16)


def _ffn_end(o_ref, lnf_ref):
    if lnf_ref is not None:
        @pl.when(pl.program_id(1) == pl.num_programs(1) - 1)
        def _():
            o_ref[...] = _rms(o_ref[...], lnf_ref[...])


def _ffn_head_kernel(x_ref, ln_ref, wg_ref, wu_ref, wd_ref, *rest, final_norm):
    rest = list(rest)
    lnf_ref = rest.pop(0) if final_norm else None
    o_ref, wgu16_ref, wd16_ref, h_ref = rest
    _ffn_begin(x_ref, ln_ref, h_ref, o_ref)
    wgu = jnp.concatenate([wg_ref[...].astype(jnp.bfloat16),
                           wu_ref[...].astype(jnp.bfloat16)], axis=1)
    wd = wd_ref[...].astype(jnp.bfloat16)
    wgu16_ref[...] = wgu
    wd16_ref[...] = wd
    o_ref[...] += jnp.dot(_ffn_act(h_ref[...], wgu), wd, preferred_element_type=jnp.float32)
    _ffn_end(o_ref, lnf_ref)


def _ffn_rest_kernel(x_ref, ln_ref, wgu_ref, wd_ref, *rest, final_norm):
    rest = list(rest)
    lnf_ref = rest.pop(0) if final_norm else None
    _, o_ref, h_ref = rest
    _ffn_begin(x_ref, ln_ref, h_ref, o_ref)
    h = h_ref[...]
    a = jnp.concatenate([_ffn_act(h, wgu_ref[t]) for t in range(wgu_ref.shape[0])], axis=1)
    o_ref[...] += jnp.dot(a, wd_ref[...], preferred_element_type=jnp.float32)
    _ffn_end(o_ref, lnf_ref)


def _ffn(x, ln, wg, wu, wd, layer, ln_final=None):
    final_norm = ln_final is not None
    tm = FFN_TM
    bf = jnp.bfloat16
    out_shape = jax.ShapeDtypeStruct((TOKENS, D_MODEL), jnp.float32)
    vec_spec = pl.BlockSpec((1, D_MODEL), lambda i, j: (0, 0))
    tail_specs = [vec_spec] if final_norm else []
    tail_args = [ln_final] if final_norm else []
    suffix = "_final" if final_norm else ""

    th = FFN_TF_HEAD
    n_head_tiles = D_FF // th
    group = FFN_TF // th
    scratch = [pltpu.VMEM((tm, D_MODEL), bf)]
    cparams = _cparams(("parallel", "arbitrary"), FFN_VMEM_MIB)

    y, wgu16, wd16 = pl.pallas_call(
        functools.partial(_ffn_head_kernel, final_norm=final_norm),
        grid=(1, n_head_tiles),
        in_specs=[pl.BlockSpec((tm, D_MODEL), lambda i, j: (0, 0), pipeline_mode=pl.Buffered(1)),
                  vec_spec,
                  pl.BlockSpec((None, D_MODEL, th), lambda i, j: (layer, 0, j)),
                  pl.BlockSpec((None, D_MODEL, th), lambda i, j: (layer, 0, j)),
                  pl.BlockSpec((None, th, D_MODEL), lambda i, j: (layer, j, 0))] + tail_specs,
        out_specs=[pl.BlockSpec((tm, D_MODEL), lambda i, j: (0, 0)),
                   pl.BlockSpec((None, D_MODEL, 2 * th), lambda i, j: (j, 0, 0)),
                   pl.BlockSpec((th, D_MODEL), lambda i, j: (j, 0))],
        out_shape=[out_shape, jax.ShapeDtypeStruct((n_head_tiles, D_MODEL, 2 * th), bf),
                   jax.ShapeDtypeStruct((D_FF, D_MODEL), bf)],
        scratch_shapes=scratch,
        compiler_params=cparams,
        name="ffn_head" + suffix,
    )(x, ln, wg, wu, wd, *tail_args)

    alias_index = 4 + len(tail_args)
    return pl.pallas_call(
        functools.partial(_ffn_rest_kernel, final_norm=final_norm),
        grid=(TOKENS // tm - 1, n_head_tiles // group),
        in_specs=[pl.BlockSpec((tm, D_MODEL), lambda i, j: (i + 1, 0)),
                  vec_spec,
                  pl.BlockSpec((group, D_MODEL, 2 * th), lambda i, j: (j, 0, 0)),
                  pl.BlockSpec((group * th, D_MODEL), lambda i, j: (j, 0))] + tail_specs
        + [pl.BlockSpec(memory_space=pl.ANY)],
        out_specs=pl.BlockSpec((tm, D_MODEL), lambda i, j: (i + 1, 0)),
        out_shape=out_shape,
        scratch_shapes=scratch,
        input_output_aliases={alias_index: 0},
        compiler_params=cparams,
        name="ffn_rest" + suffix,
    )(x, ln, wgu16, wd16, *tail_args, y)


PROJ_WIDTH = MLA_IN + MOBA_IN + DIL_IN


def _mla_prep(lat, kr, gq_ref, gkv_ref, wuq_ref, wuk_ref, wuv_ref, cos_ref, sin_ref,
              q_ref, k_ref, v_ref):
    scale = (MLA_NOPE + MLA_ROPE) ** -0.5 * LOG2E
    c = cos_ref[...]
    s = sin_ref[...]

    def rope(t):
        return t * c + pltpu.roll(t, LANES // 2, 1) * s

    nq = MLA_Q_RANK // LANES
    cq = _rms(jnp.concatenate(lat[:nq], axis=1), gq_ref[...]).astype(jnp.bfloat16)
    q = jnp.dot(cq, wuq_ref[...], preferred_element_type=jnp.float32)
    ckv = _rms(jnp.concatenate(lat[nq:], axis=1), gkv_ref[...]).astype(jnp.bfloat16)
    kn = jnp.dot(ckv, wuk_ref[...], preferred_element_type=jnp.float32)
    v_ref[...] = jnp.dot(ckv, wuv_ref[...],
                         preferred_element_type=jnp.float32).astype(jnp.bfloat16)
    kr = rope(kr).astype(jnp.bfloat16)
    for h in range(MLA_HEADS):
        b0 = h * MLA_QK_PAD
        q_ref[:, b0:b0 + LANES] = (q[:, b0:b0 + LANES] * scale).astype(jnp.bfloat16)
        q_ref[:, b0 + LANES:b0 + 2 * LANES] = (
            rope(q[:, b0 + LANES:b0 + 2 * LANES]) * scale).astype(jnp.bfloat16)
        k_ref[:, b0:b0 + LANES] = kn[:, h * LANES:(h + 1) * LANES].astype(jnp.bfloat16)
        k_ref[:, b0 + LANES:b0 + 2 * LANES] = kr


def _proj_kernel(x_ref, ln_ref, wt_ref, cos_ref, sin_ref, *rest):
    mla_args, (moba_ref, dil_ref) = rest[:-2], rest[-2:]
    h = _rms(x_ref[...], ln_ref[...]).astype(jnp.bfloat16)

    def proj(r0, rows):
        return _dot_nt(h, wt_ref[r0:r0 + rows, :])

    n_lat = MLA_Q_RANK + MLA_KV_RANK
    lat = []
    for r0 in range(0, n_lat, PROJ_TN):
        y = proj(r0, min(PROJ_TN, n_lat - r0))
        lat += [y[:, g0:g0 + LANES] for g0 in range(0, y.shape[1], LANES)]
    yk = proj(n_lat, LANES)
    lane = lax.broadcasted_iota(jnp.int32, yk.shape, 1)
    half = MLA_ROPE // 2
    kr = (jnp.where(lane < half, yk, 0.0)
          + jnp.where((lane >= LANES // 2) & (lane < LANES // 2 + half),
                      pltpu.roll(yk, half, 1), 0.0))
    _mla_prep(lat, kr, *mla_args)

    row = MLA_IN
    for o_ref, width in ((moba_ref, MOBA_IN), (dil_ref, DIL_IN)):
        for c0 in range(0, width, PROJ_TN):
            y = proj(row + c0, PROJ_TN)
            section = c0 // (width // 3)
            if section == 2:
                o_ref[:, c0:c0 + PROJ_TN] = y.astype(o_ref.dtype)
                continue
            c = cos_ref[section]
            s = sin_ref[section]
            for g0 in range(0, PROJ_TN, LANES):
                yg = y[:, g0:g0 + LANES]
                o_ref[:, c0 + g0:c0 + g0 + LANES] = (
                    yg * c + pltpu.roll(yg, LANES // 2, 1) * s).astype(o_ref.dtype)
        row += width


def _proj(x, ln, w, layer, cos_tab, sin_tab, gq, gkv, wuq, wuk, wuv, cos_r, sin_r):
    tm = PROJ_TM
    pos_blocks = SEQ // tm
    qk_w = MLA_HEADS * MLA_QK_PAD
    v_w = MLA_HEADS * MLA_V
    bf = jnp.bfloat16

    def rows(width):
        return pl.BlockSpec((tm, width), lambda i: (i, 0))

    def full(shape):
        return pl.BlockSpec(shape, lambda i: (0, 0))

    def tab_spec():
        return pl.BlockSpec((2, tm, LANES), lambda i: (0, i % pos_blocks, 0))

    def rtab_spec():
        return pl.BlockSpec((tm, LANES), lambda i: (i % pos_blocks, 0))

    return pl.pallas_call(
        _proj_kernel,
        grid=(TOKENS // tm,),
        in_specs=[
            rows(D_MODEL),
            full((1, D_MODEL)),
            pl.BlockSpec((None, PROJ_WIDTH, D_MODEL), lambda i: (layer, 0, 0),
                         pipeline_mode=pl.Buffered(1)),
            tab_spec(), tab_spec(),
            full((1, MLA_Q_RANK)), full((1, MLA_KV_RANK)),
            full((MLA_Q_RANK, qk_w)), full((MLA_KV_RANK, v_w)), full((MLA_KV_RANK, v_w)),
            rtab_spec(), rtab_spec(),
        ],
        out_specs=[rows(qk_w), rows(qk_w), rows(v_w), rows(MOBA_IN), rows(DIL_IN)],
        out_shape=[
            jax.ShapeDtypeStruct((TOKENS, qk_w), bf),
            jax.ShapeDtypeStruct((TOKENS, qk_w), bf),
            jax.ShapeDtypeStruct((TOKENS, v_w), bf),
            jax.ShapeDtypeStruct((TOKENS, MOBA_IN), bf),
            jax.ShapeDtypeStruct((TOKENS, DIL_IN), bf),
        ],
        compiler_params=_cparams(("parallel",), PROJ_VMEM_MIB),
        name="mix_proj",
    )(x, ln, w, cos_tab, sin_tab, gq, gkv, wuq, wuk, wuv, cos_r, sin_r)


def _fill_v_ones(v_ref, va_ref):
    width = v_ref.shape[1]
    va_ref[:, :width] = v_ref[...]
    va_ref[:, width:] = jnp.ones((v_ref.shape[0], va_ref.shape[1] - width), va_ref.dtype)


_V_ONES_SCRATCH = [pltpu.VMEM((SEQ, 2 * HEAD_DIM), jnp.bfloat16)]


def _mla_attn_kernel(q_ref, k_ref, v_ref, o_ref, va_ref):
    _fill_v_ones(v_ref, va_ref)
    t = MLA_T
    tri = _causal_tri(t)
    for c in reversed(range(SEQ // t)):
        n = (c + 1) * t
        s = _dot_nt(q_ref[_tile(c, t), :], k_ref[0:n, :])
        parts = [s[:, _tile(j, t)] for j in range(c)]
        parts.append(jnp.where(tri, s[:, _tile(c, t)], NEG_INF))
        o_ref[_tile(c, t), :] = _softmax_pv(parts, va_ref)


def _moba_attn_kernel(q_ref, k_ref, v_ref, o_ref, va_ref):
    _fill_v_ones(v_ref, va_ref)
    rid = lax.broadcasted_iota(jnp.int32, (LANES, HEAD_DIM), 0)
    km = jnp.zeros((LANES, HEAD_DIM), jnp.float32)
    for j in range(MOBA_NBLK):
        kj = k_ref[_tile(j), :].astype(jnp.float32)
        mean_j = jnp.sum(kj, axis=0, keepdims=True) * (1.0 / MOBA_BLOCK)
        km = jnp.where(rid == j, mean_j, km)
    km_hi = km.astype(jnp.bfloat16)
    km_lo = (km - km_hi.astype(jnp.float32)).astype(jnp.bfloat16)
    tri = _causal_tri()

    for c in ATT_ORDER:
        n = (c + 1) * ATT_T
        q = q_ref[_tile(c), :]
        s = _dot_nt(q, k_ref[0:n, :])
        parts = [s[:, _tile(j)] for j in range(c)]
        if c > MOBA_TOPK:
            gate = _dot_nt(q, km_hi) + _dot_nt(q, km_lo)
            lane = lax.broadcasted_iota(jnp.int32, gate.shape, 1)
            ahead = jnp.zeros(gate.shape, jnp.float32)
            for jp in range(c):
                cj = _lane_col(gate, jp, lane)
                wins = (cj > gate) | ((cj == gate) & (lane > jp))
                ahead = ahead + jnp.where(wins, 1.0, 0.0)
            sel = jnp.where(ahead < MOBA_TOPK, 1.0, 0.0)
            parts = [jnp.where(_lane_col(sel, j, lane) > 0.5, parts[j], NEG_INF)
                     for j in range(c)]
        parts.append(jnp.where(tri, s[:, _tile(c)], NEG_INF))
        o_ref[_tile(c), :] = _softmax_pv(parts, va_ref)


def _mla_moba_attn_kernel(qm_ref, km_ref, vm_ref, q_ref, k_ref, v_ref, om_ref, o_ref,
                          vam_ref, va_ref):
    _moba_attn_kernel(q_ref, k_ref, v_ref, o_ref, va_ref)
    _mla_attn_kernel(qm_ref, km_ref, vm_ref, om_ref, vam_ref)


def _mla_moba_attn(q_mla, k_mla, v_mla, qkv):
    assert MLA_HEADS == MOBA_HEADS
    return pl.pallas_call(
        _mla_moba_attn_kernel,
        grid=(BATCH, MLA_HEADS),
        in_specs=[_head_spec(MLA_QK_PAD, 0), _head_spec(MLA_QK_PAD, 0), _head_spec(MLA_V, 0),
                  _head_spec(HEAD_DIM, 0), _head_spec(HEAD_DIM, MOBA_HEADS),
                  _head_spec(HEAD_DIM, 2 * MOBA_HEADS)],
        out_specs=[_head_spec(MLA_V, 0), _head_spec(HEAD_DIM, 0)],
        out_shape=[jax.ShapeDtypeStruct((TOKENS, MLA_HEADS * MLA_V), jnp.bfloat16),
                   jax.ShapeDtypeStruct((TOKENS, MOBA_HEADS * HEAD_DIM), jnp.bfloat16)],
        scratch_shapes=_V_ONES_SCRATCH + _V_ONES_SCRATCH,
        compiler_params=_cparams(("parallel", "parallel"), ATT_VMEM_MIB),
        name="mla_moba_attn",
    )(q_mla, k_mla, v_mla, qkv, qkv, qkv)


def _dil_tables():
    t = DIL_T
    r = np.arange(t)[:, None]
    c = np.arange(t)[None, :]
    cnts = []
    for d in range(SEQ // t):
        delta = r - c + t * d
        cnts.append(sum(((delta >= 0) & (delta <= w) & (delta % dil == 0)).astype(np.float32)
                        for w, dil in DIL_PATTERNS))
    cnt = np.stack(cnts)
    bias = np.where(cnt > 0, 0.0, NEG_INF).astype(np.float32)
    n_weighted = max(d + 1 for d in range(SEQ // t) if cnt[d].max() > 1)
    return bias, cnt[:n_weighted]


def _dil_attn_kernel(q_ref, k_ref, v_ref, bias_ref, cnt_ref, o_ref, va_ref):
    n_weighted = cnt_ref.shape[0]
    for hh in range(DIL_HPS):
        hs = slice(hh * HEAD_DIM, (hh + 1) * HEAD_DIM)
        va = va_ref.at[hh]
        va[:, :HEAD_DIM] = v_ref[:, hs]
        va[:, HEAD_DIM:] = jnp.ones((SEQ, HEAD_DIM), va_ref.dtype)
    t = DIL_T
    for c in reversed(range(SEQ // t)):
        n = (c + 1) * t
        weights = [cnt_ref[c - j] if c - j < n_weighted else None for j in range(c + 1)]
        for hh in range(DIL_HPS):
            hs = slice(hh * HEAD_DIM, (hh + 1) * HEAD_DIM)
            s = _dot_nt(q_ref[_tile(c, t), hs], k_ref[0:n, hs])
            tiles = [s[:, _tile(j, t)] + bias_ref[c - j] for j in range(c + 1)]
            o_ref[_tile(c, t), hs] = _softmax_pv(tiles, va_ref.at[hh], weights)


def _dil_attn(qkv):
    bias, cnt = _dil_tables()
    return pl.pallas_call(
        _dil_attn_kernel,
        grid=(BATCH, DIL_HEADS // DIL_HPS),
        in_specs=[_head_spec(DIL_HPS * HEAD_DIM, 0),
                  _head_spec(DIL_HPS * HEAD_DIM, DIL_HEADS // DIL_HPS),
                  _head_spec(DIL_HPS * HEAD_DIM, 2 * DIL_HEADS // DIL_HPS),
                  pl.BlockSpec(bias.shape, lambda b, h: (0, 0, 0)),
                  pl.BlockSpec(cnt.shape, lambda b, h: (0, 0, 0))],
        out_specs=_head_spec(DIL_HPS * HEAD_DIM, 0),
        out_shape=jax.ShapeDtypeStruct((TOKENS, DIL_HEADS * HEAD_DIM), jnp.bfloat16),
        scratch_shapes=[pltpu.VMEM((DIL_HPS, SEQ, 2 * HEAD_DIM), jnp.bfloat16)],
        compiler_params=_cparams(("parallel", "parallel"), ATT_VMEM_MIB),
        name="dil_attn",
    )(qkv, qkv, qkv, jnp.asarray(bias), jnp.asarray(cnt))


def _out_proj_kernel(x_ref, mla_ref, moba_ref, dil_ref, w_ref, o_ref, wb_ref):
    @pl.when(pl.program_id(0) == 0)
    def _():
        wb_ref[...] = w_ref[...].astype(jnp.bfloat16)

    mix = jnp.concatenate([mla_ref[...], moba_ref[...], dil_ref[...]], axis=1)
    o_ref[...] = x_ref[...] + jnp.dot(mix, wb_ref[...], preferred_element_type=jnp.float32)


def _out_proj(x, o_mla, o_moba, o_dil, w_out, layer):
    tm = OUT_TM

    def rows(width):
        return pl.BlockSpec((tm, width), lambda i: (i, 0))

    return pl.pallas_call(
        _out_proj_kernel,
        grid=(TOKENS // tm,),
        in_specs=[rows(D_MODEL), rows(MLA_HEADS * MLA_V), rows(MOBA_HEADS * HEAD_DIM),
                  rows(DIL_HEADS * HEAD_DIM),
                  pl.BlockSpec((None, MIX_WIDTH, D_MODEL), lambda i: (layer, 0, 0),
                               pipeline_mode=pl.Buffered(1))],
        out_specs=rows(D_MODEL),
        out_shape=jax.ShapeDtypeStruct((TOKENS, D_MODEL), jnp.float32),
        scratch_shapes=[pltpu.VMEM((MIX_WIDTH, D_MODEL), jnp.bfloat16)],
        compiler_params=_cparams(("arbitrary",), PROJ_VMEM_MIB),
        name="out_proj",
    )(x, o_mla, o_moba, o_dil, w_out)


def _rope_tables():
    pos = np.arange(SEQ, dtype=np.float64)[:, None]
    inv_h = ROPE_THETA ** (-np.arange(0, HEAD_DIM, 2, dtype=np.float64) / HEAD_DIM)
    ang = pos * inv_h[None, :]
    cos_h = np.concatenate([np.cos(ang), np.cos(ang)], axis=1)
    sin_h = np.concatenate([-np.sin(ang), np.sin(ang)], axis=1)
    scale = HEAD_DIM ** -0.5 * LOG2E
    cos_qk = np.stack([cos_h * scale, cos_h])
    sin_qk = np.stack([sin_h * scale, sin_h])
    inv_r = ROPE_THETA ** (-np.arange(0, MLA_ROPE, 2, dtype=np.float64) / MLA_ROPE)
    ang_r = pos * inv_r[None, :]
    z = np.zeros_like(ang_r)
    cos_r = np.concatenate([np.cos(ang_r), z, np.cos(ang_r), z], axis=1)
    sin_r = np.concatenate([-np.sin(ang_r), z, np.sin(ang_r), z], axis=1)
    return tuple(jnp.asarray(t, jnp.float32) for t in (cos_qk, sin_qk, cos_r, sin_r))


def _pad_rope_cols(w):
    half = MLA_ROPE // 2
    z = jnp.zeros(w.shape[:-1] + (half,), w.dtype)
    return jnp.concatenate([w[..., :half], z, w[..., half:], z], axis=-1)


def _prep_mla_weights(w_uq, w_uk, w_uv):
    bf = jnp.bfloat16
    uq = w_uq.reshape(MLA_Q_RANK, MLA_HEADS, MLA_NOPE + MLA_ROPE)
    uq = jnp.concatenate([uq[..., :MLA_NOPE], _pad_rope_cols(uq[..., MLA_NOPE:])], axis=-1)
    uq = uq.reshape(MLA_Q_RANK, MLA_HEADS * MLA_QK_PAD).astype(bf)
    return uq, w_uk.astype(bf), w_uv.astype(bf)


def kernel(x, ln_ffn1, w_ffn1_gate, w_ffn1_up, w_ffn1_down, ln_mix, w_in, g_mla_q, g_mla_kv,
           w_mla_uq, w_mla_uk, w_mla_uv, w_out, ln_ffn2, w_ffn2_gate, w_ffn2_up, w_ffn2_down,
           ln_final):
    cos_qk, sin_qk, cos_r, sin_r = _rope_tables()
    w_proj = jnp.swapaxes(w_in, 1, 2).astype(jnp.bfloat16)
    xt = x.reshape(TOKENS, D_MODEL)
    for l in range(DEPTH):
        uq, uk, uv = _prep_mla_weights(w_mla_uq[l], w_mla_uk[l], w_mla_uv[l])
        xt = _ffn(xt, ln_ffn1[l][None], w_ffn1_gate, w_ffn1_up, w_ffn1_down, l)
        q_mla, k_mla, v_mla, qkv_moba, qkv_dil = _proj(
            xt, ln_mix[l][None], w_proj, l, cos_qk, sin_qk,
            g_mla_q[l][None], g_mla_kv[l][None], uq, uk, uv, cos_r, sin_r)
        o_mla, o_moba = _mla_moba_attn(q_mla, k_mla, v_mla, qkv_moba)
        o_dil = _dil_attn(qkv_dil)
        xt = _out_proj(xt, o_mla, o_moba, o_dil, w_out, l)
        xt = _ffn(xt, ln_ffn2[l][None], w_ffn2_gate, w_ffn2_up, w_ffn2_down, l,
                  ln_final=ln_final[None] if l == DEPTH - 1 else None)
    return xt.reshape(BATCH, SEQ, D_MODEL)
```

```python
import functools

import numpy as np

import jax
import jax.numpy as jnp
from jax import lax
from jax.experimental import pallas as pl
from jax.experimental.pallas import tpu as pltpu

D_MODEL = 2048
BATCH = 4
SEQ = 2048
DEPTH = 2
TOKENS = BATCH * SEQ

HEAD_DIM = 128
MLA_HEADS = 4
MLA_Q_RANK = 512
MLA_KV_RANK = 256
MLA_NOPE = 128
MLA_ROPE = 64
MLA_V = 128
MLA_QK_PAD = 256
MOBA_HEADS = 4
MOBA_BLOCK = 256
MOBA_TOPK = 3
MOBA_NBLK = SEQ // MOBA_BLOCK
DIL_HEADS = 8
DIL_PATTERNS = ((128, 1), (512, 4), (2048, 16))
D_FF = 5632
ROPE_THETA = 10000.0
NORM_EPS = 1e-6
NEG_INF = -1e30
LOG2E = 1.4426950408889634

MLA_IN = MLA_Q_RANK + MLA_KV_RANK + MLA_ROPE
MOBA_IN = 3 * MOBA_HEADS * HEAD_DIM
DIL_IN = 3 * DIL_HEADS * HEAD_DIM
MIX_WIDTH = MLA_HEADS * MLA_V + MOBA_HEADS * HEAD_DIM + DIL_HEADS * HEAD_DIM

LANES = 128
MIB = 1024 * 1024
FFN_VMEM_MIB = 60
PROJ_VMEM_MIB = 56

FFN_TM = 1024
FFN_TF = 512
FFN_TF_HEAD = 256
SIDE_ROWS = 80
PROJ_TM = 512
PROJ_TN = 512
ATT_T = 256
ATT_NT = SEQ // ATT_T
MLA_T = 512
DIL_T = 256
ATT_ORDER = tuple(reversed(range(ATT_NT)))
ATT_VMEM_MIB = 56
OUT_TM = 512
DIL_HPS = 2

assert ATT_T == MOBA_BLOCK


def _cparams(semantics, vmem_mib):
    return pltpu.CompilerParams(dimension_semantics=semantics,
                                vmem_limit_bytes=vmem_mib * MIB)


def _rms(x, g):
    ms = jnp.mean(x * x, axis=-1, keepdims=True)
    return x * lax.rsqrt(ms + NORM_EPS) * g


def _lane_col(a, j, lane):
    return jnp.sum(jnp.where(lane == j, a, 0.0), axis=1, keepdims=True)


def _dot_nt(a, b):
    return lax.dot_general(a, b, (((1,), (1,)), ((), ())),
                           preferred_element_type=jnp.float32)


def _tile(c, t=ATT_T):
    return slice(c * t, (c + 1) * t)


def _causal_tri(t=ATT_T):
    r = lax.broadcasted_iota(jnp.int32, (t, t), 0)
    c = lax.broadcasted_iota(jnp.int32, (t, t), 1)
    return c <= r


def _softmax_pv(tiles, v_ref, weights=None):
    mx = tiles[0]
    for t in tiles[1:]:
        mx = jnp.maximum(mx, t)
    m = jnp.max(mx, axis=1, keepdims=True)
    res = None
    for j, t in enumerate(tiles):
        e = jnp.exp2(t - m)
        if weights is not None and weights[j] is not None:
            e = e * weights[j]
        pv = jnp.dot(e.astype(jnp.bfloat16), v_ref[_tile(j, t.shape[1]), :],
                     preferred_element_type=jnp.float32)
        res = pv if res is None else res + pv
    half = res.shape[1] // 2
    return (res[:, :half] / res[:, half:]).astype(jnp.bfloat16)


def _head_spec(width, col_offset):
    return pl.BlockSpec((SEQ, width), lambda b, h: (b, col_offset + h))


def _ffn_begin(x_ref, ln_ref, h_ref, o_ref):
    @pl.when(pl.program_id(1) == 0)
    def _():
        x = x_ref[...]
        h_ref[...] = _rms(x, ln_ref[...]).astype(jnp.bfloat16)
        o_ref[...] = x


def _ffn_act(h, wgu):
    tf = wgu.shape[1] // 2
    gu = jnp.dot(h, wgu, preferred_element_type=jnp.float32)
    g = gu[:, :tf]
    u = gu[:, tf:]
    return (g * (1.0 / (1.0 + jnp.exp(-g))) * u * 0.5).astype(jnp.bfloat16)


def _ffn_end(o_ref, lnf_ref):
    if lnf_ref is not None:
        @pl.when(pl.program_id(1) == pl.num_programs(1) - 1)
        def _():
            o_ref[...] = _rms(o_ref[...], lnf_ref[...])


def _ffn_head_kernel(x_ref, ln_ref, wg_ref, wu_ref, wd_ref, *rest, final_norm):
    rest = list(rest)
    lnf_ref = rest.pop(0) if final_norm else None
    o_ref, wgu16_ref, wd16_ref, h_ref = rest
    _ffn_begin(x_ref, ln_ref, h_ref, o_ref)
    wgu = jnp.concatenate([wg_ref[...].astype(jnp.bfloat16),
                           wu_ref[...].astype(jnp.bfloat16)], axis=1)
    wd = wd_ref[...].astype(jnp.bfloat16)
    wgu16_ref[...] = wgu
    wd16_ref[...] = wd
    o_ref[...] += jnp.dot(_ffn_act(h_ref[...], wgu), wd, preferred_element_type=jnp.float32)
    _ffn_end(o_ref, lnf_ref)


def _ffn_rest_kernel(x_ref, ln_ref, wgu_ref, wd_ref, *rest, final_norm, side_cast):
    rest = list(rest)
    lnf_ref = rest.pop(0) if final_norm else None
    if side_cast:
        _, side_ref, o_ref, side16_ref, h_ref = rest
        side16_ref[...] = side_ref[...].astype(side16_ref.dtype)
    else:
        _, o_ref, h_ref = rest
    _ffn_begin(x_ref, ln_ref, h_ref, o_ref)
    h = h_ref[...]
    a = jnp.concatenate([_ffn_act(h, wgu_ref[t]) for t in range(wgu_ref.shape[0])], axis=1)
    o_ref[...] += jnp.dot(a, wd_ref[...], preferred_element_type=jnp.float32)
    _ffn_end(o_ref, lnf_ref)


def _ffn(x, ln, wg, wu, wd, layer, ln_final=None, side=None):
    final_norm = ln_final is not None
    tm = FFN_TM
    bf = jnp.bfloat16
    out_shape = jax.ShapeDtypeStruct((TOKENS, D_MODEL), jnp.float32)
    vec_spec = pl.BlockSpec((1, D_MODEL), lambda i, j: (0, 0))
    tail_specs = [vec_spec] if final_norm else []
    tail_args = [ln_final] if final_norm else []
    suffix = "_final" if final_norm else ""

    th = FFN_TF_HEAD
    n_head_tiles = D_FF // th
    group = FFN_TF // th
    scratch = [pltpu.VMEM((tm, D_MODEL), bf)]
    cparams = _cparams(("parallel", "arbitrary"), FFN_VMEM_MIB)

    y, wgu16, wd16 = pl.pallas_call(
        functools.partial(_ffn_head_kernel, final_norm=final_norm),
        grid=(1, n_head_tiles),
        in_specs=[pl.BlockSpec((tm, D_MODEL), lambda i, j: (0, 0), pipeline_mode=pl.Buffered(1)),
                  vec_spec,
                  pl.BlockSpec((None, D_MODEL, th), lambda i, j: (layer, 0, j)),
                  pl.BlockSpec((None, D_MODEL, th), lambda i, j: (layer, 0, j)),
                  pl.BlockSpec((None, th, D_MODEL), lambda i, j: (layer, j, 0))] + tail_specs,
        out_specs=[pl.BlockSpec((tm, D_MODEL), lambda i, j: (0, 0)),
                   pl.BlockSpec((None, D_MODEL, 2 * th), lambda i, j: (j, 0, 0)),
                   pl.BlockSpec((th, D_MODEL), lambda i, j: (j, 0))],
        out_shape=[out_shape, jax.ShapeDtypeStruct((n_head_tiles, D_MODEL, 2 * th), bf),
                   jax.ShapeDtypeStruct((D_FF, D_MODEL), bf)],
        scratch_shapes=scratch,
        compiler_params=cparams,
        name="ffn_head" + suffix,
    )(x, ln, wg, wu, wd, *tail_args)

    nj = n_head_tiles // group
    side_specs, side_args, side_out_specs, side_out_shapes = [], [], [], []
    if side is not None:
        _, rows, cols = side.shape
        n_side = rows // SIDE_ROWS
        assert rows % SIDE_ROWS == 0 and n_side <= (TOKENS // tm - 1) * nj

        def side_block(i, j):
            return jnp.minimum(i * nj + j, n_side - 1)

        side_specs = [pl.BlockSpec((None, SIDE_ROWS, cols),
                                   lambda i, j: (layer, side_block(i, j), 0))]
        side_args = [side]
        side_out_specs = [pl.BlockSpec((SIDE_ROWS, cols), lambda i, j: (side_block(i, j), 0))]
        side_out_shapes = [jax.ShapeDtypeStruct((rows, cols), bf)]

    alias_index = 4 + len(tail_args)
    out = pl.pallas_call(
        functools.partial(_ffn_rest_kernel, final_norm=final_norm, side_cast=side is not None),
        grid=(TOKENS // tm - 1, nj),
        in_specs=[pl.BlockSpec((tm, D_MODEL), lambda i, j: (i + 1, 0)),
                  vec_spec,
                  pl.BlockSpec((group, D_MODEL, 2 * th), lambda i, j: (j, 0, 0)),
                  pl.BlockSpec((group * th, D_MODEL), lambda i, j: (j, 0))] + tail_specs
        + [pl.BlockSpec(memory_space=pl.ANY)] + side_specs,
        out_specs=[pl.BlockSpec((tm, D_MODEL), lambda i, j: (i + 1, 0))] + side_out_specs,
        out_shape=[out_shape] + side_out_shapes,
        scratch_shapes=scratch,
        input_output_aliases={alias_index: 0},
        compiler_params=_cparams(("arbitrary", "arbitrary"), FFN_VMEM_MIB),
        name="ffn_rest" + suffix,
    )(x, ln, wgu16, wd16, *tail_args, y, *side_args)
    return out[0] if side is None else tuple(out)


PROJ_WIDTH = MLA_IN + MOBA_IN + DIL_IN


def _mla_prep(lat, kr, gq_ref, gkv_ref, wuq_ref, wuk_ref, wuv_ref, cos_ref, sin_ref,
              q_ref, k_ref, v_ref):
    scale = (MLA_NOPE + MLA_ROPE) ** -0.5 * LOG2E
    c = cos_ref[...]
    s = sin_ref[...]

    def rope(t):
        return t * c + pltpu.roll(t, LANES // 2, 1) * s

    nq = MLA_Q_RANK // LANES
    cq = _rms(jnp.concatenate(lat[:nq], axis=1), gq_ref[...]).astype(jnp.bfloat16)
    q = jnp.dot(cq, wuq_ref[...], preferred_element_type=jnp.float32)
    ckv = _rms(jnp.concatenate(lat[nq:], axis=1), gkv_ref[...]).astype(jnp.bfloat16)
    kn = jnp.dot(ckv, wuk_ref[...], preferred_element_type=jnp.float32)
    v_ref[...] = jnp.dot(ckv, wuv_ref[...],
                         preferred_element_type=jnp.float32).astype(jnp.bfloat16)
    kr = rope(kr).astype(jnp.bfloat16)
    for h in range(MLA_HEADS):
        b0 = h * MLA_QK_PAD
        q_ref[:, b0:b0 + LANES] = (q[:, b0:b0 + LANES] * scale).astype(jnp.bfloat16)
        q_ref[:, b0 + LANES:b0 + 2 * LANES] = (
            rope(q[:, b0 + LANES:b0 + 2 * LANES]) * scale).astype(jnp.bfloat16)
        k_ref[:, b0:b0 + LANES] = kn[:, h * LANES:(h + 1) * LANES].astype(jnp.bfloat16)
        k_ref[:, b0 + LANES:b0 + 2 * LANES] = kr


def _proj_kernel(x_ref, ln_ref, wt_ref, cos_ref, sin_ref, *rest):
    mla_args, (moba_ref, dil_ref) = rest[:-2], rest[-2:]
    h = _rms(x_ref[...], ln_ref[...]).astype(jnp.bfloat16)

    def proj(r0, rows):
        return _dot_nt(h, wt_ref[r0:r0 + rows, :])

    n_lat = MLA_Q_RANK + MLA_KV_RANK
    lat = []
    for r0 in range(0, n_lat, PROJ_TN):
        y = proj(r0, min(PROJ_TN, n_lat - r0))
        lat += [y[:, g0:g0 + LANES] for g0 in range(0, y.shape[1], LANES)]
    yk = proj(n_lat, LANES)
    lane = lax.broadcasted_iota(jnp.int32, yk.shape, 1)
    half = MLA_ROPE // 2
    kr = (jnp.where(lane < half, yk, 0.0)
          + jnp.where((lane >= LANES // 2) & (lane < LANES // 2 + half),
                      pltpu.roll(yk, half, 1), 0.0))
    _mla_prep(lat, kr, *mla_args)

    row = MLA_IN
    for o_ref, width in ((moba_ref, MOBA_IN), (dil_ref, DIL_IN)):
        for c0 in range(0, width, PROJ_TN):
            y = proj(row + c0, PROJ_TN)
            section = c0 // (width // 3)
            if section == 2:
                o_ref[:, c0:c0 + PROJ_TN] = y.astype(o_ref.dtype)
                continue
            c = cos_ref[section]
            s = sin_ref[section]
            for g0 in range(0, PROJ_TN, LANES):
                yg = y[:, g0:g0 + LANES]
                o_ref[:, c0 + g0:c0 + g0 + LANES] = (
                    yg * c + pltpu.roll(yg, LANES // 2, 1) * s).astype(o_ref.dtype)
        row += width


def _proj(x, ln, w, cos_tab, sin_tab, gq, gkv, wuq, wuk, wuv, cos_r, sin_r):
    tm = PROJ_TM
    pos_blocks = SEQ // tm
    qk_w = MLA_HEADS * MLA_QK_PAD
    v_w = MLA_HEADS * MLA_V
    bf = jnp.bfloat16

    def rows(width):
        return pl.BlockSpec((tm, width), lambda i: (i, 0))

    def full(shape):
        return pl.BlockSpec(shape, lambda i: (0, 0))

    def tab_spec():
        return pl.BlockSpec((2, tm, LANES), lambda i: (0, i % pos_blocks, 0))

    def rtab_spec():
        return pl.BlockSpec((tm, LANES), lambda i: (i % pos_blocks, 0))

    return pl.pallas_call(
        _proj_kernel,
        grid=(TOKENS // tm,),
        in_specs=[
            rows(D_MODEL),
            full((1, D_MODEL)),
            pl.BlockSpec((PROJ_WIDTH, D_MODEL), lambda i: (0, 0), pipeline_mode=pl.Buffered(1)),
            tab_spec(), tab_spec(),
            full((1, MLA_Q_RANK)), full((1, MLA_KV_RANK)),
            full((MLA_Q_RANK, qk_w)), full((MLA_KV_RANK, v_w)), full((MLA_KV_RANK, v_w)),
            rtab_spec(), rtab_spec(),
        ],
        out_specs=[rows(qk_w), rows(qk_w), rows(v_w), rows(MOBA_IN), rows(DIL_IN)],
        out_shape=[
            jax.ShapeDtypeStruct((TOKENS, qk_w), bf),
            jax.ShapeDtypeStruct((TOKENS, qk_w), bf),
            jax.ShapeDtypeStruct((TOKENS, v_w), bf),
            jax.ShapeDtypeStruct((TOKENS, MOBA_IN), bf),
            jax.ShapeDtypeStruct((TOKENS, DIL_IN), bf),
        ],
        compiler_params=_cparams(("parallel",), PROJ_VMEM_MIB),
        name="mix_proj",
    )(x, ln, w, cos_tab, sin_tab, gq, gkv, wuq, wuk, wuv, cos_r, sin_r)


def _fill_v_ones(v_ref, va_ref):
    width = v_ref.shape[1]
    va_ref[:, :width] = v_ref[...]
    va_ref[:, width:] = jnp.ones((v_ref.shape[0], va_ref.shape[1] - width), va_ref.dtype)


_V_ONES_SCRATCH = [pltpu.VMEM((SEQ, 2 * HEAD_DIM), jnp.bfloat16)]


def _mla_attn_kernel(q_ref, k_ref, v_ref, o_ref, va_ref):
    _fill_v_ones(v_ref, va_ref)
    t = MLA_T
    tri = _causal_tri(t)
    for c in reversed(range(SEQ // t)):
        n = (c + 1) * t
        s = _dot_nt(q_ref[_tile(c, t), :], k_ref[0:n, :])
        parts = [s[:, _tile(j, t)] for j in range(c)]
        parts.append(jnp.where(tri, s[:, _tile(c, t)], NEG_INF))
        o_ref[_tile(c, t), :] = _softmax_pv(parts, va_ref)


def _moba_attn_kernel(q_ref, k_ref, v_ref, o_ref, va_ref):
    _fill_v_ones(v_ref, va_ref)
    rid = lax.broadcasted_iota(jnp.int32, (LANES, HEAD_DIM), 0)
    km = jnp.zeros((LANES, HEAD_DIM), jnp.float32)
    for j in range(MOBA_NBLK):
        kj = k_ref[_tile(j), :].astype(jnp.float32)
        mean_j = jnp.sum(kj, axis=0, keepdims=True) * (1.0 / MOBA_BLOCK)
        km = jnp.where(rid == j, mean_j, km)
    km_hi = km.astype(jnp.bfloat16)
    km_lo = (km - km_hi.astype(jnp.float32)).astype(jnp.bfloat16)
    tri = _causal_tri()

    for c in ATT_ORDER:
        n = (c + 1) * ATT_T
        q = q_ref[_tile(c), :]
        s = _dot_nt(q, k_ref[0:n, :])
        parts = [s[:, _tile(j)] for j in range(c)]
        if c > MOBA_TOPK:
            gate = _dot_nt(q, km_hi) + _dot_nt(q, km_lo)
            lane = lax.broadcasted_iota(jnp.int32, gate.shape, 1)
            ahead = jnp.zeros(gate.shape, jnp.float32)
            for jp in range(c):
                cj = _lane_col(gate, jp, lane)
                wins = (cj > gate) | ((cj == gate) & (lane > jp))
                ahead = ahead + jnp.where(wins, 1.0, 0.0)
            sel = jnp.where(ahead < MOBA_TOPK, 1.0, 0.0)
            parts = [jnp.where(_lane_col(sel, j, lane) > 0.5, parts[j], NEG_INF)
                     for j in range(c)]
        parts.append(jnp.where(tri, s[:, _tile(c)], NEG_INF))
        o_ref[_tile(c), :] = _softmax_pv(parts, va_ref)


def _mla_moba_attn_kernel(qm_ref, km_ref, vm_ref, q_ref, k_ref, v_ref, om_ref, o_ref,
                          vam_ref, va_ref):
    _moba_attn_kernel(q_ref, k_ref, v_ref, o_ref, va_ref)
    _mla_attn_kernel(qm_ref, km_ref, vm_ref, om_ref, vam_ref)


def _mla_moba_attn(q_mla, k_mla, v_mla, qkv):
    assert MLA_HEADS == MOBA_HEADS
    return pl.pallas_call(
        _mla_moba_attn_kernel,
        grid=(BATCH, MLA_HEADS),
        in_specs=[_head_spec(MLA_QK_PAD, 0), _head_spec(MLA_QK_PAD, 0), _head_spec(MLA_V, 0),
                  _head_spec(HEAD_DIM, 0), _head_spec(HEAD_DIM, MOBA_HEADS),
                  _head_spec(HEAD_DIM, 2 * MOBA_HEADS)],
        out_specs=[_head_spec(MLA_V, 0), _head_spec(HEAD_DIM, 0)],
        out_shape=[jax.ShapeDtypeStruct((TOKENS, MLA_HEADS * MLA_V), jnp.bfloat16),
                   jax.ShapeDtypeStruct((TOKENS, MOBA_HEADS * HEAD_DIM), jnp.bfloat16)],
        scratch_shapes=_V_ONES_SCRATCH + _V_ONES_SCRATCH,
        compiler_params=_cparams(("parallel", "parallel"), ATT_VMEM_MIB),
        name="mla_moba_attn",
    )(q_mla, k_mla, v_mla, qkv, qkv, qkv)


def _dil_tables():
    t = DIL_T
    r = np.arange(t)[:, None]
    c = np.arange(t)[None, :]
    cnts = []
    for d in range(SEQ // t):
        delta = r - c + t * d
        cnts.append(sum(((delta >= 0) & (delta <= w) & (delta % dil == 0)).astype(np.float32)
                        for w, dil in DIL_PATTERNS))
    cnt = np.stack(cnts)
    bias = np.where(cnt > 0, 0.0, NEG_INF).astype(np.float32)
    n_weighted = max(d + 1 for d in range(SEQ // t) if cnt[d].max() > 1)
    return bias, cnt[:n_weighted]


def _dil_attn_kernel(q_ref, k_ref, v_ref, bias_ref, cnt_ref, o_ref, va_ref):
    n_weighted = cnt_ref.shape[0]
    for hh in range(DIL_HPS):
        hs = slice(hh * HEAD_DIM, (hh + 1) * HEAD_DIM)
        va = va_ref.at[hh]
        va[:, :HEAD_DIM] = v_ref[:, hs]
        va[:, HEAD_DIM:] = jnp.ones((SEQ, HEAD_DIM), va_ref.dtype)
    t = DIL_T
    for c in reversed(range(SEQ // t)):
        n = (c + 1) * t
        weights = [cnt_ref[c - j] if c - j < n_weighted else None for j in range(c + 1)]
        for hh in range(DIL_HPS):
            hs = slice(hh * HEAD_DIM, (hh + 1) * HEAD_DIM)
            s = _dot_nt(q_ref[_tile(c, t), hs], k_ref[0:n, hs])
            tiles = [s[:, _tile(j, t)] + bias_ref[c - j] for j in range(c + 1)]
            o_ref[_tile(c, t), hs] = _softmax_pv(tiles, va_ref.at[hh], weights)


def _dil_attn(qkv):
    bias, cnt = _dil_tables()
    return pl.pallas_call(
        _dil_attn_kernel,
        grid=(BATCH, DIL_HEADS // DIL_HPS),
        in_specs=[_head_spec(DIL_HPS * HEAD_DIM, 0),
                  _head_spec(DIL_HPS * HEAD_DIM, DIL_HEADS // DIL_HPS),
                  _head_spec(DIL_HPS * HEAD_DIM, 2 * DIL_HEADS // DIL_HPS),
                  pl.BlockSpec(bias.shape, lambda b, h: (0, 0, 0)),
                  pl.BlockSpec(cnt.shape, lambda b, h: (0, 0, 0))],
        out_specs=_head_spec(DIL_HPS * HEAD_DIM, 0),
        out_shape=jax.ShapeDtypeStruct((TOKENS, DIL_HEADS * HEAD_DIM), jnp.bfloat16),
        scratch_shapes=[pltpu.VMEM((DIL_HPS, SEQ, 2 * HEAD_DIM), jnp.bfloat16)],
        compiler_params=_cparams(("parallel", "parallel"), ATT_VMEM_MIB),
        name="dil_attn",
    )(qkv, qkv, qkv, jnp.asarray(bias), jnp.asarray(cnt))


def _out_proj_kernel(x_ref, mla_ref, moba_ref, dil_ref, w_ref, o_ref, wb_ref):
    @pl.when(pl.program_id(0) == 0)
    def _():
        wb_ref[...] = w_ref[...].astype(jnp.bfloat16)

    mix = jnp.concatenate([mla_ref[...], moba_ref[...], dil_ref[...]], axis=1)
    o_ref[...] = x_ref[...] + jnp.dot(mix, wb_ref[...], preferred_element_type=jnp.float32)


def _out_proj(x, o_mla, o_moba, o_dil, w_out, layer):
    tm = OUT_TM

    def rows(width):
        return pl.BlockSpec((tm, width), lambda i: (i, 0))

    return pl.pallas_call(
        _out_proj_kernel,
        grid=(TOKENS // tm,),
        in_specs=[rows(D_MODEL), rows(MLA_HEADS * MLA_V), rows(MOBA_HEADS * HEAD_DIM),
                  rows(DIL_HEADS * HEAD_DIM),
                  pl.BlockSpec((None, MIX_WIDTH, D_MODEL), lambda i: (layer, 0, 0),
                               pipeline_mode=pl.Buffered(1))],
        out_specs=rows(D_MODEL),
        out_shape=jax.ShapeDtypeStruct((TOKENS, D_MODEL), jnp.float32),
        scratch_shapes=[pltpu.VMEM((MIX_WIDTH, D_MODEL), jnp.bfloat16)],
        compiler_params=_cparams(("arbitrary",), PROJ_VMEM_MIB),
        name="out_proj",
    )(x, o_mla, o_moba, o_dil, w_out)


def _rope_tables():
    pos = np.arange(SEQ, dtype=np.float64)[:, None]
    inv_h = ROPE_THETA ** (-np.arange(0, HEAD_DIM, 2, dtype=np.float64) / HEAD_DIM)
    ang = pos * inv_h[None, :]
    cos_h = np.concatenate([np.cos(ang), np.cos(ang)], axis=1)
    sin_h = np.concatenate([-np.sin(ang), np.sin(ang)], axis=1)
    scale = HEAD_DIM ** -0.5 * LOG2E
    cos_qk = np.stack([cos_h * scale, cos_h])
    sin_qk = np.stack([sin_h * scale, sin_h])
    inv_r = ROPE_THETA ** (-np.arange(0, MLA_ROPE, 2, dtype=np.float64) / MLA_ROPE)
    ang_r = pos * inv_r[None, :]
    z = np.zeros_like(ang_r)
    cos_r = np.concatenate([np.cos(ang_r), z, np.cos(ang_r), z], axis=1)
    sin_r = np.concatenate([-np.sin(ang_r), z, np.sin(ang_r), z], axis=1)
    return tuple(jnp.asarray(t, jnp.float32) for t in (cos_qk, sin_qk, cos_r, sin_r))


def _pad_rope_cols(w):
    half = MLA_ROPE // 2
    z = jnp.zeros(w.shape[:-1] + (half,), w.dtype)
    return jnp.concatenate([w[..., :half], z, w[..., half:], z], axis=-1)


def _prep_mla_weights(w_uq, w_uk, w_uv):
    bf = jnp.bfloat16
    uq = w_uq.reshape(MLA_Q_RANK, MLA_HEADS, MLA_NOPE + MLA_ROPE)
    uq = jnp.concatenate([uq[..., :MLA_NOPE], _pad_rope_cols(uq[..., MLA_NOPE:])], axis=-1)
    uq = uq.reshape(MLA_Q_RANK, MLA_HEADS * MLA_QK_PAD).astype(bf)
    return uq, w_uk.astype(bf), w_uv.astype(bf)


def kernel(x, ln_ffn1, w_ffn1_gate, w_ffn1_up, w_ffn1_down, ln_mix, w_in, g_mla_q, g_mla_kv,
           w_mla_uq, w_mla_uk, w_mla_uv, w_out, ln_ffn2, w_ffn2_gate, w_ffn2_up, w_ffn2_down,
           ln_final):
    cos_qk, sin_qk, cos_r, sin_r = _rope_tables()
    w_in_t = jnp.swapaxes(w_in, 1, 2)
    xt = x.reshape(TOKENS, D_MODEL)
    for l in range(DEPTH):
        uq, uk, uv = _prep_mla_weights(w_mla_uq[l], w_mla_uk[l], w_mla_uv[l])
        xt, w_proj = _ffn(xt, ln_ffn1[l][None], w_ffn1_gate, w_ffn1_up, w_ffn1_down, l,
                          side=w_in_t)
        q_mla, k_mla, v_mla, qkv_moba, qkv_dil = _proj(
            xt, ln_mix[l][None], w_proj, cos_qk, sin_qk,
            g_mla_q[l][None], g_mla_kv[l][None], uq, uk, uv, cos_r, sin_r)
        o_mla, o_moba = _mla_moba_attn(q_mla, k_mla, v_mla, qkv_moba)
        o_dil = _dil_attn(qkv_dil)
        xt = _out_proj(xt, o_mla, o_moba, o_dil, w_out, l)
        xt = _ffn(xt, ln_ffn2[l][None], w_ffn2_gate, w_ffn2_up, w_ffn2_down, l,
                  ln_final=ln_final[None] if l == DEPTH - 1 else None)
    return xt.reshape(BATCH, SEQ, D_MODEL)
```

```python
import functools

import numpy as np

import jax
import jax.numpy as jnp
from jax import lax
from jax.experimental import pallas as pl
from jax.experimental.pallas import tpu as pltpu

D_MODEL = 2048
BATCH = 4
SEQ = 2048
DEPTH = 2
TOKENS = BATCH * SEQ

HEAD_DIM = 128
MLA_HEADS = 4
MLA_Q_RANK = 512
MLA_KV_RANK = 256
MLA_NOPE = 128
MLA_ROPE = 64
MLA_V = 128
MLA_QK_PAD = 256
MOBA_HEADS = 4
MOBA_BLOCK = 256
MOBA_TOPK = 3
MOBA_NBLK = SEQ // MOBA_BLOCK
DIL_HEADS = 8
DIL_PATTERNS = ((128, 1), (512, 4), (2048, 16))
D_FF = 5632
ROPE_THETA = 10000.0
NORM_EPS = 1e-6
NEG_INF = -1e30
LOG2E = 1.4426950408889634

MLA_IN = MLA_Q_RANK + MLA_KV_RANK + MLA_ROPE
MOBA_IN = 3 * MOBA_HEADS * HEAD_DIM
DIL_IN = 3 * DIL_HEADS * HEAD_DIM
MIX_WIDTH = MLA_HEADS * MLA_V + MOBA_HEADS * HEAD_DIM + DIL_HEADS * HEAD_DIM

LANES = 128
MIB = 1024 * 1024
FFN_VMEM_MIB = 60
PROJ_VMEM_MIB = 56

FFN_TM = 1024
FFN_TF = 512
FFN_TF_HEAD = 256
SIDE_ROWS = 80
PROJ_TM = 512
PROJ_TN = 512
ATT_T = 256
ATT_NT = SEQ // ATT_T
MLA_T = 512
DIL_T = 256
ATT_ORDER = tuple(reversed(range(ATT_NT)))
ATT_VMEM_MIB = 56
OUT_TM = 512
DIL_HPS = 2

assert ATT_T == MOBA_BLOCK


def _cparams(semantics, vmem_mib):
    return pltpu.CompilerParams(dimension_semantics=semantics,
                                vmem_limit_bytes=vmem_mib * MIB)


def _rms(x, g):
    ms = jnp.mean(x * x, axis=-1, keepdims=True)
    return x * lax.rsqrt(ms + NORM_EPS) * g


def _lane_col(a, j, lane):
    return jnp.sum(jnp.where(lane == j, a, 0.0), axis=1, keepdims=True)


def _dot_nt(a, b):
    return lax.dot_general(a, b, (((1,), (1,)), ((), ())),
                           preferred_element_type=jnp.float32)


def _tile(c, t=ATT_T):
    return slice(c * t, (c + 1) * t)


def _causal_tri(t=ATT_T):
    r = lax.broadcasted_iota(jnp.int32, (t, t), 0)
    c = lax.broadcasted_iota(jnp.int32, (t, t), 1)
    return c <= r


def _softmax_pv(tiles, v_ref):
    mx = tiles[0]
    for t in tiles[1:]:
        mx = jnp.maximum(mx, t)
    m = jnp.max(mx, axis=1, keepdims=True)
    res = None
    for j, t in enumerate(tiles):
        e = jnp.exp2(t - m)
        pv = jnp.dot(e.astype(jnp.bfloat16), v_ref[_tile(j, t.shape[1]), :],
                     preferred_element_type=jnp.float32)
        res = pv if res is None else res + pv
    half = res.shape[1] // 2
    return (res[:, :half] / res[:, half:]).astype(jnp.bfloat16)


def _head_spec(width, col_offset):
    return pl.BlockSpec((SEQ, width), lambda b, h: (b, col_offset + h))


def _ffn_begin(first_step, x_ref, ln_ref, h_ref, o_ref):
    @pl.when(first_step)
    def _():
        x = x_ref[...]
        h_ref[...] = _rms(x, ln_ref[...]).astype(jnp.bfloat16)
        o_ref[...] = x


def _ffn_act(h, wgu):
    tf = wgu.shape[1] // 2
    gu = jnp.dot(h, wgu, preferred_element_type=jnp.float32)
    g = gu[:, :tf]
    u = gu[:, tf:]
    return (g * (1.0 / (1.0 + jnp.exp(-g))) * u * 0.5).astype(jnp.bfloat16)


def _ffn_end(last_step, o_ref, lnf_ref):
    if lnf_ref is not None:
        @pl.when(last_step)
        def _():
            o_ref[...] = _rms(o_ref[...], lnf_ref[...])


def _ffn_head_kernel(x_ref, ln_ref, wg_ref, wu_ref, wd_ref, *rest, final_norm):
    rest = list(rest)
    lnf_ref = rest.pop(0) if final_norm else None
    o_ref, wgu16_ref, wd16_ref, h_ref = rest
    j = pl.program_id(0)
    _ffn_begin(j == 0, x_ref, ln_ref, h_ref, o_ref)
    wgu = jnp.concatenate([wg_ref[...].astype(jnp.bfloat16),
                           wu_ref[...].astype(jnp.bfloat16)], axis=1)
    wd = wd_ref[...].astype(jnp.bfloat16)
    wgu16_ref[...] = wgu
    wd16_ref[...] = wd
    o_ref[...] += jnp.dot(_ffn_act(h_ref[...], wgu), wd, preferred_element_type=jnp.float32)
    _ffn_end(j == pl.num_programs(0) - 1, o_ref, lnf_ref)


def _ffn_rest_kernel(x_ref, ln_ref, wgu_ref, wd_ref, *rest, final_norm, side_cast, nj):
    rest = list(rest)
    lnf_ref = rest.pop(0) if final_norm else None
    y0_hbm = rest.pop(0)
    if side_cast:
        side_ref, o_ref, side16_ref, h_ref, sem = rest
        side16_ref[...] = side_ref[...].astype(side16_ref.dtype)
    else:
        o_ref, h_ref, sem = rest
    s = pl.program_id(0)

    @pl.when(s == 0)
    def _():
        copy = pltpu.make_async_copy(y0_hbm, o_ref, sem.at[0])
        copy.start()
        copy.wait()

    @pl.when(s > 0)
    def _():
        j = lax.rem(s - 1, nj)
        _ffn_begin(j == 0, x_ref, ln_ref, h_ref, o_ref)
        h = h_ref[...]
        a = jnp.concatenate([_ffn_act(h, wgu_ref[t]) for t in range(wgu_ref.shape[0])], axis=1)
        o_ref[...] += jnp.dot(a, wd_ref[...], preferred_element_type=jnp.float32)
        _ffn_end(j == nj - 1, o_ref, lnf_ref)


def _ffn(x, ln, wg, wu, wd, layer, ln_final=None, side=None):
    final_norm = ln_final is not None
    tm = FFN_TM
    bf = jnp.bfloat16
    vec_spec = pl.BlockSpec((1, D_MODEL), lambda s: (0, 0))
    tail_specs = [vec_spec] if final_norm else []
    tail_args = [ln_final] if final_norm else []
    suffix = "_final" if final_norm else ""

    th = FFN_TF_HEAD
    n_head_tiles = D_FF // th
    group = FFN_TF // th
    nj = n_head_tiles // group
    n_rest_tiles = TOKENS // tm - 1
    cparams = _cparams(("arbitrary",), FFN_VMEM_MIB)

    y0, wgu16, wd16 = pl.pallas_call(
        functools.partial(_ffn_head_kernel, final_norm=final_norm),
        grid=(n_head_tiles,),
        in_specs=[pl.BlockSpec((tm, D_MODEL), lambda j: (0, 0), pipeline_mode=pl.Buffered(1)),
                  vec_spec,
                  pl.BlockSpec((None, D_MODEL, th), lambda j: (layer, 0, j)),
                  pl.BlockSpec((None, D_MODEL, th), lambda j: (layer, 0, j)),
                  pl.BlockSpec((None, th, D_MODEL), lambda j: (layer, j, 0))] + tail_specs,
        out_specs=[pl.BlockSpec((tm, D_MODEL), lambda j: (0, 0)),
                   pl.BlockSpec((None, D_MODEL, 2 * th), lambda j: (j, 0, 0)),
                   pl.BlockSpec((th, D_MODEL), lambda j: (j, 0))],
        out_shape=[jax.ShapeDtypeStruct((tm, D_MODEL), jnp.float32),
                   jax.ShapeDtypeStruct((n_head_tiles, D_MODEL, 2 * th), bf),
                   jax.ShapeDtypeStruct((D_FF, D_MODEL), bf)],
        scratch_shapes=[pltpu.VMEM((tm, D_MODEL), bf)],
        compiler_params=cparams,
        name="ffn_head" + suffix,
    )(x, ln, wg, wu, wd, *tail_args)

    def row_tile(s):
        return jnp.where(s == 0, 0, (s - 1) // nj + 1)

    def ff_step(s):
        return jnp.where(s == 0, 0, (s - 1) % nj)

    side_specs, side_args, side_out_specs, side_out_shapes = [], [], [], []
    if side is not None:
        _, rows, cols = side.shape
        n_side = rows // SIDE_ROWS
        assert rows % SIDE_ROWS == 0 and n_side <= 1 + n_rest_tiles * nj

        def side_block(s):
            return jnp.minimum(s, n_side - 1)

        side_specs = [pl.BlockSpec((None, SIDE_ROWS, cols), lambda s: (layer, side_block(s), 0))]
        side_args = [side]
        side_out_specs = [pl.BlockSpec((SIDE_ROWS, cols), lambda s: (side_block(s), 0))]
        side_out_shapes = [jax.ShapeDtypeStruct((rows, cols), bf)]

    out = pl.pallas_call(
        functools.partial(_ffn_rest_kernel, final_norm=final_norm, side_cast=side is not None,
                          nj=nj),
        grid=(1 + n_rest_tiles * nj,),
        in_specs=[pl.BlockSpec((tm, D_MODEL), lambda s: (jnp.maximum(row_tile(s), 1), 0)),
                  vec_spec,
                  pl.BlockSpec((group, D_MODEL, 2 * th), lambda s: (ff_step(s), 0, 0)),
                  pl.BlockSpec((group * th, D_MODEL), lambda s: (ff_step(s), 0))] + tail_specs
        + [pl.BlockSpec(memory_space=pl.ANY)] + side_specs,
        out_specs=[pl.BlockSpec((tm, D_MODEL), lambda s: (row_tile(s), 0))] + side_out_specs,
        out_shape=[jax.ShapeDtypeStruct((TOKENS, D_MODEL), jnp.float32)] + side_out_shapes,
        scratch_shapes=[pltpu.VMEM((tm, D_MODEL), bf), pltpu.SemaphoreType.DMA((1,))],
        compiler_params=cparams,
        name="ffn_rest" + suffix,
    )(x, ln, wgu16, wd16, *tail_args, y0, *side_args)
    return out[0] if side is None else tuple(out)


PROJ_WIDTH = MLA_IN + MOBA_IN + DIL_IN


def _mla_prep(lat, kr, gq_ref, gkv_ref, wuq_ref, wuk_ref, wuv_ref, cos_ref, sin_ref,
              q_ref, k_ref, v_ref):
    scale = (MLA_NOPE + MLA_ROPE) ** -0.5 * LOG2E
    c = cos_ref[...]
    s = sin_ref[...]

    def rope(t):
        return t * c + pltpu.roll(t, LANES // 2, 1) * s

    nq = MLA_Q_RANK // LANES
    cq = _rms(jnp.concatenate(lat[:nq], axis=1), gq_ref[...]).astype(jnp.bfloat16)
    q = jnp.dot(cq, wuq_ref[...], preferred_element_type=jnp.float32)
    ckv = _rms(jnp.concatenate(lat[nq:], axis=1), gkv_ref[...]).astype(jnp.bfloat16)
    kn = jnp.dot(ckv, wuk_ref[...], preferred_element_type=jnp.float32)
    v_ref[...] = jnp.dot(ckv, wuv_ref[...],
                         preferred_element_type=jnp.float32).astype(jnp.bfloat16)
    kr = rope(kr).astype(jnp.bfloat16)
    for h in range(MLA_HEADS):
        b0 = h * MLA_QK_PAD
        q_ref[:, b0:b0 + LANES] = (q[:, b0:b0 + LANES] * scale).astype(jnp.bfloat16)
        q_ref[:, b0 + LANES:b0 + 2 * LANES] = (
            rope(q[:, b0 + LANES:b0 + 2 * LANES]) * scale).astype(jnp.bfloat16)
        k_ref[:, b0:b0 + LANES] = kn[:, h * LANES:(h + 1) * LANES].astype(jnp.bfloat16)
        k_ref[:, b0 + LANES:b0 + 2 * LANES] = kr


def _proj_kernel(x_ref, ln_ref, wt_ref, cos_ref, sin_ref, *rest):
    mla_args, (moba_ref, dil_ref) = rest[:-2], rest[-2:]
    h = _rms(x_ref[...], ln_ref[...]).astype(jnp.bfloat16)

    def proj(r0, rows):
        return _dot_nt(h, wt_ref[r0:r0 + rows, :])

    n_lat = MLA_Q_RANK + MLA_KV_RANK
    lat = []
    for r0 in range(0, n_lat, PROJ_TN):
        y = proj(r0, min(PROJ_TN, n_lat - r0))
        lat += [y[:, g0:g0 + LANES] for g0 in range(0, y.shape[1], LANES)]
    yk = proj(n_lat, LANES)
    lane = lax.broadcasted_iota(jnp.int32, yk.shape, 1)
    half = MLA_ROPE // 2
    kr = (jnp.where(lane < half, yk, 0.0)
          + jnp.where((lane >= LANES // 2) & (lane < LANES // 2 + half),
                      pltpu.roll(yk, half, 1), 0.0))
    _mla_prep(lat, kr, *mla_args)

    row = MLA_IN
    for o_ref, width in ((moba_ref, MOBA_IN), (dil_ref, DIL_IN)):
        for c0 in range(0, width, PROJ_TN):
            y = proj(row + c0, PROJ_TN)
            section = c0 // (width // 3)
            if section == 2:
                o_ref[:, c0:c0 + PROJ_TN] = y.astype(o_ref.dtype)
                continue
            c = cos_ref[section]
            s = sin_ref[section]
            for g0 in range(0, PROJ_TN, LANES):
                yg = y[:, g0:g0 + LANES]
                o_ref[:, c0 + g0:c0 + g0 + LANES] = (
                    yg * c + pltpu.roll(yg, LANES // 2, 1) * s).astype(o_ref.dtype)
        row += width


def _proj(x, ln, w, cos_tab, sin_tab, gq, gkv, wuq, wuk, wuv, cos_r, sin_r):
    tm = PROJ_TM
    pos_blocks = SEQ // tm
    qk_w = MLA_HEADS * MLA_QK_PAD
    v_w = MLA_HEADS * MLA_V
    bf = jnp.bfloat16

    def rows(width):
        return pl.BlockSpec((tm, width), lambda i: (i, 0))

    def full(shape):
        return pl.BlockSpec(shape, lambda i: (0, 0))

    def tab_spec():
        return pl.BlockSpec((2, tm, LANES), lambda i: (0, i % pos_blocks, 0))

    def rtab_spec():
        return pl.BlockSpec((tm, LANES), lambda i: (i % pos_blocks, 0))

    return pl.pallas_call(
        _proj_kernel,
        grid=(TOKENS // tm,),
        in_specs=[
            rows(D_MODEL),
            full((1, D_MODEL)),
            pl.BlockSpec((PROJ_WIDTH, D_MODEL), lambda i: (0, 0), pipeline_mode=pl.Buffered(1)),
            tab_spec(), tab_spec(),
            full((1, MLA_Q_RANK)), full((1, MLA_KV_RANK)),
            full((MLA_Q_RANK, qk_w)), full((MLA_KV_RANK, v_w)), full((MLA_KV_RANK, v_w)),
            rtab_spec(), rtab_spec(),
        ],
        out_specs=[rows(qk_w), rows(qk_w), rows(v_w), rows(MOBA_IN), rows(DIL_IN)],
        out_shape=[
            jax.ShapeDtypeStruct((TOKENS, qk_w), bf),
            jax.ShapeDtypeStruct((TOKENS, qk_w), bf),
            jax.ShapeDtypeStruct((TOKENS, v_w), bf),
            jax.ShapeDtypeStruct((TOKENS, MOBA_IN), bf),
            jax.ShapeDtypeStruct((TOKENS, DIL_IN), bf),
        ],
        compiler_params=_cparams(("parallel",), PROJ_VMEM_MIB),
        name="mix_proj",
    )(x, ln, w, cos_tab, sin_tab, gq, gkv, wuq, wuk, wuv, cos_r, sin_r)


def _fill_v_ones(v_ref, va_ref):
    width = v_ref.shape[1]
    va_ref[:, :width] = v_ref[...]
    va_ref[:, width:] = jnp.ones((v_ref.shape[0], va_ref.shape[1] - width), va_ref.dtype)


_V_ONES_SCRATCH = [pltpu.VMEM((SEQ, 2 * HEAD_DIM), jnp.bfloat16)]


def _mla_attn_kernel(q_ref, k_ref, v_ref, o_ref, va_ref):
    _fill_v_ones(v_ref, va_ref)
    t = MLA_T
    tri = _causal_tri(t)
    for c in reversed(range(SEQ // t)):
        n = (c + 1) * t
        s = _dot_nt(q_ref[_tile(c, t), :], k_ref[0:n, :])
        parts = [s[:, _tile(j, t)] for j in range(c)]
        parts.append(jnp.where(tri, s[:, _tile(c, t)], NEG_INF))
        o_ref[_tile(c, t), :] = _softmax_pv(parts, va_ref)


def _moba_attn_kernel(q_ref, k_ref, v_ref, o_ref, va_ref):
    _fill_v_ones(v_ref, va_ref)
    rid = lax.broadcasted_iota(jnp.int32, (LANES, HEAD_DIM), 0)
    km = jnp.zeros((LANES, HEAD_DIM), jnp.float32)
    for j in range(MOBA_NBLK):
        kj = k_ref[_tile(j), :].astype(jnp.float32)
        mean_j = jnp.sum(kj, axis=0, keepdims=True) * (1.0 / MOBA_BLOCK)
        km = jnp.where(rid == j, mean_j, km)
    km_hi = km.astype(jnp.bfloat16)
    km_lo = (km - km_hi.astype(jnp.float32)).astype(jnp.bfloat16)
    tri = _causal_tri()

    for c in ATT_ORDER:
        n = (c + 1) * ATT_T
        q = q_ref[_tile(c), :]
        s = _dot_nt(q, k_ref[0:n, :])
        parts = [s[:, _tile(j)] for j in range(c)]
        if c > MOBA_TOPK:
            gate = _dot_nt(q, km_hi) + _dot_nt(q, km_lo)
            lane = lax.broadcasted_iota(jnp.int32, gate.shape, 1)
            ahead = jnp.zeros(gate.shape, jnp.float32)
            for jp in range(c):
                cj = _lane_col(gate, jp, lane)
                wins = (cj > gate) | ((cj == gate) & (lane > jp))
                ahead = ahead + jnp.where(wins, 1.0, 0.0)
            sel = jnp.where(ahead < MOBA_TOPK, 1.0, 0.0)
            parts = [jnp.where(_lane_col(sel, j, lane) > 0.5, parts[j], NEG_INF)
                     for j in range(c)]
        parts.append(jnp.where(tri, s[:, _tile(c)], NEG_INF))
        o_ref[_tile(c), :] = _softmax_pv(parts, va_ref)


def _mla_moba_attn_kernel(qm_ref, km_ref, vm_ref, q_ref, k_ref, v_ref, om_ref, o_ref,
                          vam_ref, va_ref):
    _moba_attn_kernel(q_ref, k_ref, v_ref, o_ref, va_ref)
    _mla_attn_kernel(qm_ref, km_ref, vm_ref, om_ref, vam_ref)


def _mla_moba_attn(q_mla, k_mla, v_mla, qkv):
    assert MLA_HEADS == MOBA_HEADS
    return pl.pallas_call(
        _mla_moba_attn_kernel,
        grid=(BATCH, MLA_HEADS),
        in_specs=[_head_spec(MLA_QK_PAD, 0), _head_spec(MLA_QK_PAD, 0), _head_spec(MLA_V, 0),
                  _head_spec(HEAD_DIM, 0), _head_spec(HEAD_DIM, MOBA_HEADS),
                  _head_spec(HEAD_DIM, 2 * MOBA_HEADS)],
        out_specs=[_head_spec(MLA_V, 0), _head_spec(HEAD_DIM, 0)],
        out_shape=[jax.ShapeDtypeStruct((TOKENS, MLA_HEADS * MLA_V), jnp.bfloat16),
                   jax.ShapeDtypeStruct((TOKENS, MOBA_HEADS * HEAD_DIM), jnp.bfloat16)],
        scratch_shapes=_V_ONES_SCRATCH + _V_ONES_SCRATCH,
        compiler_params=_cparams(("parallel", "parallel"), ATT_VMEM_MIB),
        name="mla_moba_attn",
    )(q_mla, k_mla, v_mla, qkv, qkv, qkv)


def _dil_tables():
    t = DIL_T
    r = np.arange(t)[:, None]
    c = np.arange(t)[None, :]
    cnts = []
    for d in range(SEQ // t):
        delta = r - c + t * d
        cnts.append(sum(((delta >= 0) & (delta <= w) & (delta % dil == 0)).astype(np.float64)
                        for w, dil in DIL_PATTERNS))
    cnt = np.stack(cnts)
    return np.where(cnt > 0, np.log2(np.maximum(cnt, 1.0)), NEG_INF).astype(np.float32)


def _dil_attn_kernel(q_ref, k_ref, v_ref, bias_ref, side_ref, o_ref, side16_ref, va_ref):
    side16_ref[...] = side_ref[...].astype(side16_ref.dtype)
    for hh in range(DIL_HPS):
        hs = slice(hh * HEAD_DIM, (hh + 1) * HEAD_DIM)
        va = va_ref.at[hh]
        va[:, :HEAD_DIM] = v_ref[:, hs]
        va[:, HEAD_DIM:] = jnp.ones((SEQ, HEAD_DIM), va_ref.dtype)
    t = DIL_T
    for c in reversed(range(SEQ // t)):
        n = (c + 1) * t
        for hh in range(DIL_HPS):
            hs = slice(hh * HEAD_DIM, (hh + 1) * HEAD_DIM)
            s = _dot_nt(q_ref[_tile(c, t), hs], k_ref[0:n, hs])
            tiles = [s[:, _tile(j, t)] + bias_ref[c - j] for j in range(c + 1)]
            o_ref[_tile(c, t), hs] = _softmax_pv(tiles, va_ref.at[hh])


def _dil_attn(qkv, w_side, layer):
    bias = _dil_tables()
    groups = DIL_HEADS // DIL_HPS
    _, rows, cols = w_side.shape
    side_rows = rows // (BATCH * groups)
    assert rows == side_rows * BATCH * groups and side_rows % 16 == 0
    return pl.pallas_call(
        _dil_attn_kernel,
        grid=(BATCH, groups),
        in_specs=[_head_spec(DIL_HPS * HEAD_DIM, 0),
                  _head_spec(DIL_HPS * HEAD_DIM, groups),
                  _head_spec(DIL_HPS * HEAD_DIM, 2 * groups),
                  pl.BlockSpec(bias.shape, lambda b, h: (0, 0, 0)),
                  pl.BlockSpec((None, side_rows, cols), lambda b, h: (layer, b * groups + h, 0))],
        out_specs=[_head_spec(DIL_HPS * HEAD_DIM, 0),
                   pl.BlockSpec((side_rows, cols), lambda b, h: (b * groups + h, 0))],
        out_shape=[jax.ShapeDtypeStruct((TOKENS, DIL_HEADS * HEAD_DIM), jnp.bfloat16),
                   jax.ShapeDtypeStruct((rows, cols), jnp.bfloat16)],
        scratch_shapes=[pltpu.VMEM((DIL_HPS, SEQ, 2 * HEAD_DIM), jnp.bfloat16)],
        compiler_params=_cparams(("parallel", "parallel"), ATT_VMEM_MIB),
        name="dil_attn",
    )(qkv, qkv, qkv, jnp.asarray(bias), w_side)


def _out_proj_kernel(x_ref, mla_ref, moba_ref, dil_ref, w_ref, o_ref):
    mix = jnp.concatenate([mla_ref[...], moba_ref[...], dil_ref[...]], axis=1)
    o_ref[...] = x_ref[...] + jnp.dot(mix, w_ref[...], preferred_element_type=jnp.float32)


def _out_proj(x, o_mla, o_moba, o_dil, w_out):
    tm = OUT_TM

    def rows(width):
        return pl.BlockSpec((tm, width), lambda i: (i, 0))

    return pl.pallas_call(
        _out_proj_kernel,
        grid=(TOKENS // tm,),
        in_specs=[rows(D_MODEL), rows(MLA_HEADS * MLA_V), rows(MOBA_HEADS * HEAD_DIM),
                  rows(DIL_HEADS * HEAD_DIM),
                  pl.BlockSpec((MIX_WIDTH, D_MODEL), lambda i: (0, 0),
                               pipeline_mode=pl.Buffered(1))],
        out_specs=rows(D_MODEL),
        out_shape=jax.ShapeDtypeStruct((TOKENS, D_MODEL), jnp.float32),
        compiler_params=_cparams(("parallel",), PROJ_VMEM_MIB),
        name="out_proj",
    )(x, o_mla, o_moba, o_dil, w_out)


def _rope_tables():
    pos = np.arange(SEQ, dtype=np.float64)[:, None]
    inv_h = ROPE_THETA ** (-np.arange(0, HEAD_DIM, 2, dtype=np.float64) / HEAD_DIM)
    ang = pos * inv_h[None, :]
    cos_h = np.concatenate([np.cos(ang), np.cos(ang)], axis=1)
    sin_h = np.concatenate([-np.sin(ang), np.sin(ang)], axis=1)
    scale = HEAD_DIM ** -0.5 * LOG2E
    cos_qk = np.stack([cos_h * scale, cos_h])
    sin_qk = np.stack([sin_h * scale, sin_h])
    inv_r = ROPE_THETA ** (-np.arange(0, MLA_ROPE, 2, dtype=np.float64) / MLA_ROPE)
    ang_r = pos * inv_r[None, :]
    z = np.zeros_like(ang_r)
    cos_r = np.concatenate([np.cos(ang_r), z, np.cos(ang_r), z], axis=1)
    sin_r = np.concatenate([-np.sin(ang_r), z, np.sin(ang_r), z], axis=1)
    return tuple(jnp.asarray(t, jnp.float32) for t in (cos_qk, sin_qk, cos_r, sin_r))


def _pad_rope_cols(w):
    half = MLA_ROPE // 2
    z = jnp.zeros(w.shape[:-1] + (half,), w.dtype)
    return jnp.concatenate([w[..., :half], z, w[..., half:], z], axis=-1)


def _prep_mla_weights(w_uq, w_uk, w_uv):
    bf = jnp.bfloat16
    uq = w_uq.reshape(MLA_Q_RANK, MLA_HEADS, MLA_NOPE + MLA_ROPE)
    uq = jnp.concatenate([uq[..., :MLA_NOPE], _pad_rope_cols(uq[..., MLA_NOPE:])], axis=-1)
    uq = uq.reshape(MLA_Q_RANK, MLA_HEADS * MLA_QK_PAD).astype(bf)
    return uq, w_uk.astype(bf), w_uv.astype(bf)


def kernel(x, ln_ffn1, w_ffn1_gate, w_ffn1_up, w_ffn1_down, ln_mix, w_in, g_mla_q, g_mla_kv,
           w_mla_uq, w_mla_uk, w_mla_uv, w_out, ln_ffn2, w_ffn2_gate, w_ffn2_up, w_ffn2_down,
           ln_final):
    cos_qk, sin_qk, cos_r, sin_r = _rope_tables()
    w_in_t = jnp.swapaxes(w_in, 1, 2)
    xt = x.reshape(TOKENS, D_MODEL)
    for l in range(DEPTH):
        uq, uk, uv = _prep_mla_weights(w_mla_uq[l], w_mla_uk[l], w_mla_uv[l])
        xt, w_proj = _ffn(xt, ln_ffn1[l][None], w_ffn1_gate, w_ffn1_up, w_ffn1_down, l,
                          side=w_in_t)
        q_mla, k_mla, v_mla, qkv_moba, qkv_dil = _proj(
            xt, ln_mix[l][None], w_proj, cos_qk, sin_qk,
            g_mla_q[l][None], g_mla_kv[l][None], uq, uk, uv, cos_r, sin_r)
        o_mla, o_moba = _mla_moba_attn(q_mla, k_mla, v_mla, qkv_moba)
        o_dil, w_out16 = _dil_attn(qkv_dil, w_out, l)
        xt = _out_proj(xt, o_mla, o_moba, o_dil, w_out16)
        xt = _ffn(xt, ln_ffn2[l][None], w_ffn2_gate, w_ffn2_up, w_ffn2_down, l,
                  ln_final=ln_final[None] if l == DEPTH - 1 else None)
    return xt.reshape(BATCH, SEQ, D_MODEL)
```
